```python
import math
import jax, jax.numpy as jnp
from jax import lax
import numpy as np

D_MODEL = 1024
BATCH = 4
SEQ = 8192
DEPTH = 4

D_MIX = D_MODEL
N_MIXERS = 4
GROUP_WIDTH = D_MIX // N_MIXERS
HEAD_DIM = 64
N_HEADS = GROUP_WIDTH // HEAD_DIM
ATTN_BLOCK = 128
SG_CHUNK = 128
SSM_CHUNK = 128
SSM_GROUPS = 2
SSM_HEADS_PER_GROUP = N_HEADS // SSM_GROUPS
SSM_STATE = 128
CONV_WIDTH = 4
SSM_XBC = GROUP_WIDTH + 2 * SSM_GROUPS * SSM_STATE
MLSTM_CHUNK = 128
MLSTM_M_INIT = -1e30
N_EXPERT_GROUPS = 4
EXPERTS_PER_GROUP = 8
N_EXPERTS = N_EXPERT_GROUPS * EXPERTS_PER_GROUP
TOP_K = 2
D_EXPERT = 512
MOE_BLOCK = 256
NORM_EPS = 1e-6
ATTN_COLS = 3 * GROUP_WIDTH + N_HEADS
SG_COLS = 2 * GROUP_WIDTH
SSM_COLS = GROUP_WIDTH + SSM_XBC + N_HEADS
MLSTM_COLS = 4 * GROUP_WIDTH + 2 * N_HEADS
D_IN_PROJ = ATTN_COLS + SG_COLS + SSM_COLS + MLSTM_COLS

kernel_name = 'hybrid_fox_gmlp_ssd_mlstm_hmoe'


def rms_norm(x, gain):
    xf = x.astype(jnp.float32)
    y = xf * lax.rsqrt(jnp.mean(xf * xf, axis=-1, keepdims=True) + NORM_EPS)
    return (y * gain.astype(jnp.float32)).astype(x.dtype)


def split_columns(x, widths):
    cuts = [int(i) for i in np.cumsum(widths)[:-1]]
    return jnp.split(x, cuts, axis=-1)


def causal_depthwise_conv(x, w, b):
    width, ch = w.shape
    y = lax.conv_general_dilated(x, w[:, None, :].astype(x.dtype), window_strides=(1,),
                                 padding=((width - 1, 0),),
                                 dimension_numbers=('NWC', 'WIO', 'NWC'),
                                 feature_group_count=ch)
    return y + b.astype(y.dtype)


def forgetting_attention(q, k, v, f_logit):
    Bb, S, H, Dh = q.shape
    L = ATTN_BLOCK
    qf = q.astype(jnp.float32) * (Dh ** -0.5)
    kf = k.astype(jnp.float32)
    vf = v.astype(jnp.float32)
    cum_logf = jnp.cumsum(jax.nn.log_sigmoid(f_logit.astype(jnp.float32)), axis=1).transpose(0, 2, 1)
    diag = jnp.tril(jnp.ones((L, L), bool))
    outs = []
    for blk in range(S // L):
        q0, q1 = blk * L, (blk + 1) * L
        logits = jnp.einsum('bqhd,bkhd->bhqk', qf[:, q0:q1], kf[:, :q1])
        logits = logits + cum_logf[:, :, q0:q1, None] - cum_logf[:, :, None, :q1]
        mask = jnp.concatenate([jnp.ones((L, q0), bool), diag], axis=1)
        probs = jax.nn.softmax(jnp.where(mask, logits, -jnp.inf), axis=-1)
        outs.append(jnp.einsum('bhqk,bkhd->bqhd', probs, vf[:, :q1]))
    return jnp.concatenate(outs, axis=1)


def spatial_gating(u, v, v_gain, w_s, b_s):
    Bb, S, H, Dh = u.shape
    L = SG_CHUNK
    u = jax.nn.gelu(u.astype(jnp.float32))
    v = rms_norm(jax.nn.gelu(v.astype(jnp.float32)), v_gain)
    causal = jnp.tril(jnp.ones((L, L), bool))
    w = jnp.where(causal[None], w_s.astype(jnp.float32), 0.0)
    vc = v.reshape(Bb, S // L, L, H, Dh)
    mixed = jnp.einsum('hts,bcshd->bcthd', w, vc) + b_s.astype(jnp.float32).T[None, None, :, :, None]
    return u * mixed.reshape(Bb, S, H, Dh)


def ssd_chunked(x, a, b, c):
    Bb, S, G, R, P = x.shape
    L = SSM_CHUNK
    nc = S // L
    x = x.reshape(Bb, nc, L, G, R, P)
    a = a.reshape(Bb, nc, L, G, R)
    b = b.reshape(Bb, nc, L, G, -1)
    c = c.reshape(Bb, nc, L, G, -1)
    a_cum = jnp.cumsum(a, axis=2)
    causal = jnp.tril(jnp.ones((L, L), bool))[None, None, :, :, None, None]
    seg = a_cum[:, :, :, None] - a_cum[:, :, None, :]
    decay = jnp.exp(jnp.where(causal, seg, -jnp.inf))
    scores = jnp.einsum('bctgn,bcsgn->bctsg', c, b)
    y_diag = jnp.einsum('bctsgr,bcsgrp->bctgrp', scores[..., None] * decay, x)
    decay_to_end = jnp.exp(a_cum[:, :, -1:] - a_cum)
    states = jnp.einsum('bcsgn,bcsgrp->bcgrpn', b, x * decay_to_end[..., None])
    chunk_decay = jnp.exp(a_cum[:, :, -1])

    def carry_state(h, inp):
        st, dc = inp
        return dc[..., None, None] * h + st, h

    h0 = jnp.zeros((Bb,) + states.shape[2:], jnp.float32)
    _, h_in = lax.scan(carry_state, h0, (jnp.moveaxis(states, 1, 0), jnp.moveaxis(chunk_decay, 1, 0)))
    h_in = jnp.moveaxis(h_in, 0, 1)
    y_off = jnp.einsum('bctgn,bcgrpn->bctgrp', c, h_in) * jnp.exp(a_cum)[..., None]
    return (y_diag + y_off).reshape(Bb, S, G, R, P)


def mamba2_mixer(z, xbc, dt_raw, conv_w, conv_b, dt_bias, a_log, d_skip):
    Bb, S, _ = xbc.shape
    G, R = SSM_GROUPS, SSM_HEADS_PER_GROUP
    xbc = jax.nn.silu(causal_depthwise_conv(xbc, conv_w, conv_b)).astype(jnp.float32)
    xs, bm, cm = jnp.split(xbc, [GROUP_WIDTH, GROUP_WIDTH + SSM_GROUPS * SSM_STATE], axis=-1)
    xs = xs.reshape(Bb, S, G, R, HEAD_DIM)
    bm = bm.reshape(Bb, S, G, SSM_STATE)
    cm = cm.reshape(Bb, S, G, SSM_STATE)
    dt = jax.nn.softplus(dt_raw.astype(jnp.float32) + dt_bias.astype(jnp.float32)).reshape(Bb, S, G, R)
    a = -jnp.exp(a_log.astype(jnp.float32)).reshape(G, R)
    y = ssd_chunked(xs * dt[..., None], dt * a, bm, cm)
    y = y + d_skip.astype(jnp.float32).reshape(G, R)[..., None] * xs
    y = y.reshape(Bb, S, N_HEADS, HEAD_DIM)
    return y * jax.nn.silu(z.astype(jnp.float32)).reshape(Bb, S, N_HEADS, HEAD_DIM)


def mlstm_chunkwise(q, k, v, i_pre, f_pre):
    Bb, S, H, Dh = q.shape
    L = MLSTM_CHUNK
    nc = S // L
    q = q.astype(jnp.float32).reshape(Bb, nc, L, H, Dh)
    k = (k.astype(jnp.float32) * (Dh ** -0.5)).reshape(Bb, nc, L, H, Dh)
    v = v.astype(jnp.float32).reshape(Bb, nc, L, H, Dh)
    log_f = jax.nn.log_sigmoid(f_pre.astype(jnp.float32)).reshape(Bb, nc, L, H)
    ig = i_pre.astype(jnp.float32).reshape(Bb, nc, L, H)
    a = jnp.cumsum(log_f, axis=2)
    a_end = a[:, :, -1]
    g = a_end[:, :, None] - a + ig

    def carry_state(carry, inp):
        C, n, m = carry
        kc, vc, gc, ac = inp
        m_new = jnp.maximum(ac + m, gc.max(axis=1))
        w_s = jnp.exp(gc - m_new[:, None, :])
        scale = jnp.exp(ac + m - m_new)
        C_new = scale[..., None, None] * C + jnp.einsum('blhd,blhe->bhde', vc * w_s[..., None], kc)
        n_new = scale[..., None] * n + jnp.einsum('blh,blhe->bhe', w_s, kc)
        return (C_new, n_new, m_new), (C, n, m)

    init = (jnp.zeros((Bb, H, Dh, Dh), jnp.float32), jnp.zeros((Bb, H, Dh), jnp.float32),
            jnp.full((Bb, H), MLSTM_M_INIT, jnp.float32))
    _, (C_in, n_in, m_in) = lax.scan(carry_state, init, (jnp.moveaxis(k, 1, 0), jnp.moveaxis(v, 1, 0),
                                                         jnp.moveaxis(g, 1, 0), jnp.moveaxis(a_end, 1, 0)))
    C_in = jnp.moveaxis(C_in, 0, 1)
    n_in = jnp.moveaxis(n_in, 0, 1)
    m_in = jnp.moveaxis(m_in, 0, 1)
    causal = jnp.tril(jnp.ones((L, L), bool))[None, None, :, :, None]
    log_d = a[:, :, :, None, :] - a[:, :, None, :, :] + ig[:, :, None, :, :]
    log_d = jnp.where(causal, log_d, -jnp.inf)
    log_inter = a + m_in[:, :, None, :]
    m_t = jnp.maximum(log_d.max(axis=3), log_inter)
    w = jnp.einsum('bcthd,bcshd->bctsh', q, k) * jnp.exp(log_d - m_t[:, :, :, None, :])
    inter = jnp.exp(log_inter - m_t)
    num = jnp.einsum('bctsh,bcshd->bcthd', w, v) + inter[..., None] * jnp.einsum('bchde,bcthe->bcthd', C_in, q)
    den = w.sum(axis=3) + inter * jnp.einsum('bche,bcthe->bcth', n_in, q)
    h = num / jnp.maximum(jnp.abs(den), jnp.exp(-m_t))[..., None]
    return h.reshape(Bb, S, H, Dh)


def hybrid_mixer(h, w_in, w_out, mix_gain, attn_f_bias, sg_w, sg_b, conv_w, conv_b,
                 dt_bias, a_log, d_skip, i_bias, f_bias):
    Bb, S, _ = h.shape
    heads = lambda t: t.reshape(Bb, S, N_HEADS, HEAD_DIM)
    gw = GROUP_WIDTH
    proj = h @ w_in
    (aq, ak, av, af, su, sv, mz, mxbc, mdt, lq, lk, lv, lo, li, lf) = split_columns(
        proj, [gw, gw, gw, N_HEADS, gw, gw, gw, SSM_XBC, N_HEADS, gw, gw, gw, gw, N_HEADS, N_HEADS])
    g_attn, g_sg, g_ssm, g_ml = [t.reshape(N_HEADS, HEAD_DIM) for t in jnp.split(mix_gain, N_MIXERS)]
    y_attn = rms_norm(forgetting_attention(heads(aq), heads(ak), heads(av),
                                           af.astype(jnp.float32) + attn_f_bias.astype(jnp.float32)), g_attn)
    y_sg = spatial_gating(heads(su), heads(sv), g_sg, sg_w, sg_b)
    y_ssm = rms_norm(mamba2_mixer(mz, mxbc, mdt, conv_w, conv_b, dt_bias, a_log, d_skip), g_ssm)
    y_ml = jax.nn.sigmoid(heads(lo).astype(jnp.float32)) * rms_norm(
        mlstm_chunkwise(heads(lq), heads(lk), heads(lv),
                        li.astype(jnp.float32) + i_bias.astype(jnp.float32),
                        lf.astype(jnp.float32) + f_bias.astype(jnp.float32)), g_ml)
    y = jnp.concatenate([y_attn, y_sg, y_ssm, y_ml], axis=2).reshape(Bb, S, D_MIX).astype(h.dtype)
    return y @ w_out


def hierarchical_moe(h, w_rg, b_rg, w_re, b_re, w_gate, w_up, w_down):
    Bb, S, D = h.shape
    tokens = h.reshape(-1, D)
    T = tokens.shape[0]
    rows = jnp.arange(T)
    g_logits = (tokens @ w_rg).astype(jnp.float32) + b_rg.astype(jnp.float32)
    g_sel = jnp.argmax(g_logits, axis=-1)
    g_prob = jax.nn.softmax(g_logits, axis=-1)[rows, g_sel]
    e_logits = ((tokens @ w_re).astype(jnp.float32) + b_re.astype(jnp.float32)).reshape(
        T, N_EXPERT_GROUPS, EXPERTS_PER_GROUP)[rows, g_sel]
    top_logit, top_local = lax.top_k(e_logits, TOP_K)
    gate = g_prob[:, None] * jax.nn.softmax(top_logit, axis=-1)
    expert = g_sel[:, None] * EXPERTS_PER_GROUP + top_local
    M = T * TOP_K
    e_flat = expert.reshape(-1).astype(jnp.int32)
    tok_flat = (jnp.arange(M) // TOP_K).astype(jnp.int32)
    w_flat = gate.reshape(-1)
    order = jnp.argsort(e_flat)
    e_sorted = e_flat[order]
    counts = jnp.bincount(e_flat, length=N_EXPERTS)
    starts = jnp.cumsum(counts) - counts
    padded = ((counts + MOE_BLOCK - 1) // MOE_BLOCK) * MOE_BLOCK
    p_ends = jnp.cumsum(padded)
    p_starts = p_ends - padded
    dest = p_starts[e_sorted] + (jnp.arange(M) - starts[e_sorted])
    n_blocks = -(-M // MOE_BLOCK) + N_EXPERTS
    P = n_blocks * MOE_BLOCK
    row_tok = jnp.full((P,), T, jnp.int32).at[dest].set(tok_flat[order])
    row_w = jnp.zeros((P,), jnp.float32).at[dest].set(w_flat[order])
    block_e = jnp.minimum(jnp.searchsorted(p_ends, jnp.arange(n_blocks) * MOE_BLOCK, side='right'),
                          N_EXPERTS - 1)
    tokens_pad = jnp.concatenate([tokens, jnp.zeros((1, D), tokens.dtype)], axis=0)
    xb = tokens_pad[row_tok].reshape(n_blocks, MOE_BLOCK, D)

    def expert_block(args):
        xe, e = args
        return (jax.nn.silu(xe @ w_gate[e]) * (xe @ w_up[e])) @ w_down[e]

    yb = lax.map(expert_block, (xb, block_e)).reshape(P, D)
    y = jnp.zeros((T + 1, D), jnp.float32).at[row_tok].add(yb.astype(jnp.float32) * row_w[:, None])
    return y[:T].reshape(Bb, S, D).astype(h.dtype)


def setup_inputs(seed: int = 0) -> dict:
    key = jax.random.key(seed)
    ks = jax.random.split(key, 32)
    nrm = lambda k, shape, s: s * jax.random.normal(k, shape, jnp.float32)
    x = nrm(ks[0], (BATCH, SEQ, D_MODEL), 1.0)
    c = nrm(ks[1], (BATCH, D_MODEL), 1.0)
    w_in = nrm(ks[2], (DEPTH, D_MODEL, D_IN_PROJ), D_MODEL ** -0.5)
    w_out = nrm(ks[3], (DEPTH, D_MIX, D_MODEL), D_MIX ** -0.5)
    w_mix_norm = 1.0 + nrm(ks[4], (DEPTH, D_MIX), 0.1)
    attn_f_bias = jnp.linspace(2.0, 6.0, N_HEADS)[None, :] + nrm(ks[5], (DEPTH, N_HEADS), 0.1)
    sg_w = nrm(ks[6], (DEPTH, N_HEADS, SG_CHUNK, SG_CHUNK), 0.5 * SG_CHUNK ** -0.5)
    sg_b = 1.0 + nrm(ks[7], (DEPTH, N_HEADS, SG_CHUNK), 0.1)
    ssm_conv_w = nrm(ks[8], (DEPTH, CONV_WIDTH, SSM_XBC), CONV_WIDTH ** -0.5)
    ssm_conv_b = nrm(ks[9], (DEPTH, SSM_XBC), 0.02)
    dt0 = jnp.exp(jax.random.uniform(ks[10], (DEPTH, N_HEADS), jnp.float32,
                                     minval=math.log(1e-3), maxval=math.log(1e-1)))
    ssm_dt_bias = dt0 + jnp.log(-jnp.expm1(-dt0))
    ssm_a_log = jnp.log(jax.random.uniform(ks[11], (DEPTH, N_HEADS), jnp.float32, minval=1.0, maxval=16.0))
    ssm_d = 1.0 + nrm(ks[12], (DEPTH, N_HEADS), 0.1)
    mlstm_i_bias = nrm(ks[13], (DEPTH, N_HEADS), 0.1)
    mlstm_f_bias = jnp.linspace(3.0, 6.0, N_HEADS)[None, :] + nrm(ks[14], (DEPTH, N_HEADS), 0.1)
    w_ada = nrm(ks[15], (DEPTH, D_MODEL, 6 * D_MODEL), 0.5 * D_MODEL ** -0.5)
    b_ada = nrm(ks[16], (DEPTH, 6 * D_MODEL), 0.02)
    w_norm1 = 1.0 + nrm(ks[17], (DEPTH, D_MODEL), 0.1)
    w_norm2 = 1.0 + nrm(ks[18], (DEPTH, D_MODEL), 0.1)
    w_router_group = nrm(ks[19], (DEPTH, D_MODEL, N_EXPERT_GROUPS), D_MODEL ** -0.5)
    b_router_group = nrm(ks[20], (DEPTH, N_EXPERT_GROUPS), 0.01)
    w_router_expert = nrm(ks[21], (DEPTH, D_MODEL, N_EXPERTS), D_MODEL ** -0.5)
    b_router_expert = nrm(ks[22], (DEPTH, N_EXPERTS), 0.01)
    w_expert_gate = nrm(ks[23], (DEPTH, N_EXPERTS, D_MODEL, D_EXPERT), D_MODEL ** -0.5)
    w_expert_up = nrm(ks[24], (DEPTH, N_EXPERTS, D_MODEL, D_EXPERT), D_MODEL ** -0.5)
    w_expert_down = nrm(ks[25], (DEPTH, N_EXPERTS, D_EXPERT, D_MODEL), D_EXPERT ** -0.5)
    w_norm_final = 1.0 + nrm(ks[26], (D_MODEL,), 0.1)
    return {'x': x, 'c': c, 'w_in': w_in, 'w_out': w_out, 'w_mix_norm': w_mix_norm,
            'attn_f_bias': attn_f_bias, 'sg_w': sg_w, 'sg_b': sg_b,
            'ssm_conv_w': ssm_conv_w, 'ssm_conv_b': ssm_conv_b, 'ssm_dt_bias': ssm_dt_bias,
            'ssm_a_log': ssm_a_log, 'ssm_d': ssm_d, 'mlstm_i_bias': mlstm_i_bias,
            'mlstm_f_bias': mlstm_f_bias, 'w_ada': w_ada, 'b_ada': b_ada,
            'w_norm1': w_norm1, 'w_norm2': w_norm2,
            'w_router_group': w_router_group, 'b_router_group': b_router_group,
            'w_router_expert': w_router_expert, 'b_router_expert': b_router_expert,
            'w_expert_gate': w_expert_gate, 'w_expert_up': w_expert_up,
            'w_expert_down': w_expert_down, 'w_norm_final': w_norm_final}


def reference(x, c, w_in, w_out, w_mix_norm, attn_f_bias, sg_w, sg_b, ssm_conv_w, ssm_conv_b,
              ssm_dt_bias, ssm_a_log, ssm_d, mlstm_i_bias, mlstm_f_bias, w_ada, b_ada,
              w_norm1, w_norm2, w_router_group, b_router_group, w_router_expert, b_router_expert,
              w_expert_gate, w_expert_up, w_expert_down, w_norm_final):
    cond = jax.nn.silu(c)
    for layer in range(DEPTH):
        mod = cond @ w_ada[layer] + b_ada[layer]
        shift1, scale1, gate1, shift2, scale2, gate2 = jnp.split(mod[:, None, :], 6, axis=-1)
        h = rms_norm(x, w_norm1[layer]) * (1.0 + scale1) + shift1
        x = x + gate1 * hybrid_mixer(h, w_in[layer], w_out[layer], w_mix_norm[layer], attn_f_bias[layer],
                                     sg_w[layer], sg_b[layer], ssm_conv_w[layer], ssm_conv_b[layer],
                                     ssm_dt_bias[layer], ssm_a_log[layer], ssm_d[layer],
                                     mlstm_i_bias[layer], mlstm_f_bias[layer])
        h = rms_norm(x, w_norm2[layer]) * (1.0 + scale2) + shift2
        x = x + gate2 * hierarchical_moe(h, w_router_group[layer], b_router_group[layer],
                                         w_router_expert[layer], b_router_expert[layer],
                                         w_expert_gate[layer], w_expert_up[layer], w_expert_down[layer])
    return rms_norm(x, w_norm_final)
```

```python
import functools

import jax
import jax.numpy as jnp
from jax import lax
from jax.experimental import pallas as pl
from jax.experimental.pallas import tpu as pltpu

F32 = jnp.float32
BF16 = jnp.bfloat16
I32 = jnp.int32

LANES = 128
SUBLANES = 8
HEAD_DIM = 64
N_HEADS = 4
GROUP_WIDTH = 256
CHUNK = 128
SSM_GROUPS = 2
CONV_WIDTH = 4
N_EXPERT_GROUPS = 4
EXPERTS_PER_GROUP = 8
N_EXPERTS = N_EXPERT_GROUPS * EXPERTS_PER_GROUP
TOP_K = 2
MOE_BLOCK = 256
NORM_EPS = 1e-6
MLSTM_M_INIT = -1e30
NEG_INF = float("-inf")
VMEM_LIMIT_BYTES = 48 * 1024 * 1024
IDX_CHUNK = 1024

(PB_Q, PB_V, PB_SU, PB_SV, PB_MZ, PB_X, PB_B, PB_C, PB_LQ, PB_LK, PB_LV, PB_LO) = range(12)
PROJ_COLS = 12 * GROUP_WIDTH
GATE_AF, GATE_DT, GATE_LI, GATE_LF = 0, 4, 8, 12
ROUTE_G, ROUTE_E = 0, 8

NT_DIMS = (((1,), (1,)), ((), ()))


def _cparams(*sem):
    return pltpu.CompilerParams(dimension_semantics=sem, vmem_limit_bytes=VMEM_LIMIT_BYTES)


def _dot(a, b):
    return jnp.dot(a, b, preferred_element_type=F32)


def _dot_nt(a, b):
    return lax.dot_general(a, b, NT_DIMS, preferred_element_type=F32)


def _dot_exact(a, b):
    return jnp.dot(a, b, preferred_element_type=F32, precision=lax.Precision.HIGHEST)


def _head_of_lane(shape, axis=1):
    return lax.broadcasted_iota(I32, shape, axis) // HEAD_DIM


def _sigmoid(x):
    return 1.0 / (1.0 + jnp.exp(-x))


def _silu(x):
    return x * _sigmoid(x)


def _log_sigmoid(x):
    return jnp.minimum(x, 0.0) - jnp.log1p(jnp.exp(-jnp.abs(x)))


def _softplus(x):
    return jnp.maximum(x, 0.0) + jnp.log1p(jnp.exp(-jnp.abs(x)))


def _gelu_tanh(x):
    return 0.5 * x * (1.0 + jnp.tanh(0.7978845608028654 * (x + 0.044715 * (x * x * x))))


def _expand_heads(cols, width=GROUP_WIDTH):
    rows = cols[0].shape[0]
    head = _head_of_lane((rows, width))
    out = jnp.broadcast_to(cols[N_HEADS - 1], (rows, width))
    for h in range(N_HEADS - 2, -1, -1):
        out = jnp.where(head == h, jnp.broadcast_to(cols[h], (rows, width)), out)
    return out


def _mask_head(x, h):
    return jnp.where(_head_of_lane(x.shape) == h, x, jnp.zeros_like(x))


def _head_rms_norm(y, gain):
    head = _head_of_lane(y.shape)
    sq = y * y
    cols = [jnp.sum(jnp.where(head == h, sq, 0.0), axis=1, keepdims=True) * (1.0 / HEAD_DIM)
            for h in range(N_HEADS)]
    return y * lax.rsqrt(_expand_heads(cols) + NORM_EPS) * gain


def _rms_norm(x, gain):
    ms = jnp.mean(x * x, axis=1, keepdims=True)
    return x * lax.rsqrt(ms + NORM_EPS) * gain


def _tril(n, strict=False):
    r = lax.broadcasted_iota(I32, (n, n), 0)
    c = lax.broadcasted_iota(I32, (n, n), 1)
    return (r > c) if strict else (r >= c)


def _ada_kernel(c_ref, w_ref, b_ref, o_ref):
    o_ref[0] = _dot_exact(_silu(c_ref[...]), w_ref[0]) + b_ref[0]


def _ada_modulation(c, w_ada, b_ada):
    depth, d, d6 = w_ada.shape
    bsz = c.shape[0]
    return pl.pallas_call(
        _ada_kernel,
        grid=(depth, d6 // d),
        in_specs=[pl.BlockSpec((bsz, d), lambda l, j: (0, 0)),
                  pl.BlockSpec((1, d, d), lambda l, j: (l, 0, j)),
                  pl.BlockSpec((1, 1, d), lambda l, j: (l, 0, j))],
        out_specs=pl.BlockSpec((1, bsz, d), lambda l, j: (l, 0, j)),
        out_shape=jax.ShapeDtypeStruct((depth, bsz, d6), F32),
        compiler_params=_cparams("arbitrary", "arbitrary"),
        name="ada_modulation",
    )(c, w_ada, b_ada.reshape(depth, 1, d6))


def _inproj_kernel(x_ref, mod_ref, wn_ref, wmain_ref, wkt_ref, wg_ref,
                   proj_ref, kt_ref, gates_ref, h_scr):
    mod = mod_ref[0]
    h = _rms_norm(x_ref[...], wn_ref[...]) * (1.0 + mod[1:2, :]) + mod[0:1, :]
    h_scr[...] = h.astype(BF16)
    for c0 in range(0, PROJ_COLS, GROUP_WIDTH):
        proj_ref[:, c0:c0 + GROUP_WIDTH] = _dot(h_scr[...], wmain_ref[:, c0:c0 + GROUP_WIDTH]).astype(BF16)
    kt_ref[0] = _dot_nt(wkt_ref[...], h_scr[...]).astype(BF16)
    gates_ref[...] = _dot(h_scr[...], wg_ref[...])


def _inproj(x2, mod, w_norm, wmain, wkt, wg, bsz, seq, tm=512):
    t, d = x2.shape
    spb = seq // tm
    return pl.pallas_call(
        _inproj_kernel,
        grid=(t // tm,),
        in_specs=[pl.BlockSpec((tm, d), lambda i: (i, 0)),
                  pl.BlockSpec((1, 6, d), lambda i: (i // spb, 0, 0)),
                  pl.BlockSpec((1, d), lambda i: (0, 0)),
                  pl.BlockSpec((d, PROJ_COLS), lambda i: (0, 0)),
                  pl.BlockSpec((GROUP_WIDTH, d), lambda i: (0, 0)),
                  pl.BlockSpec((d, LANES), lambda i: (0, 0))],
        out_specs=[pl.BlockSpec((tm, PROJ_COLS), lambda i: (i, 0)),
                   pl.BlockSpec((1, GROUP_WIDTH, tm), lambda i: (i // spb, 0, i % spb)),
                   pl.BlockSpec((tm, LANES), lambda i: (i, 0))],
        out_shape=[jax.ShapeDtypeStruct((t, PROJ_COLS), BF16),
                   jax.ShapeDtypeStruct((bsz, GROUP_WIDTH, seq), BF16),
                   jax.ShapeDtypeStruct((t, LANES), F32)],
        scratch_shapes=[pltpu.VMEM((tm, d), BF16)],
        compiler_params=_cparams("arbitrary"),
        name="norm_inproj",
    )(x2, mod, w_norm, wmain, wkt, wg)


def _fcum_kernel(g_ref, b_ref, f_ref, carry):
    @pl.when(pl.program_id(1) == 0)
    def _():
        carry[...] = jnp.zeros_like(carry)

    tb = g_ref.shape[0]
    ls = _log_sigmoid(g_ref[...] + b_ref[...])
    cum = _dot_exact(_tril(tb).astype(F32), ls) + carry[...]
    carry[...] = cum[tb - 1:tb, :]
    f_ref[0] = cum.T[0:SUBLANES, :]


def _forget_cumsum(gates, f_bias_row, bsz, seq, tb=512):
    spb = seq // tb
    return pl.pallas_call(
        _fcum_kernel,
        grid=(bsz, spb),
        in_specs=[pl.BlockSpec((tb, LANES), lambda b, j: (b * spb + j, 0)),
                  pl.BlockSpec((1, LANES), lambda b, j: (0, 0))],
        out_specs=pl.BlockSpec((1, SUBLANES, tb), lambda b, j: (b, 0, j)),
        out_shape=jax.ShapeDtypeStruct((bsz, SUBLANES, seq), F32),
        scratch_shapes=[pltpu.VMEM((1, LANES), F32)],
        compiler_params=_cparams("arbitrary", "arbitrary"),
        name="forget_cumsum",
    )(gates, f_bias_row)


def _attn_kernel(qi_ref, kj_ref, q_ref, kt_ref, v_ref, f_ref, gain_ref, o_ref, qm, m_s, l_s, acc):
    p = pl.program_id(1)
    qi, kj = qi_ref[p], kj_ref[p]
    tq = q_ref.shape[0]

    @pl.when(kj == 0)
    def _():
        q = q_ref[...]
        for h in range(N_HEADS):
            qm[h] = _mask_head(q, h)
        m_s[...] = jnp.full_like(m_s, NEG_INF)
        l_s[...] = jnp.zeros_like(l_s)
        acc[...] = jnp.zeros_like(acc)

    def step(diagonal):
        kt = kt_ref[0]
        v = v_ref[...]
        f = f_ref[0]
        for h in range(N_HEADS):
            s = _dot(qm[h], kt) - f[h:h + 1, :]
            if diagonal:
                s = jnp.where(_tril(tq), s, NEG_INF)
            m_prev = m_s[h]
            m_new = jnp.maximum(m_prev, jnp.max(s, axis=1, keepdims=True))
            alpha = jnp.exp(m_prev - m_new)
            pr = jnp.exp(s - m_new)
            l_s[h] = alpha * l_s[h] + jnp.sum(pr, axis=1, keepdims=True)
            acc[h] = alpha * acc[h] + _dot(pr.astype(BF16), v)
            m_s[h] = m_new

    @pl.when(kj < qi)
    def _():
        step(False)

    @pl.when(kj == qi)
    def _():
        step(True)
        y = _mask_head(acc[0] / l_s[0], 0)
        for h in range(1, N_HEADS):
            y = y + _mask_head(acc[h] / l_s[h], h)
        o_ref[...] = _head_rms_norm(y, gain_ref[...]).astype(BF16)


def _attention(proj, kt, frow, gain, bsz, seq, tq=512):
    nq = seq // tq
    pairs = [(i, j) for i in range(nq) for j in range(i + 1)]
    qi = jnp.asarray([a for a, _ in pairs], I32)
    kj = jnp.asarray([b for _, b in pairs], I32)
    t = bsz * seq
    grid_spec = pltpu.PrefetchScalarGridSpec(
        num_scalar_prefetch=2,
        grid=(bsz, len(pairs)),
        in_specs=[pl.BlockSpec((tq, GROUP_WIDTH), lambda b, p, qi, kj: (b * nq + qi[p], PB_Q)),
                  pl.BlockSpec((1, GROUP_WIDTH, tq), lambda b, p, qi, kj: (b, 0, kj[p])),
                  pl.BlockSpec((tq, GROUP_WIDTH), lambda b, p, qi, kj: (b * nq + kj[p], PB_V)),
                  pl.BlockSpec((1, SUBLANES, tq), lambda b, p, qi, kj: (b, 0, kj[p])),
                  pl.BlockSpec((1, GROUP_WIDTH), lambda b, p, qi, kj: (0, 0))],
        out_specs=pl.BlockSpec((tq, GROUP_WIDTH), lambda b, p, qi, kj: (b * nq + qi[p], 0)),
        scratch_shapes=[pltpu.VMEM((N_HEADS, tq, GROUP_WIDTH), BF16),
                        pltpu.VMEM((N_HEADS, tq, 1), F32),
                        pltpu.VMEM((N_HEADS, tq, 1), F32),
                        pltpu.VMEM((N_HEADS, tq, GROUP_WIDTH), F32)])
    return pl.pallas_call(
        _attn_kernel,
        grid_spec=grid_spec,
        out_shape=jax.ShapeDtypeStruct((t, GROUP_WIDTH), BF16),
        compiler_params=_cparams("arbitrary", "arbitrary"),
        name="fox_attention",
    )(qi, kj, proj, kt, proj, frow, gain)


def _sg_kernel(u_ref, v_ref, gain_ref, w_ref, b_ref, o_ref):
    tm = u_ref.shape[0]
    u = _gelu_tanh(u_ref[...].astype(F32))
    v = _head_rms_norm(_gelu_tanh(v_ref[...].astype(F32)), gain_ref[...])
    causal = _tril(CHUNK)
    ws = [jnp.where(causal, w_ref[h], 0.0).astype(BF16) for h in range(N_HEADS)]
    for c0 in range(0, tm, CHUNK):
        vc = v[c0:c0 + CHUNK, :].astype(BF16)
        mixed = b_ref[...]
        for h in range(N_HEADS):
            mixed = mixed + _dot(ws[h], _mask_head(vc, h))
        o_ref[c0:c0 + CHUNK, :] = (u[c0:c0 + CHUNK, :] * mixed).astype(BF16)


def _spatial_gating(proj, gain, sg_w, sg_bias_full, tm=512):
    t = proj.shape[0]
    return pl.pallas_call(
        _sg_kernel,
        grid=(t // tm,),
        in_specs=[pl.BlockSpec((tm, GROUP_WIDTH), lambda i: (i, PB_SU)),
                  pl.BlockSpec((tm, GROUP_WIDTH), lambda i: (i, PB_SV)),
                  pl.BlockSpec((1, GROUP_WIDTH), lambda i: (0, 0)),
                  pl.BlockSpec((N_HEADS, CHUNK, CHUNK), lambda i: (0, 0, 0)),
                  pl.BlockSpec((CHUNK, GROUP_WIDTH), lambda i: (0, 0))],
        out_specs=pl.BlockSpec((tm, GROUP_WIDTH), lambda i: (i, 0)),
        out_shape=jax.ShapeDtypeStruct((t, GROUP_WIDTH), BF16),
        compiler_params=_cparams("arbitrary"),
        name="spatial_gating",
    )(proj, proj, gain, sg_w, sg_bias_full)


def _ssd_kernel(z_ref, x_ref, b_ref, c_ref, g_ref, cw_ref, cb_ref, dtb_ref, alog_ref, dskip_ref, gain_ref,
                o_ref, conv_scr, state):
    L = CHUNK
    W = GROUP_WIDTH

    @pl.when(pl.program_id(1) == 0)
    def _():
        conv_scr[0:SUBLANES, :] = jnp.zeros((SUBLANES, 3 * W), F32)
        state[...] = jnp.zeros_like(state)

    conv_scr[SUBLANES:, 0:W] = x_ref[...].astype(F32)
    conv_scr[SUBLANES:, W:2 * W] = b_ref[...].astype(F32)
    conv_scr[SUBLANES:, 2 * W:] = c_ref[...].astype(F32)
    cw = cw_ref[...]
    conv = cb_ref[...] + cw[CONV_WIDTH - 1:CONV_WIDTH, :] * conv_scr[SUBLANES:, :]
    for s in range(1, CONV_WIDTH):
        conv = conv + cw[CONV_WIDTH - 1 - s:CONV_WIDTH - s, :] * conv_scr[SUBLANES - s:SUBLANES - s + L, :]
    conv_scr[0:SUBLANES, :] = conv_scr[L:L + SUBLANES, :]
    xbc = _silu(conv)
    xs, bm, cm = xbc[:, 0:W], xbc[:, W:2 * W], xbc[:, 2 * W:]

    dt = _softplus(g_ref[...] + dtb_ref[...])
    da = dt * (-jnp.exp(alog_ref[...]))
    a_cum = _dot_exact(_tril(L).astype(F32), da)
    a_row = a_cum.T
    dt_cols = [dt[:, GATE_DT + h:GATE_DT + h + 1] for h in range(N_HEADS)]
    a_cols = [a_cum[:, GATE_DT + h:GATE_DT + h + 1] for h in range(N_HEADS)]
    a_end = [a_cum[L - 1:L, GATE_DT + h:GATE_DT + h + 1] for h in range(N_HEADS)]
    xdt = xs * _expand_heads(dt_cols)
    xdt_b = xdt.astype(BF16)
    xw = (xdt * _expand_heads([jnp.exp(a_end[h] - a_cols[h]) for h in range(N_HEADS)])).astype(BF16)
    exp_a = _expand_heads([jnp.exp(a_cols[h]) for h in range(N_HEADS)])
    causal = _tril(L)
    half = lax.broadcasted_iota(I32, (1, LANES), 1) < HEAD_DIM

    y = dskip_ref[...] * xs
    y_off = []
    for g in range(SSM_GROUPS):
        bg = bm[:, g * LANES:(g + 1) * LANES]
        cg = cm[:, g * LANES:(g + 1) * LANES].astype(BF16)
        scores = _dot_nt(cg, bg.astype(BF16))
        for r in range(N_HEADS // SSM_GROUPS):
            h = g * (N_HEADS // SSM_GROUPS) + r
            seg = a_cols[h] - a_row[GATE_DT + h:GATE_DT + h + 1, :]
            decay = jnp.exp(jnp.where(causal, seg, NEG_INF))
            y = y + _dot((scores * decay).astype(BF16), _mask_head(xdt_b, h))
        st_in = state[g]
        y_off.append(_dot(cg, st_in.astype(BF16)))
        chunk_decay = jnp.where(half, jnp.exp(a_end[2 * g]), jnp.exp(a_end[2 * g + 1]))
        state[g] = chunk_decay * st_in + _dot(bg.T.astype(BF16), xw[:, g * LANES:(g + 1) * LANES])
    y = y + jnp.concatenate(y_off, axis=1) * exp_a
    y = y * _silu(z_ref[...].astype(F32))
    o_ref[...] = _head_rms_norm(y, gain_ref[...]).astype(BF16)


def _ssd_mixer(proj, gates, conv_w, conv_b, dtb_row, alog_row, dskip_row, gain, bsz, seq):
    t = proj.shape[0]
    nc = seq // CHUNK
    row = lambda blk: pl.BlockSpec((CHUNK, GROUP_WIDTH), lambda b, j, blk=blk: (b * nc + j, blk))
    const = lambda shape: pl.BlockSpec(shape, lambda b, j: (0,) * len(shape))
    return pl.pallas_call(
        _ssd_kernel,
        grid=(bsz, nc),
        in_specs=[row(PB_MZ), row(PB_X), row(PB_B), row(PB_C),
                  pl.BlockSpec((CHUNK, LANES), lambda b, j: (b * nc + j, 0)),
                  const((CONV_WIDTH, 3 * GROUP_WIDTH)), const((1, 3 * GROUP_WIDTH)),
                  const((1, LANES)), const((1, LANES)), const((1, GROUP_WIDTH)), const((1, GROUP_WIDTH))],
        out_specs=pl.BlockSpec((CHUNK, GROUP_WIDTH), lambda b, j: (b * nc + j, 0)),
        out_shape=jax.ShapeDtypeStruct((t, GROUP_WIDTH), BF16),
        scratch_shapes=[pltpu.VMEM((CHUNK + SUBLANES, 3 * GROUP_WIDTH), F32),
                        pltpu.VMEM((SSM_GROUPS, LANES, LANES), F32)],
        compiler_params=_cparams("arbitrary", "arbitrary"),
        name="ssd_mixer",
    )(proj, proj, proj, proj, gates, conv_w, conv_b, dtb_row, alog_row, dskip_row, gain)


def _mlstm_kernel(q_ref, k_ref, v_ref, o_gate_ref, g_ref, bias_ref, gain_ref, o_ref, ct, nb, m_row):
    L = CHUNK
    W = GROUP_WIDTH

    @pl.when(pl.program_id(1) == 0)
    def _():
        ct[...] = jnp.zeros_like(ct)
        nb[...] = jnp.zeros_like(nb)
        m_row[...] = jnp.full_like(m_row, MLSTM_M_INIT)

    q = q_ref[...]
    k = k_ref[...]
    v = v_ref[...]
    gate = g_ref[...] + bias_ref[...]
    a_full = _dot_exact(_tril(L).astype(F32), _log_sigmoid(gate))
    a_rows = a_full.T
    g_rows = gate.T
    causal = _tril(L)
    lane = lax.broadcasted_iota(I32, (1, LANES), 1)

    inter_q = _dot(q, ct[...].astype(BF16))
    n_q = _dot(q, nb[...].astype(BF16))
    m_old = m_row[...]
    num = jnp.zeros((L, W), F32)
    inter_cols, den_cols, ws_cols, scale_cols = [], [], [], []
    m_next = m_old
    for h in range(N_HEADS):
        a_col = a_full[:, GATE_LF + h:GATE_LF + h + 1]
        i_col = gate[:, GATE_LI + h:GATE_LI + h + 1]
        a_r = a_rows[GATE_LF + h:GATE_LF + h + 1, :]
        i_r = g_rows[GATE_LI + h:GATE_LI + h + 1, :]
        a_end = a_full[L - 1:L, GATE_LF + h:GATE_LF + h + 1]
        m_in = m_old[:, h:h + 1]
        log_d = jnp.where(causal, a_col - a_r + i_r, NEG_INF)
        log_inter = a_col + m_in
        m_t = jnp.maximum(jnp.max(log_d, axis=1, keepdims=True), log_inter)
        w = _dot_nt(_mask_head(q, h), k) * jnp.exp(log_d - m_t)
        inter = jnp.exp(log_inter - m_t)
        num = num + _dot(w.astype(BF16), _mask_head(v, h))
        den = jnp.sum(w, axis=1, keepdims=True) + inter * n_q[:, h:h + 1]
        inter_cols.append(inter)
        den_cols.append(jnp.maximum(jnp.abs(den), jnp.exp(-m_t)))
        g_col = a_end - a_col + i_col
        m_new = jnp.maximum(a_end + m_in, jnp.max(g_col, axis=0, keepdims=True))
        ws_cols.append(jnp.exp(g_col - m_new))
        scale_cols.append(jnp.exp(a_end + m_in - m_new))
        m_next = jnp.where(lane == h, m_new, m_next)
    hout = (num + _expand_heads(inter_cols) * inter_q) / _expand_heads(den_cols)
    y = _sigmoid(o_gate_ref[...].astype(F32)) * _head_rms_norm(hout, gain_ref[...])
    o_ref[...] = y.astype(BF16)

    kw_t = (k.astype(F32) * _expand_heads(ws_cols)).T.astype(BF16)
    scale_row = _expand_heads(scale_cols)
    same_head = _head_of_lane((W, W), 0) == _head_of_lane((W, W), 1)
    ct[...] = scale_row * ct[...] + jnp.where(same_head, _dot(kw_t, v), 0.0)
    col_is_head = _head_of_lane((W, LANES), 0) == lax.broadcasted_iota(I32, (W, LANES), 1)
    scale_n = scale_cols[N_HEADS - 1]
    for h in range(N_HEADS - 2, -1, -1):
        scale_n = jnp.where(lane == h, scale_cols[h], scale_n)
    nb[...] = scale_n * nb[...] + jnp.where(col_is_head, _dot(kw_t, jnp.ones((L, LANES), BF16)), 0.0)
    m_row[...] = m_next


def _mlstm_mixer(proj, gates, bias_row, gain, bsz, seq):
    t = proj.shape[0]
    nc = seq // CHUNK
    row = lambda blk: pl.BlockSpec((CHUNK, GROUP_WIDTH), lambda b, j, blk=blk: (b * nc + j, blk))
    const = lambda shape: pl.BlockSpec(shape, lambda b, j: (0,) * len(shape))
    return pl.pallas_call(
        _mlstm_kernel,
        grid=(bsz, nc),
        in_specs=[row(PB_LQ), row(PB_LK), row(PB_LV), row(PB_LO),
                  pl.BlockSpec((CHUNK, LANES), lambda b, j: (b * nc + j, 0)),
                  const((1, LANES)), const((1, GROUP_WIDTH))],
        out_specs=pl.BlockSpec((CHUNK, GROUP_WIDTH), lambda b, j: (b * nc + j, 0)),
        out_shape=jax.ShapeDtypeStruct((t, GROUP_WIDTH), BF16),
        scratch_shapes=[pltpu.VMEM((GROUP_WIDTH, GROUP_WIDTH), F32),
                        pltpu.VMEM((GROUP_WIDTH, LANES), F32),
                        pltpu.VMEM((1, LANES), F32)],
        compiler_params=_cparams("arbitrary", "arbitrary"),
        name="mlstm_mixer",
    )(proj, proj, proj, proj, gates, bias_row, gain)


def _outproj_router_kernel(x_ref, ya_ref, ys_ref, ym_ref, yl_ref, wo_ref, mod_ref, wn_ref, wr_ref, br_ref,
                           xo_ref, h2_ref, eid_ref, gcol_ref):
    W = GROUP_WIDTH
    tm = x_ref.shape[0]
    mod = mod_ref[0]
    out = _dot(ya_ref[...], wo_ref[0:W, :])
    out = out + _dot(ys_ref[...], wo_ref[W:2 * W, :])
    out = out + _dot(ym_ref[...], wo_ref[2 * W:3 * W, :])
    out = out + _dot(yl_ref[...], wo_ref[3 * W:4 * W, :])
    x1 = x_ref[...] + mod[2:3, :] * out
    xo_ref[...] = x1
    h2 = _rms_norm(x1, wn_ref[...]) * (1.0 + mod[4:5, :]) + mod[3:4, :]
    h2_ref[...] = h2

    logits_t = (_dot(h2.astype(BF16), wr_ref[...]) + br_ref[...]).T
    row8 = lax.broadcasted_iota(I32, (SUBLANES, tm), 0).astype(F32)
    gl = jnp.where(row8 < N_EXPERT_GROUPS, logits_t[ROUTE_G:ROUTE_G + SUBLANES, :], NEG_INF)
    g_max = jnp.max(gl, axis=0, keepdims=True)
    g_sel = jnp.min(jnp.where(gl == g_max, row8, SUBLANES), axis=0, keepdims=True)
    g_prob = 1.0 / jnp.sum(jnp.exp(gl - g_max), axis=0, keepdims=True)
    el = logits_t[ROUTE_E:ROUTE_E + EXPERTS_PER_GROUP, :]
    for g in range(1, N_EXPERT_GROUPS):
        lo = ROUTE_E + g * EXPERTS_PER_GROUP
        el = jnp.where(g_sel == g, logits_t[lo:lo + EXPERTS_PER_GROUP, :], el)
    m1 = jnp.max(el, axis=0, keepdims=True)
    i1 = jnp.min(jnp.where(el == m1, row8, SUBLANES), axis=0, keepdims=True)
    el2 = jnp.where(row8 == i1, NEG_INF, el)
    m2 = jnp.max(el2, axis=0, keepdims=True)
    i2 = jnp.min(jnp.where(el2 == m2, row8, SUBLANES), axis=0, keepdims=True)
    ratio = jnp.exp(m2 - m1)
    p1 = 1.0 / (1.0 + ratio)
    eid_ref[0:1, :] = (g_sel * EXPERTS_PER_GROUP + i1).astype(I32)
    eid_ref[1:2, :] = (g_sel * EXPERTS_PER_GROUP + i2).astype(I32)
    rows = lax.broadcasted_iota(I32, (LANES, tm), 0)
    gate_rows = jnp.where(rows == 0, g_prob * p1, jnp.where(rows == 1, g_prob * p1 * ratio, 0.0))
    gcol_ref[...] = gate_rows.T


def _outproj_router(x2, ys, w_out, mod, w_norm2, w_route, b_route, seq, tm=256):
    t, d = x2.shape
    spb = seq // tm
    ytile = pl.BlockSpec((tm, GROUP_WIDTH), lambda i: (i, 0))
    return pl.pallas_call(
        _outproj_router_kernel,
        grid=(t // tm,),
        in_specs=[pl.BlockSpec((tm, d), lambda i: (i, 0)), ytile, ytile, ytile, ytile,
                  pl.BlockSpec((d, d), lambda i: (0, 0)),
                  pl.BlockSpec((1, 6, d), lambda i: (i // spb, 0, 0)),
                  pl.BlockSpec((1, d), lambda i: (0, 0)),
                  pl.BlockSpec((d, LANES), lambda i: (0, 0)),
                  pl.BlockSpec((1, LANES), lambda i: (0, 0))],
        out_specs=[pl.BlockSpec((tm, d), lambda i: (i, 0)),
                   pl.BlockSpec((tm, d), lambda i: (i, 0)),
                   pl.BlockSpec((TOP_K, tm), lambda i: (0, i)),
                   pl.BlockSpec((tm, LANES), lambda i: (i, 0))],
        out_shape=[jax.ShapeDtypeStruct((t, d), F32),
                   jax.ShapeDtypeStruct((t, d), F32),
                   jax.ShapeDtypeStruct((TOP_K, t), I32),
                   jax.ShapeDtypeStruct((t, LANES), F32)],
        compiler_params=_cparams("arbitrary"),
        name="outproj_router",
    )(x2, *ys, w_out, mod, w_norm2, w_route, b_route)


def _rank_kernel(eid_ref, rank_ref, count_ref, carry):
    @pl.when(pl.program_id(0) == 0)
    def _():
        carry[...] = jnp.zeros_like(carry)

    tr = eid_ref.shape[1]
    expert = lax.broadcasted_iota(I32, (N_EXPERTS, tr), 0)
    before = (lax.broadcasted_iota(I32, (tr, tr), 0) < lax.broadcasted_iota(I32, (tr, tr), 1)).astype(BF16)
    base = carry[...]
    for k in range(TOP_K):
        onehot = (expert == eid_ref[k:k + 1, :]).astype(F32)
        prefix = _dot(onehot.astype(BF16), before)
        rank_ref[k:k + 1, :] = jnp.sum(onehot * (base + prefix), axis=0, keepdims=True).astype(I32)
        base = base + jnp.sum(onehot, axis=1, keepdims=True)
    carry[...] = base
    count_ref[...] = jnp.broadcast_to(base, count_ref.shape)


def _expert_ranks(eids, tr=512):
    t = eids.shape[1]
    return pl.pallas_call(
        _rank_kernel,
        grid=(t // tr,),
        in_specs=[pl.BlockSpec((TOP_K, tr), lambda i: (0, i))],
        out_specs=[pl.BlockSpec((TOP_K, tr), lambda i: (0, i)),
                   pl.BlockSpec((N_EXPERTS, LANES), lambda i: (0, 0))],
        out_shape=[jax.ShapeDtypeStruct((TOP_K, t), I32),
                   jax.ShapeDtypeStruct((N_EXPERTS, LANES), F32)],
        scratch_shapes=[pltpu.VMEM((N_EXPERTS, 1), F32)],
        compiler_params=_cparams("arbitrary"),
        name="expert_ranks",
    )(eids)


def _dest_kernel(pstart_ref, eid_ref, rank_ref, dest_ref):
    e = eid_ref[...]
    dest = rank_ref[...]
    for j in range(N_EXPERTS):
        dest = dest + jnp.where(e == j, pstart_ref[j], 0)
    dest_ref[...] = dest


def _dest_rows(p_starts, eids, ranks, tm=2048):
    t = eids.shape[1]
    grid_spec = pltpu.PrefetchScalarGridSpec(
        num_scalar_prefetch=1,
        grid=(t // tm,),
        in_specs=[pl.BlockSpec((TOP_K, tm), lambda i, ps: (0, i)),
                  pl.BlockSpec((TOP_K, tm), lambda i, ps: (0, i))],
        out_specs=pl.BlockSpec((TOP_K, tm), lambda i, ps: (0, i)))
    return pl.pallas_call(
        _dest_kernel,
        grid_spec=grid_spec,
        out_shape=jax.ShapeDtypeStruct((TOP_K, t), I32),
        compiler_params=_cparams("arbitrary"),
        name="dest_rows",
    )(p_starts, eids, ranks)


def _row_copy_kernel(*refs, scatter, n_src):
    if scatter:
        idx_hbm, src_hbm, _, dst_hbm, idx_smem, idx_sem, row_sem = refs
    else:
        idx_hbm, src_hbm, dst_hbm, idx_smem, idx_sem, row_sem = refs
    c = pl.program_id(0)
    base = c * IDX_CHUNK
    idx_copy = pltpu.make_async_copy(idx_hbm.at[pl.ds(base, IDX_CHUNK)], idx_smem, idx_sem)
    idx_copy.start()
    idx_copy.wait()
    base_src = base - (base // n_src) * n_src

    def row_copy(src_row, dst_row):
        return pltpu.make_async_copy(src_hbm.at[pl.ds(src_row, 1)], dst_hbm.at[pl.ds(dst_row, 1)], row_sem)

    def issue(r, carry):
        j = idx_smem[r]
        if scatter:
            row_copy(base_src + r, j).start()
        else:
            row_copy(j, base + r).start()
        return carry

    lax.fori_loop(0, IDX_CHUNK, issue, 0)

    def drain(r, carry):
        row_copy(0, 0).wait()
        return carry

    @pl.when(c > 0)
    def _():
        lax.fori_loop(0, IDX_CHUNK, drain, 0)

    @pl.when(c == pl.num_programs(0) - 1)
    def _():
        lax.fori_loop(0, IDX_CHUNK, drain, 0)


def _row_copy(idx_flat, src, dst_rows, scatter):
    n = idx_flat.shape[0]
    d = src.shape[1]
    any_spec = pl.BlockSpec(memory_space=pl.ANY)
    kernel = functools.partial(_row_copy_kernel, scatter=scatter, n_src=src.shape[0])
    operands = [idx_flat, src]
    kwargs = {}
    if scatter:
        operands.append(jnp.zeros((dst_rows, d), src.dtype))
        kwargs["input_output_aliases"] = {2: 0}
    return pl.pallas_call(
        kernel,
        grid=(n // IDX_CHUNK,),
        in_specs=[any_spec] * len(operands),
        out_specs=any_spec,
        out_shape=jax.ShapeDtypeStruct((dst_rows, d), src.dtype),
        scratch_shapes=[pltpu.SMEM((IDX_CHUNK,), I32), pltpu.SemaphoreType.DMA, pltpu.SemaphoreType.DMA],
        compiler_params=_cparams("arbitrary"),
        name="row_scatter" if scatter else "row_gather",
        **kwargs,
    )(*operands)


def _expert_kernel(be_ref, nu_ref, x_ref, wg_ref, wu_ref, wd_ref, y_ref):
    @pl.when(pl.program_id(0) < nu_ref[0])
    def _():
        x = x_ref[...].astype(BF16)
        a = _silu(_dot(x, wg_ref[0])) * _dot(x, wu_ref[0])
        y_ref[...] = _dot(a.astype(BF16), wd_ref[0])


def _expert_mlp(block_e, n_used, xb, w_gate, w_up, w_down):
    p, d = xb.shape
    de = w_gate.shape[2]
    blk = lambda b, be, nu: (jnp.minimum(b, nu[0] - 1), 0)
    grid_spec = pltpu.PrefetchScalarGridSpec(
        num_scalar_prefetch=2,
        grid=(p // MOE_BLOCK,),
        in_specs=[pl.BlockSpec((MOE_BLOCK, d), blk),
                  pl.BlockSpec((1, d, de), lambda b, be, nu: (be[b], 0, 0)),
                  pl.BlockSpec((1, d, de), lambda b, be, nu: (be[b], 0, 0)),
                  pl.BlockSpec((1, de, d), lambda b, be, nu: (be[b], 0, 0))],
        out_specs=pl.BlockSpec((MOE_BLOCK, d), blk))
    return pl.pallas_call(
        _expert_kernel,
        grid_spec=grid_spec,
        out_shape=jax.ShapeDtypeStruct((p, d), F32),
        compiler_params=_cparams("arbitrary"),
        name="expert_mlp",
    )(block_e, n_used, xb, w_gate, w_up, w_down)


def _combine_kernel(x_ref, y0_ref, y1_ref, gcol_ref, mod_ref, wnf_ref, o_ref, *, final):
    gc = gcol_ref[...]
    moe = gc[:, 0:1] * y0_ref[...] + gc[:, 1:2] * y1_ref[...]
    x2 = x_ref[...] + mod_ref[0][5:6, :] * moe
    o_ref[...] = _rms_norm(x2, wnf_ref[...]) if final else x2


def _combine(x2, y2, gcol, mod, w_norm_final, seq, final, tm=512):
    t, d = x2.shape
    spb = seq // tm
    nt = t // tm
    return pl.pallas_call(
        functools.partial(_combine_kernel, final=final),
        grid=(nt,),
        in_specs=[pl.BlockSpec((tm, d), lambda i: (i, 0)),
                  pl.BlockSpec((tm, d), lambda i: (i, 0)),
                  pl.BlockSpec((tm, d), lambda i: (i + nt, 0)),
                  pl.BlockSpec((tm, LANES), lambda i: (i, 0)),
                  pl.BlockSpec((1, 6, d), lambda i: (i // spb, 0, 0)),
                  pl.BlockSpec((1, d), lambda i: (0, 0))],
        out_specs=pl.BlockSpec((tm, d), lambda i: (i, 0)),
        out_shape=jax.ShapeDtypeStruct((t, d), F32),
        compiler_params=_cparams("arbitrary"),
        name="moe_combine",
    )(x2, y2, y2, gcol, mod, w_norm_final)


def _lane_row(pieces, width=LANES):
    row = jnp.zeros((width,), F32)
    for off, vec in pieces.items():
        row = row.at[off:off + vec.shape[0]].set(vec.astype(F32))
    return row.reshape(1, width)


def _split_w_in(w_in):
    gw, nh = GROUP_WIDTH, N_HEADS
    widths = [gw, gw, gw, nh, gw, gw, gw, 3 * gw, nh, gw, gw, gw, gw, nh, nh]
    cuts, acc = [], 0
    for w in widths[:-1]:
        acc += w
        cuts.append(acc)
    (aq, ak, av, af, su, sv, mz, mxbc, mdt, lq, lk, lv, lo, li, lf) = jnp.split(w_in, cuts, axis=1)
    scale = HEAD_DIM ** -0.5
    wmain = jnp.concatenate([aq * scale, av, su, sv, mz, mxbc, lq, lk * scale, lv, lo], axis=1).astype(BF16)
    wkt = ak.T.astype(BF16)
    wg = jnp.zeros((w_in.shape[0], LANES), F32)
    for off, w in ((GATE_AF, af), (GATE_DT, mdt), (GATE_LI, li), (GATE_LF, lf)):
        wg = wg.at[:, off:off + nh].set(w)
    return wmain, wkt, wg.astype(BF16)


def _moe_layer(x1, h2, eids, gcol, mod_l, w_gate, w_up, w_down, w_norm_final, seq, final):
    t, d = x1.shape
    ranks, counts = _expert_ranks(eids)
    counts = counts[:, 0].astype(I32)
    padded = ((counts + MOE_BLOCK - 1) // MOE_BLOCK) * MOE_BLOCK
    p_ends = jnp.cumsum(padded)
    p_starts = (p_ends - padded).astype(I32)
    n_blocks = (t * TOP_K) // MOE_BLOCK + N_EXPERTS
    blocks = jnp.arange(n_blocks, dtype=I32)
    block_e = jnp.minimum(jnp.searchsorted(p_ends, blocks * MOE_BLOCK, side="right"), N_EXPERTS - 1).astype(I32)
    n_used = (p_ends[-1] // MOE_BLOCK).astype(I32)
    block_e = jnp.where(blocks < n_used, block_e, block_e[n_used - 1])
    dest = _dest_rows(p_starts, eids, ranks).reshape(-1)
    xb = _row_copy(dest, h2, n_blocks * MOE_BLOCK, scatter=True)
    yb = _expert_mlp(block_e, n_used.reshape(1), xb, w_gate, w_up, w_down)
    y2 = _row_copy(dest, yb, t * TOP_K, scatter=False)
    return _combine(x1, y2, gcol, mod_l, w_norm_final, seq, final)


def kernel(x, c, w_in, w_out, w_mix_norm, attn_f_bias, sg_w, sg_b, ssm_conv_w, ssm_conv_b, ssm_dt_bias,
           ssm_a_log, ssm_d, mlstm_i_bias, mlstm_f_bias, w_ada, b_ada, w_norm1, w_norm2, w_router_group,
           b_router_group, w_router_expert, b_router_expert, w_expert_gate, w_expert_up, w_expert_down,
           w_norm_final):
    bsz, seq, d = x.shape
    depth = w_in.shape[0]
    gw = GROUP_WIDTH
    mod = _ada_modulation(c, w_ada, b_ada).reshape(depth, bsz, 6, d)
    x2 = x.reshape(bsz * seq, d)
    wnf = w_norm_final.reshape(1, d)
    for l in range(depth):
        wmain, wkt, wg = _split_w_in(w_in[l])
        gains = w_mix_norm[l].reshape(N_HEADS, 1, gw)
        proj, kt, gates = _inproj(x2, mod[l], w_norm1[l].reshape(1, d), wmain, wkt, wg, bsz, seq)
        frow = _forget_cumsum(gates, _lane_row({GATE_AF: attn_f_bias[l]}), bsz, seq)
        y_attn = _attention(proj, kt, frow, gains[0], bsz, seq)
        sg_bias_full = jnp.repeat(sg_b[l].T, HEAD_DIM, axis=1)
        y_sg = _spatial_gating(proj, gains[1], sg_w[l], sg_bias_full)
        y_ssm = _ssd_mixer(proj, gates, ssm_conv_w[l], ssm_conv_b[l].reshape(1, -1),
                           _lane_row({GATE_DT: ssm_dt_bias[l]}), _lane_row({GATE_DT: ssm_a_log[l]}),
                           jnp.repeat(ssm_d[l], HEAD_DIM).reshape(1, gw), gains[2], bsz, seq)
        y_ml = _mlstm_mixer(proj, gates, _lane_row({GATE_LI: mlstm_i_bias[l], GATE_LF: mlstm_f_bias[l]}),
                            gains[3], bsz, seq)
        w_route = jnp.zeros((d, LANES), F32)
        w_route = w_route.at[:, ROUTE_G:ROUTE_G + N_EXPERT_GROUPS].set(w_router_group[l])
        w_route = w_route.at[:, ROUTE_E:ROUTE_E + N_EXPERTS].set(w_router_expert[l]).astype(BF16)
        b_route = _lane_row({ROUTE_G: b_router_group[l], ROUTE_E: b_router_expert[l]})
        x1, h2, eids, gcol = _outproj_router(x2, (y_attn, y_sg, y_ssm, y_ml), w_out[l].astype(BF16), mod[l],
                                             w_norm2[l].reshape(1, d), w_route, b_route, seq)
        x2 = _moe_layer(x1, h2, eids, gcol, mod[l], w_expert_gate[l].astype(BF16), w_expert_up[l].astype(BF16),
                        w_expert_down[l].astype(BF16), wnf, seq, final=(l == depth - 1))
    return x2.reshape(bsz, seq, d)
```

```python
import functools

import jax
import jax.numpy as jnp
from jax import lax
from jax.experimental import pallas as pl
from jax.experimental.pallas import tpu as pltpu

F32 = jnp.float32
BF16 = jnp.bfloat16
I32 = jnp.int32

LANES = 128
SUBLANES = 8
HEAD_DIM = 64
N_HEADS = 4
GROUP_WIDTH = 256
CHUNK = 128
SSM_GROUPS = 2
CONV_WIDTH = 4
N_EXPERT_GROUPS = 4
EXPERTS_PER_GROUP = 8
N_EXPERTS = N_EXPERT_GROUPS * EXPERTS_PER_GROUP
TOP_K = 2
MOE_BLOCK = 256
NORM_EPS = 1e-6
MLSTM_M_INIT = -1e30
NEG_INF = float("-inf")
VMEM_LIMIT_BYTES = 48 * 1024 * 1024
IDX_CHUNK = 1024

(PB_K, PB_SU, PB_SV, PB_MZ, PB_X, PB_B, PB_C, PB_LQ, PB_LK, PB_LV, PB_LO) = range(11)
PROJ_COLS = 11 * GROUP_WIDTH
LOG2_E = 1.4426950408889634
GATE_AF, GATE_DT, GATE_LI, GATE_LF = 0, 4, 8, 12
ROUTE_G, ROUTE_E = 0, 8

NT_DIMS = (((1,), (1,)), ((), ()))


def _cparams(*sem):
    return pltpu.CompilerParams(dimension_semantics=sem, vmem_limit_bytes=VMEM_LIMIT_BYTES)


def _dot(a, b):
    return jnp.dot(a, b, preferred_element_type=F32)


def _dot_nt(a, b):
    return lax.dot_general(a, b, NT_DIMS, preferred_element_type=F32)


def _dot_exact(a, b):
    return jnp.dot(a, b, preferred_element_type=F32, precision=lax.Precision.HIGHEST)


def _head_of_lane(shape, axis=1):
    return lax.broadcasted_iota(I32, shape, axis) // HEAD_DIM


def _sigmoid(x):
    return 1.0 / (1.0 + jnp.exp(-x))


def _silu(x):
    return x * _sigmoid(x)


def _log_sigmoid(x):
    return jnp.minimum(x, 0.0) - jnp.log1p(jnp.exp(-jnp.abs(x)))


def _softplus(x):
    return jnp.maximum(x, 0.0) + jnp.log1p(jnp.exp(-jnp.abs(x)))


def _gelu_tanh(x):
    return 0.5 * x * (1.0 + jnp.tanh(0.7978845608028654 * (x + 0.044715 * (x * x * x))))


def _expand_heads(cols, width=GROUP_WIDTH):
    rows = cols[0].shape[0]
    head = _head_of_lane((rows, width))
    out = jnp.broadcast_to(cols[N_HEADS - 1], (rows, width))
    for h in range(N_HEADS - 2, -1, -1):
        out = jnp.where(head == h, jnp.broadcast_to(cols[h], (rows, width)), out)
    return out


def _mask_head(x, h):
    return jnp.where(_head_of_lane(x.shape) == h, x, jnp.zeros_like(x))


def _head_rms_norm(y, gain):
    head = _head_of_lane(y.shape)
    sq = y * y
    cols = [jnp.sum(jnp.where(head == h, sq, 0.0), axis=1, keepdims=True) * (1.0 / HEAD_DIM)
            for h in range(N_HEADS)]
    return y * lax.rsqrt(_expand_heads(cols) + NORM_EPS) * gain


def _rms_norm(x, gain):
    ms = jnp.mean(x * x, axis=1, keepdims=True)
    return x * lax.rsqrt(ms + NORM_EPS) * gain


def _tril(n, strict=False):
    r = lax.broadcasted_iota(I32, (n, n), 0)
    c = lax.broadcasted_iota(I32, (n, n), 1)
    return (r > c) if strict else (r >= c)


def _ada_kernel(c_ref, w_ref, b_ref, o_ref):
    o_ref[0] = _dot_exact(_silu(c_ref[...]), w_ref[0]) + b_ref[0]


def _ada_modulation(c, w_ada, b_ada):
    depth, d, d6 = w_ada.shape
    bsz = c.shape[0]
    return pl.pallas_call(
        _ada_kernel,
        grid=(depth, d6 // d),
        in_specs=[pl.BlockSpec((bsz, d), lambda l, j: (0, 0)),
                  pl.BlockSpec((1, d, d), lambda l, j: (l, 0, j)),
                  pl.BlockSpec((1, 1, d), lambda l, j: (l, 0, j))],
        out_specs=pl.BlockSpec((1, bsz, d), lambda l, j: (l, 0, j)),
        out_shape=jax.ShapeDtypeStruct((depth, bsz, d6), F32),
        compiler_params=_cparams("arbitrary", "arbitrary"),
        name="ada_modulation",
    )(c, w_ada, b_ada.reshape(depth, 1, d6))


def _inproj_kernel(x_ref, mod_ref, wn_ref, wmain_ref, wqt_ref, wvt_ref, wg_ref,
                   proj_ref, qt_ref, vt_ref, gates_ref, h_scr):
    mod = mod_ref[0]
    h = _rms_norm(x_ref[...], wn_ref[...]) * (1.0 + mod[1:2, :]) + mod[0:1, :]
    h_scr[...] = h.astype(BF16)
    for c0 in range(0, PROJ_COLS, GROUP_WIDTH):
        proj_ref[:, c0:c0 + GROUP_WIDTH] = _dot(h_scr[...], wmain_ref[:, c0:c0 + GROUP_WIDTH]).astype(BF16)
    qt_ref[0] = (_dot_nt(wqt_ref[...], h_scr[...]) * LOG2_E).astype(BF16)
    vt_ref[0] = _dot_nt(wvt_ref[...], h_scr[...]).astype(BF16)
    gates_ref[...] = _dot(h_scr[...], wg_ref[...])


def _inproj(x2, mod, w_norm, wmain, wqt, wvt, wg, bsz, seq, tm=512):
    t, d = x2.shape
    spb = seq // tm
    tspec = pl.BlockSpec((1, GROUP_WIDTH, tm), lambda i: (i // spb, 0, i % spb))
    return pl.pallas_call(
        _inproj_kernel,
        grid=(t // tm,),
        in_specs=[pl.BlockSpec((tm, d), lambda i: (i, 0)),
                  pl.BlockSpec((1, 6, d), lambda i: (i // spb, 0, 0)),
                  pl.BlockSpec((1, d), lambda i: (0, 0)),
                  pl.BlockSpec((d, PROJ_COLS), lambda i: (0, 0)),
                  pl.BlockSpec((GROUP_WIDTH, d), lambda i: (0, 0)),
                  pl.BlockSpec((GROUP_WIDTH, d), lambda i: (0, 0)),
                  pl.BlockSpec((d, LANES), lambda i: (0, 0))],
        out_specs=[pl.BlockSpec((tm, PROJ_COLS), lambda i: (i, 0)), tspec, tspec,
                   pl.BlockSpec((tm, LANES), lambda i: (i, 0))],
        out_shape=[jax.ShapeDtypeStruct((t, PROJ_COLS), BF16),
                   jax.ShapeDtypeStruct((bsz, GROUP_WIDTH, seq), BF16),
                   jax.ShapeDtypeStruct((bsz, GROUP_WIDTH, seq), BF16),
                   jax.ShapeDtypeStruct((t, LANES), F32)],
        scratch_shapes=[pltpu.VMEM((tm, d), BF16)],
        compiler_params=_cparams("arbitrary"),
        name="norm_inproj",
    )(x2, mod, w_norm, wmain, wqt, wvt, wg)


def _fcum_kernel(g_ref, b_ref, f_ref, carry):
    @pl.when(pl.program_id(1) == 0)
    def _():
        carry[...] = jnp.zeros_like(carry)

    tb = g_ref.shape[0]
    ls = _log_sigmoid(g_ref[...] + b_ref[...])
    cum = _dot_exact(_tril(tb).astype(F32), ls) + carry[...]
    carry[...] = cum[tb - 1:tb, :]
    f_ref[...] = cum * LOG2_E


def _forget_cumsum(gates, f_bias_row, bsz, seq, tb=512):
    spb = seq // tb
    return pl.pallas_call(
        _fcum_kernel,
        grid=(bsz, spb),
        in_specs=[pl.BlockSpec((tb, LANES), lambda b, j: (b * spb + j, 0)),
                  pl.BlockSpec((1, LANES), lambda b, j: (0, 0))],
        out_specs=pl.BlockSpec((tb, LANES), lambda b, j: (b * spb + j, 0)),
        out_shape=jax.ShapeDtypeStruct((bsz * seq, LANES), F32),
        scratch_shapes=[pltpu.VMEM((1, LANES), F32)],
        compiler_params=_cparams("arbitrary", "arbitrary"),
        name="forget_cumsum",
    )(gates, f_bias_row)


def _attn_kernel(qi_ref, kj_ref, k_ref, qt_ref, vt_ref, f_ref, gain_ref, o_ref, qtm, m_s, l_s, acc):
    p = pl.program_id(1)
    qi, kj = qi_ref[p], kj_ref[p]
    tk = k_ref.shape[0]
    tq = qt_ref.shape[2]

    @pl.when(kj == 0)
    def _():
        qt = qt_ref[0]
        head = _head_of_lane(qt.shape, 0)
        for h in range(N_HEADS):
            qtm[h] = jnp.where(head == h, qt, jnp.zeros_like(qt))
        m_s[...] = jnp.full_like(m_s, NEG_INF)
        l_s[...] = jnp.zeros_like(l_s)
        acc[...] = jnp.zeros_like(acc)

    def step(diagonal):
        k = k_ref[...]
        f = f_ref[...]
        if diagonal:
            visible = lax.broadcasted_iota(I32, (tk, tq), 0) <= lax.broadcasted_iota(I32, (tk, tq), 1)
        for h in range(N_HEADS):
            s = _dot(k, qtm[h]) - f[:, GATE_AF + h:GATE_AF + h + 1]
            if diagonal:
                s = jnp.where(visible, s, NEG_INF)
            m_prev = m_s[h]
            m_new = jnp.maximum(m_prev, jnp.max(s, axis=0, keepdims=True))
            alpha = jnp.exp2(m_prev - m_new)
            pr = jnp.exp2(s - m_new)
            l_s[h] = alpha * l_s[h] + jnp.sum(pr, axis=0, keepdims=True)
            pv = _dot(vt_ref[0, h * HEAD_DIM:(h + 1) * HEAD_DIM, :], pr.astype(BF16))
            acc[h] = alpha * acc[h] + pv
            m_s[h] = m_new

    @pl.when(kj < qi)
    def _():
        step(False)

    @pl.when(kj == qi)
    def _():
        step(True)
        yt = jnp.concatenate([acc[h] / l_s[h] for h in range(N_HEADS)], axis=0)
        o_ref[...] = _head_rms_norm(yt.T, gain_ref[...]).astype(BF16)


def _attention(proj, qt, vt, f2, gain, bsz, seq, tq=512):
    nq = seq // tq
    pairs = [(i, j) for i in range(nq) for j in range(i + 1)]
    qi = jnp.asarray([a for a, _ in pairs], I32)
    kj = jnp.asarray([b for _, b in pairs], I32)
    t = bsz * seq
    grid_spec = pltpu.PrefetchScalarGridSpec(
        num_scalar_prefetch=2,
        grid=(bsz, len(pairs)),
        in_specs=[pl.BlockSpec((tq, GROUP_WIDTH), lambda b, p, qi, kj: (b * nq + kj[p], PB_K)),
                  pl.BlockSpec((1, GROUP_WIDTH, tq), lambda b, p, qi, kj: (b, 0, qi[p])),
                  pl.BlockSpec((1, GROUP_WIDTH, tq), lambda b, p, qi, kj: (b, 0, kj[p])),
                  pl.BlockSpec((tq, LANES), lambda b, p, qi, kj: (b * nq + kj[p], 0)),
                  pl.BlockSpec((1, GROUP_WIDTH), lambda b, p, qi, kj: (0, 0))],
        out_specs=pl.BlockSpec((tq, GROUP_WIDTH), lambda b, p, qi, kj: (b * nq + qi[p], 0)),
        scratch_shapes=[pltpu.VMEM((N_HEADS, GROUP_WIDTH, tq), BF16),
                        pltpu.VMEM((N_HEADS, 1, tq), F32),
                        pltpu.VMEM((N_HEADS, 1, tq), F32),
                        pltpu.VMEM((N_HEADS, HEAD_DIM, tq), F32)])
    return pl.pallas_call(
        _attn_kernel,
        grid_spec=grid_spec,
        out_shape=jax.ShapeDtypeStruct((t, GROUP_WIDTH), BF16),
        compiler_params=_cparams("arbitrary", "arbitrary"),
        name="fox_attention",
    )(qi, kj, proj, qt, vt, f2, gain)


def _sg_kernel(u_ref, v_ref, gain_ref, w_ref, b_ref, o_ref):
    tm = u_ref.shape[0]
    u = _gelu_tanh(u_ref[...].astype(F32))
    v = _head_rms_norm(_gelu_tanh(v_ref[...].astype(F32)), gain_ref[...])
    causal = _tril(CHUNK)
    ws = [jnp.where(causal, w_ref[h], 0.0).astype(BF16) for h in range(N_HEADS)]
    for c0 in range(0, tm, CHUNK):
        vc = v[c0:c0 + CHUNK, :].astype(BF16)
        mixed = b_ref[...]
        for h in range(N_HEADS):
            mixed = mixed + _dot(ws[h], _mask_head(vc, h))
        o_ref[c0:c0 + CHUNK, :] = (u[c0:c0 + CHUNK, :] * mixed).astype(BF16)


def _spatial_gating(proj, gain, sg_w, sg_bias_full, tm=512):
    t = proj.shape[0]
    return pl.pallas_call(
        _sg_kernel,
        grid=(t // tm,),
        in_specs=[pl.BlockSpec((tm, GROUP_WIDTH), lambda i: (i, PB_SU)),
                  pl.BlockSpec((tm, GROUP_WIDTH), lambda i: (i, PB_SV)),
                  pl.BlockSpec((1, GROUP_WIDTH), lambda i: (0, 0)),
                  pl.BlockSpec((N_HEADS, CHUNK, CHUNK), lambda i: (0, 0, 0)),
                  pl.BlockSpec((CHUNK, GROUP_WIDTH), lambda i: (0, 0))],
        out_specs=pl.BlockSpec((tm, GROUP_WIDTH), lambda i: (i, 0)),
        out_shape=jax.ShapeDtypeStruct((t, GROUP_WIDTH), BF16),
        compiler_params=_cparams("arbitrary"),
        name="spatial_gating",
    )(proj, proj, gain, sg_w, sg_bias_full)


def _ssd_kernel(z_ref, x_ref, b_ref, c_ref, g_ref, cw_ref, cb_ref, dtb_ref, alog_ref, dskip_ref, gain_ref,
                o_ref, conv_scr, state):
    L = CHUNK
    W = GROUP_WIDTH

    @pl.when(pl.program_id(1) == 0)
    def _():
        conv_scr[0:SUBLANES, :] = jnp.zeros((SUBLANES, 3 * W), F32)
        state[...] = jnp.zeros_like(state)

    conv_scr[SUBLANES:, 0:W] = x_ref[...].astype(F32)
    conv_scr[SUBLANES:, W:2 * W] = b_ref[...].astype(F32)
    conv_scr[SUBLANES:, 2 * W:] = c_ref[...].astype(F32)
    cw = cw_ref[...]
    conv = cb_ref[...] + cw[CONV_WIDTH - 1:CONV_WIDTH, :] * conv_scr[SUBLANES:, :]
    for s in range(1, CONV_WIDTH):
        conv = conv + cw[CONV_WIDTH - 1 - s:CONV_WIDTH - s, :] * conv_scr[SUBLANES - s:SUBLANES - s + L, :]
    conv_scr[0:SUBLANES, :] = conv_scr[L:L + SUBLANES, :]
    xbc = _silu(conv)
    xs, bm, cm = xbc[:, 0:W], xbc[:, W:2 * W], xbc[:, 2 * W:]

    dt = _softplus(g_ref[...] + dtb_ref[...])
    da = dt * (-jnp.exp(alog_ref[...]))
    a_cum = _dot_exact(_tril(L).astype(F32), da)
    a_row = a_cum.T
    dt_cols = [dt[:, GATE_DT + h:GATE_DT + h + 1] for h in range(N_HEADS)]
    a_cols = [a_cum[:, GATE_DT + h:GATE_DT + h + 1] for h in range(N_HEADS)]
    a_end = [a_cum[L - 1:L, GATE_DT + h:GATE_DT + h + 1] for h in range(N_HEADS)]
    xdt = xs * _expand_heads(dt_cols)
    xdt_b = xdt.astype(BF16)
    xw = (xdt * _expand_heads([jnp.exp(a_end[h] - a_cols[h]) for h in range(N_HEADS)])).astype(BF16)
    exp_a = _expand_heads([jnp.exp(a_cols[h]) for h in range(N_HEADS)])
    causal = _tril(L)
    half = lax.broadcasted_iota(I32, (1, LANES), 1) < HEAD_DIM

    y = dskip_ref[...] * xs
    y_off = []
    for g in range(SSM_GROUPS):
        bg = bm[:, g * LANES:(g + 1) * LANES]
        cg = cm[:, g * LANES:(g + 1) * LANES].astype(BF16)
        scores = _dot_nt(cg, bg.astype(BF16))
        for r in range(N_HEADS // SSM_GROUPS):
            h = g * (N_HEADS // SSM_GROUPS) + r
            seg = a_cols[h] - a_row[GATE_DT + h:GATE_DT + h + 1, :]
            decay = jnp.exp(jnp.where(causal, seg, NEG_INF))
            y = y + _dot((scores * decay).astype(BF16), _mask_head(xdt_b, h))
        st_in = state[g]
        y_off.append(_dot(cg, st_in.astype(BF16)))
        chunk_decay = jnp.where(half, jnp.exp(a_end[2 * g]), jnp.exp(a_end[2 * g + 1]))
        state[g] = chunk_decay * st_in + _dot(bg.T.astype(BF16), xw[:, g * LANES:(g + 1) * LANES])
    y = y + jnp.concatenate(y_off, axis=1) * exp_a
    y = y * _silu(z_ref[...].astype(F32))
    o_ref[...] = _head_rms_norm(y, gain_ref[...]).astype(BF16)


def _ssd_mixer(proj, gates, conv_w, conv_b, dtb_row, alog_row, dskip_row, gain, bsz, seq):
    t = proj.shape[0]
    nc = seq // CHUNK
    row = lambda blk: pl.BlockSpec((CHUNK, GROUP_WIDTH), lambda b, j, blk=blk: (b * nc + j, blk))
    const = lambda shape: pl.BlockSpec(shape, lambda b, j: (0,) * len(shape))
    return pl.pallas_call(
        _ssd_kernel,
        grid=(bsz, nc),
        in_specs=[row(PB_MZ), row(PB_X), row(PB_B), row(PB_C),
                  pl.BlockSpec((CHUNK, LANES), lambda b, j: (b * nc + j, 0)),
                  const((CONV_WIDTH, 3 * GROUP_WIDTH)), const((1, 3 * GROUP_WIDTH)),
                  const((1, LANES)), const((1, LANES)), const((1, GROUP_WIDTH)), const((1, GROUP_WIDTH))],
        out_specs=pl.BlockSpec((CHUNK, GROUP_WIDTH), lambda b, j: (b * nc + j, 0)),
        out_shape=jax.ShapeDtypeStruct((t, GROUP_WIDTH), BF16),
        scratch_shapes=[pltpu.VMEM((CHUNK + SUBLANES, 3 * GROUP_WIDTH), F32),
                        pltpu.VMEM((SSM_GROUPS, LANES, LANES), F32)],
        compiler_params=_cparams("arbitrary", "arbitrary"),
        name="ssd_mixer",
    )(proj, proj, proj, proj, gates, conv_w, conv_b, dtb_row, alog_row, dskip_row, gain)


def _mlstm_kernel(q_ref, k_ref, v_ref, o_gate_ref, g_ref, bias_ref, gain_ref, o_ref, ct, nb, m_row):
    L = CHUNK
    W = GROUP_WIDTH

    @pl.when(pl.program_id(1) == 0)
    def _():
        ct[...] = jnp.zeros_like(ct)
        nb[...] = jnp.zeros_like(nb)
        m_row[...] = jnp.full_like(m_row, MLSTM_M_INIT)

    q = q_ref[...]
    k = k_ref[...]
    v = v_ref[...]
    gate = g_ref[...] + bias_ref[...]
    a_full = _dot_exact(_tril(L).astype(F32), _log_sigmoid(gate))
    a_rows = a_full.T
    g_rows = gate.T
    causal = _tril(L)
    lane = lax.broadcasted_iota(I32, (1, LANES), 1)

    inter_q = _dot(q, ct[...].astype(BF16))
    n_q = _dot(q, nb[...].astype(BF16))
    m_old = m_row[...]
    num = jnp.zeros((L, W), F32)
    inter_cols, den_cols, ws_cols, scale_cols = [], [], [], []
    m_next = m_old
    for h in range(N_HEADS):
        a_col = a_full[:, GATE_LF + h:GATE_LF + h + 1]
        i_col = gate[:, GATE_LI + h:GATE_LI + h + 1]
        a_r = a_rows[GATE_LF + h:GATE_LF + h + 1, :]
        i_r = g_rows[GATE_LI + h:GATE_LI + h + 1, :]
        a_end = a_full[L - 1:L, GATE_LF + h:GATE_LF + h + 1]
        m_in = m_old[:, h:h + 1]
        log_d = jnp.where(causal, a_col - a_r + i_r, NEG_INF)
        log_inter = a_col + m_in
        m_t = jnp.maximum(jnp.max(log_d, axis=1, keepdims=True), log_inter)
        w = _dot_nt(_mask_head(q, h), k) * jnp.exp(log_d - m_t)
        inter = jnp.exp(log_inter - m_t)
        num = num + _dot(w.astype(BF16), _mask_head(v, h))
        den = jnp.sum(w, axis=1, keepdims=True) + inter * n_q[:, h:h + 1]
        inter_cols.append(inter)
        den_cols.append(jnp.maximum(jnp.abs(den), jnp.exp(-m_t)))
        g_col = a_end - a_col + i_col
        m_new = jnp.maximum(a_end + m_in, jnp.max(g_col, axis=0, keepdims=True))
        ws_cols.append(jnp.exp(g_col - m_new))
        scale_cols.append(jnp.exp(a_end + m_in - m_new))
        m_next = jnp.where(lane == h, m_new, m_next)
    hout = (num + _expand_heads(inter_cols) * inter_q) / _expand_heads(den_cols)
    y = _sigmoid(o_gate_ref[...].astype(F32)) * _head_rms_norm(hout, gain_ref[...])
    o_ref[...] = y.astype(BF16)

    kw_t = (k.astype(F32) * _expand_heads(ws_cols)).T.astype(BF16)
    scale_row = _expand_heads(scale_cols)
    same_head = _head_of_lane((W, W), 0) == _head_of_lane((W, W), 1)
    ct[...] = scale_row * ct[...] + jnp.where(same_head, _dot(kw_t, v), 0.0)
    col_is_head = _head_of_lane((W, LANES), 0) == lax.broadcasted_iota(I32, (W, LANES), 1)
    scale_n = scale_cols[N_HEADS - 1]
    for h in range(N_HEADS - 2, -1, -1):
        scale_n = jnp.where(lane == h, scale_cols[h], scale_n)
    nb[...] = scale_n * nb[...] + jnp.where(col_is_head, _dot(kw_t, jnp.ones((L, LANES), BF16)), 0.0)
    m_row[...] = m_next


def _mlstm_mixer(proj, gates, bias_row, gain, bsz, seq):
    t = proj.shape[0]
    nc = seq // CHUNK
    row = lambda blk: pl.BlockSpec((CHUNK, GROUP_WIDTH), lambda b, j, blk=blk: (b * nc + j, blk))
    const = lambda shape: pl.BlockSpec(shape, lambda b, j: (0,) * len(shape))
    return pl.pallas_call(
        _mlstm_kernel,
        grid=(bsz, nc),
        in_specs=[row(PB_LQ), row(PB_LK), row(PB_LV), row(PB_LO),
                  pl.BlockSpec((CHUNK, LANES), lambda b, j: (b * nc + j, 0)),
                  const((1, LANES)), const((1, GROUP_WIDTH))],
        out_specs=pl.BlockSpec((CHUNK, GROUP_WIDTH), lambda b, j: (b * nc + j, 0)),
        out_shape=jax.ShapeDtypeStruct((t, GROUP_WIDTH), BF16),
        scratch_shapes=[pltpu.VMEM((GROUP_WIDTH, GROUP_WIDTH), F32),
                        pltpu.VMEM((GROUP_WIDTH, LANES), F32),
                        pltpu.VMEM((1, LANES), F32)],
        compiler_params=_cparams("arbitrary", "arbitrary"),
        name="mlstm_mixer",
    )(proj, proj, proj, proj, gates, bias_row, gain)


def _outproj_router_kernel(x_ref, ya_ref, ys_ref, ym_ref, yl_ref, wo_ref, mod_ref, wn_ref, wr_ref, br_ref,
                           xo_ref, h2_ref, eid_ref, gcol_ref):
    W = GROUP_WIDTH
    tm = x_ref.shape[0]
    mod = mod_ref[0]
    out = _dot(ya_ref[...], wo_ref[0:W, :])
    out = out + _dot(ys_ref[...], wo_ref[W:2 * W, :])
    out = out + _dot(ym_ref[...], wo_ref[2 * W:3 * W, :])
    out = out + _dot(yl_ref[...], wo_ref[3 * W:4 * W, :])
    x1 = x_ref[...] + mod[2:3, :] * out
    xo_ref[...] = x1
    h2 = _rms_norm(x1, wn_ref[...]) * (1.0 + mod[4:5, :]) + mod[3:4, :]
    h2_ref[...] = h2

    logits_t = (_dot(h2.astype(BF16), wr_ref[...]) + br_ref[...]).T
    row8 = lax.broadcasted_iota(I32, (SUBLANES, tm), 0).astype(F32)
    gl = jnp.where(row8 < N_EXPERT_GROUPS, logits_t[ROUTE_G:ROUTE_G + SUBLANES, :], NEG_INF)
    g_max = jnp.max(gl, axis=0, keepdims=True)
    g_sel = jnp.min(jnp.where(gl == g_max, row8, SUBLANES), axis=0, keepdims=True)
    g_prob = 1.0 / jnp.sum(jnp.exp(gl - g_max), axis=0, keepdims=True)
    el = logits_t[ROUTE_E:ROUTE_E + EXPERTS_PER_GROUP, :]
    for g in range(1, N_EXPERT_GROUPS):
        lo = ROUTE_E + g * EXPERTS_PER_GROUP
        el = jnp.where(g_sel == g, logits_t[lo:lo + EXPERTS_PER_GROUP, :], el)
    m1 = jnp.max(el, axis=0, keepdims=True)
    i1 = jnp.min(jnp.where(el == m1, row8, SUBLANES), axis=0, keepdims=True)
    el2 = jnp.where(row8 == i1, NEG_INF, el)
    m2 = jnp.max(el2, axis=0, keepdims=True)
    i2 = jnp.min(jnp.where(el2 == m2, row8, SUBLANES), axis=0, keepdims=True)
    ratio = jnp.exp(m2 - m1)
    p1 = 1.0 / (1.0 + ratio)
    eid_ref[0:1, :] = (g_sel * EXPERTS_PER_GROUP + i1).astype(I32)
    eid_ref[1:2, :] = (g_sel * EXPERTS_PER_GROUP + i2).astype(I32)
    rows = lax.broadcasted_iota(I32, (LANES, tm), 0)
    gate_rows = jnp.where(rows == 0, g_prob * p1, jnp.where(rows == 1, g_prob * p1 * ratio, 0.0))
    gcol_ref[...] = gate_rows.T


def _outproj_router(x2, ys, w_out, mod, w_norm2, w_route, b_route, seq, tm=256):
    t, d = x2.shape
    spb = seq // tm
    ytile = pl.BlockSpec((tm, GROUP_WIDTH), lambda i: (i, 0))
    return pl.pallas_call(
        _outproj_router_kernel,
        grid=(t // tm,),
        in_specs=[pl.BlockSpec((tm, d), lambda i: (i, 0)), ytile, ytile, ytile, ytile,
                  pl.BlockSpec((d, d), lambda i: (0, 0)),
                  pl.BlockSpec((1, 6, d), lambda i: (i // spb, 0, 0)),
                  pl.BlockSpec((1, d), lambda i: (0, 0)),
                  pl.BlockSpec((d, LANES), lambda i: (0, 0)),
                  pl.BlockSpec((1, LANES), lambda i: (0, 0))],
        out_specs=[pl.BlockSpec((tm, d), lambda i: (i, 0)),
                   pl.BlockSpec((tm, d), lambda i: (i, 0)),
                   pl.BlockSpec((TOP_K, tm), lambda i: (0, i)),
                   pl.BlockSpec((tm, LANES), lambda i: (i, 0))],
        out_shape=[jax.ShapeDtypeStruct((t, d), F32),
                   jax.ShapeDtypeStruct((t, d), F32),
                   jax.ShapeDtypeStruct((TOP_K, t), I32),
                   jax.ShapeDtypeStruct((t, LANES), F32)],
        compiler_params=_cparams("arbitrary"),
        name="outproj_router",
    )(x2, *ys, w_out, mod, w_norm2, w_route, b_route)


def _rank_kernel(eid_ref, rank_ref, count_ref, carry):
    @pl.when(pl.program_id(0) == 0)
    def _():
        carry[...] = jnp.zeros_like(carry)

    tr = eid_ref.shape[1]
    expert = lax.broadcasted_iota(I32, (N_EXPERTS, tr), 0)
    before = (lax.broadcasted_iota(I32, (tr, tr), 0) < lax.broadcasted_iota(I32, (tr, tr), 1)).astype(BF16)
    base = carry[...]
    for k in range(TOP_K):
        onehot = (expert == eid_ref[k:k + 1, :]).astype(F32)
        prefix = _dot(onehot.astype(BF16), before)
        rank_ref[k:k + 1, :] = jnp.sum(onehot * (base + prefix), axis=0, keepdims=True).astype(I32)
        base = base + jnp.sum(onehot, axis=1, keepdims=True)
    carry[...] = base
    count_ref[...] = jnp.broadcast_to(base, count_ref.shape)


def _expert_ranks(eids, tr=512):
    t = eids.shape[1]
    return pl.pallas_call(
        _rank_kernel,
        grid=(t // tr,),
        in_specs=[pl.BlockSpec((TOP_K, tr), lambda i: (0, i))],
        out_specs=[pl.BlockSpec((TOP_K, tr), lambda i: (0, i)),
                   pl.BlockSpec((N_EXPERTS, LANES), lambda i: (0, 0))],
        out_shape=[jax.ShapeDtypeStruct((TOP_K, t), I32),
                   jax.ShapeDtypeStruct((N_EXPERTS, LANES), F32)],
        scratch_shapes=[pltpu.VMEM((N_EXPERTS, 1), F32)],
        compiler_params=_cparams("arbitrary"),
        name="expert_ranks",
    )(eids)


def _dest_kernel(pstart_ref, eid_ref, rank_ref, dest_ref):
    e = eid_ref[...]
    dest = rank_ref[...]
    for j in range(N_EXPERTS):
        dest = dest + jnp.where(e == j, pstart_ref[j], 0)
    dest_ref[...] = dest


def _dest_rows(p_starts, eids, ranks, tm=2048):
    t = eids.shape[1]
    grid_spec = pltpu.PrefetchScalarGridSpec(
        num_scalar_prefetch=1,
        grid=(t // tm,),
        in_specs=[pl.BlockSpec((TOP_K, tm), lambda i, ps: (0, i)),
                  pl.BlockSpec((TOP_K, tm), lambda i, ps: (0, i))],
        out_specs=pl.BlockSpec((TOP_K, tm), lambda i, ps: (0, i)))
    return pl.pallas_call(
        _dest_kernel,
        grid_spec=grid_spec,
        out_shape=jax.ShapeDtypeStruct((TOP_K, t), I32),
        compiler_params=_cparams("arbitrary"),
        name="dest_rows",
    )(p_starts, eids, ranks)


ROW_TILE = IDX_CHUNK // TOP_K


def _load_tile_indices(idx_hbm, idx_smem, idx_sem):
    copy = pltpu.make_async_copy(idx_hbm.at[pl.ds(pl.program_id(0) * IDX_CHUNK, IDX_CHUNK)], idx_smem, idx_sem)
    copy.start()
    copy.wait()


def _dispatch_kernel(idx_hbm, h_ref, _, xb_hbm, idx_smem, idx_sem, row_sem):
    _load_tile_indices(idx_hbm, idx_smem, idx_sem)

    def row_copy(r, dst_row):
        return pltpu.make_async_copy(h_ref.at[pl.ds(r, 1)], xb_hbm.at[pl.ds(dst_row, 1)], row_sem)

    def issue(r, carry):
        for k in range(TOP_K):
            row_copy(r, idx_smem[k * ROW_TILE + r]).start()
        return carry

    lax.fori_loop(0, ROW_TILE, issue, 0, unroll=8)

    def drain(r, carry):
        for k in range(TOP_K):
            row_copy(0, 0).wait()
        return carry

    lax.fori_loop(0, ROW_TILE, drain, 0, unroll=8)


def _dispatch(idx_tiles, h2, dst_rows):
    t, d = h2.shape
    any_spec = pl.BlockSpec(memory_space=pl.ANY)
    return pl.pallas_call(
        _dispatch_kernel,
        grid=(t // ROW_TILE,),
        in_specs=[any_spec, pl.BlockSpec((ROW_TILE, d), lambda i: (i, 0)), any_spec],
        out_specs=any_spec,
        out_shape=jax.ShapeDtypeStruct((dst_rows, d), h2.dtype),
        scratch_shapes=[pltpu.SMEM((IDX_CHUNK,), I32), pltpu.SemaphoreType.DMA, pltpu.SemaphoreType.DMA],
        input_output_aliases={2: 0},
        compiler_params=_cparams("arbitrary"),
        name="moe_dispatch",
    )(idx_tiles, h2, jnp.zeros((dst_rows, d), h2.dtype))


def _expert_kernel(be_ref, nu_ref, x_ref, wg_ref, wu_ref, wd_ref, y_ref):
    @pl.when(pl.program_id(0) < nu_ref[0])
    def _():
        x = x_ref[...].astype(BF16)
        a = _silu(_dot(x, wg_ref[0])) * _dot(x, wu_ref[0])
        y_ref[...] = _dot(a.astype(BF16), wd_ref[0])


def _expert_mlp(block_e, n_used, xb, w_gate, w_up, w_down):
    p, d = xb.shape
    de = w_gate.shape[2]
    blk = lambda b, be, nu: (jnp.minimum(b, nu[0] - 1), 0)
    grid_spec = pltpu.PrefetchScalarGridSpec(
        num_scalar_prefetch=2,
        grid=(p // MOE_BLOCK,),
        in_specs=[pl.BlockSpec((MOE_BLOCK, d), blk),
                  pl.BlockSpec((1, d, de), lambda b, be, nu: (be[b], 0, 0)),
                  pl.BlockSpec((1, d, de), lambda b, be, nu: (be[b], 0, 0)),
                  pl.BlockSpec((1, de, d), lambda b, be, nu: (be[b], 0, 0))],
        out_specs=pl.BlockSpec((MOE_BLOCK, d), blk))
    return pl.pallas_call(
        _expert_kernel,
        grid_spec=grid_spec,
        out_shape=jax.ShapeDtypeStruct((p, d), F32),
        compiler_params=_cparams("arbitrary"),
        name="expert_mlp",
    )(block_e, n_used, xb, w_gate, w_up, w_down)


def _combine_kernel(idx_hbm, yb_hbm, x_ref, gcol_ref, mod_ref, wnf_ref, o_ref, ybuf, idx_smem, idx_sem, row_sem,
                    *, final):
    _load_tile_indices(idx_hbm, idx_smem, idx_sem)

    def row_copy(src_row, k, r):
        return pltpu.make_async_copy(yb_hbm.at[pl.ds(src_row, 1)], ybuf.at[k, pl.ds(r, 1)], row_sem)

    def issue(r, carry):
        for k in range(TOP_K):
            row_copy(idx_smem[k * ROW_TILE + r], k, r).start()
        return carry

    lax.fori_loop(0, ROW_TILE, issue, 0, unroll=8)

    def drain(r, carry):
        for k in range(TOP_K):
            row_copy(0, k, 0).wait()
        return carry

    lax.fori_loop(0, ROW_TILE, drain, 0, unroll=8)

    gc = gcol_ref[...]
    moe = gc[:, 0:1] * ybuf[0] + gc[:, 1:2] * ybuf[1]
    x2 = x_ref[...] + mod_ref[0][5:6, :] * moe
    o_ref[...] = _rms_norm(x2, wnf_ref[...]) if final else x2


def _combine(idx_tiles, yb, x2, gcol, mod, w_norm_final, seq, final):
    t, d = x2.shape
    tm = ROW_TILE
    spb = seq // tm
    any_spec = pl.BlockSpec(memory_space=pl.ANY)
    return pl.pallas_call(
        functools.partial(_combine_kernel, final=final),
        grid=(t // tm,),
        in_specs=[any_spec, any_spec,
                  pl.BlockSpec((tm, d), lambda i: (i, 0)),
                  pl.BlockSpec((tm, LANES), lambda i: (i, 0)),
                  pl.BlockSpec((1, 6, d), lambda i: (i // spb, 0, 0)),
                  pl.BlockSpec((1, d), lambda i: (0, 0))],
        out_specs=pl.BlockSpec((tm, d), lambda i: (i, 0)),
        out_shape=jax.ShapeDtypeStruct((t, d), F32),
        scratch_shapes=[pltpu.VMEM((TOP_K, tm, d), F32), pltpu.SMEM((IDX_CHUNK,), I32),
                        pltpu.SemaphoreType.DMA, pltpu.SemaphoreType.DMA],
        compiler_params=_cparams("arbitrary"),
        name="moe_combine",
    )(idx_tiles, yb, x2, gcol, mod, w_norm_final)


def _lane_row(pieces, width=LANES):
    row = jnp.zeros((width,), F32)
    for off, vec in pieces.items():
        row = row.at[off:off + vec.shape[0]].set(vec.astype(F32))
    return row.reshape(1, width)


def _split_w_in(w_in):
    gw, nh = GROUP_WIDTH, N_HEADS
    widths = [gw, gw, gw, nh, gw, gw, gw, 3 * gw, nh, gw, gw, gw, gw, nh, nh]
    cuts, acc = [], 0
    for w in widths[:-1]:
        acc += w
        cuts.append(acc)
    (aq, ak, av, af, su, sv, mz, mxbc, mdt, lq, lk, lv, lo, li, lf) = jnp.split(w_in, cuts, axis=1)
    scale = HEAD_DIM ** -0.5
    wmain = jnp.concatenate([ak, su, sv, mz, mxbc, lq, lk * scale, lv, lo], axis=1).astype(BF16)
    wqt = (aq * scale).T.astype(BF16)
    wvt = av.T.astype(BF16)
    wg = jnp.zeros((w_in.shape[0], LANES), F32)
    for off, w in ((GATE_AF, af), (GATE_DT, mdt), (GATE_LI, li), (GATE_LF, lf)):
        wg = wg.at[:, off:off + nh].set(w)
    return wmain, wqt, wvt, wg.astype(BF16)


def _moe_layer(x1, h2, eids, gcol, mod_l, w_gate, w_up, w_down, w_norm_final, seq, final):
    t, d = x1.shape
    ranks, counts = _expert_ranks(eids)
    counts = counts[:, 0].astype(I32)
    padded = ((counts + MOE_BLOCK - 1) // MOE_BLOCK) * MOE_BLOCK
    p_ends = jnp.cumsum(padded)
    p_starts = (p_ends - padded).astype(I32)
    n_blocks = (t * TOP_K) // MOE_BLOCK + N_EXPERTS
    blocks = jnp.arange(n_blocks, dtype=I32)
    block_e = jnp.minimum(jnp.searchsorted(p_ends, blocks * MOE_BLOCK, side="right"), N_EXPERTS - 1).astype(I32)
    n_used = (p_ends[-1] // MOE_BLOCK).astype(I32)
    block_e = jnp.where(blocks < n_used, block_e, block_e[n_used - 1])
    dest = _dest_rows(p_starts, eids, ranks)
    idx_tiles = dest.reshape(TOP_K, t // ROW_TILE, ROW_TILE).transpose(1, 0, 2).reshape(-1)
    xb = _dispatch(idx_tiles, h2, n_blocks * MOE_BLOCK)
    yb = _expert_mlp(block_e, n_used.reshape(1), xb, w_gate, w_up, w_down)
    return _combine(idx_tiles, yb, x1, gcol, mod_l, w_norm_final, seq, final)


def kernel(x, c, w_in, w_out, w_mix_norm, attn_f_bias, sg_w, sg_b, ssm_conv_w, ssm_conv_b, ssm_dt_bias,
           ssm_a_log, ssm_d, mlstm_i_bias, mlstm_f_bias, w_ada, b_ada, w_norm1, w_norm2, w_router_group,
           b_router_group, w_router_expert, b_router_expert, w_expert_gate, w_expert_up, w_expert_down,
           w_norm_final):
    bsz, seq, d = x.shape
    depth = w_in.shape[0]
    gw = GROUP_WIDTH
    mod = _ada_modulation(c, w_ada, b_ada).reshape(depth, bsz, 6, d)
    x2 = x.reshape(bsz * seq, d)
    wnf = w_norm_final.reshape(1, d)
    for l in range(depth):
        wmain, wqt, wvt, wg = _split_w_in(w_in[l])
        gains = w_mix_norm[l].reshape(N_HEADS, 1, gw)
        proj, qt, vt, gates = _inproj(x2, mod[l], w_norm1[l].reshape(1, d), wmain, wqt, wvt, wg, bsz, seq)
        f2 = _forget_cumsum(gates, _lane_row({GATE_AF: attn_f_bias[l]}), bsz, seq)
        y_attn = _attention(proj, qt, vt, f2, gains[0], bsz, seq)
        sg_bias_full = jnp.repeat(sg_b[l].T, HEAD_DIM, axis=1)
        y_sg = _spatial_gating(proj, gains[1], sg_w[l], sg_bias_full)
        y_ssm = _ssd_mixer(proj, gates, ssm_conv_w[l], ssm_conv_b[l].reshape(1, -1),
                           _lane_row({GATE_DT: ssm_dt_bias[l]}), _lane_row({GATE_DT: ssm_a_log[l]}),
                           jnp.repeat(ssm_d[l], HEAD_DIM).reshape(1, gw), gains[2], bsz, seq)
        y_ml = _mlstm_mixer(proj, gates, _lane_row({GATE_LI: mlstm_i_bias[l], GATE_LF: mlstm_f_bias[l]}),
                            gains[3], bsz, seq)
        w_route = jnp.zeros((d, LANES), F32)
        w_route = w_route.at[:, ROUTE_G:ROUTE_G + N_EXPERT_GROUPS].set(w_router_group[l])
        w_route = w_route.at[:, ROUTE_E:ROUTE_E + N_EXPERTS].set(w_router_expert[l]).astype(BF16)
        b_route = _lane_row({ROUTE_G: b_router_group[l], ROUTE_E: b_router_expert[l]})
        x1, h2, eids, gcol = _outproj_router(x2, (y_attn, y_sg, y_ssm, y_ml), w_out[l].astype(BF16), mod[l],
                                             w_norm2[l].reshape(1, d), w_route, b_route, seq)
        x2 = _moe_layer(x1, h2, eids, gcol, mod[l], w_expert_gate[l].astype(BF16), w_expert_up[l].astype(BF16),
                        w_expert_down[l].astype(BF16), wnf, seq, final=(l == depth - 1))
    return x2.reshape(bsz, seq, d)
```

```python
import functools

import jax
import jax.numpy as jnp
from jax import lax
from jax.experimental import pallas as pl
from jax.experimental.pallas import tpu as pltpu

F32 = jnp.float32
BF16 = jnp.bfloat16
I32 = jnp.int32

LANES = 128
SUBLANES = 8
HEAD_DIM = 64
N_HEADS = 4
GROUP_WIDTH = 256
CHUNK = 128
MIX_TILE = 4 * CHUNK
SSM_GROUPS = 2
CONV_WIDTH = 4
N_EXPERT_GROUPS = 4
EXPERTS_PER_GROUP = 8
N_EXPERTS = N_EXPERT_GROUPS * EXPERTS_PER_GROUP
TOP_K = 2
MOE_BLOCK = 256
NORM_EPS = 1e-6
MLSTM_M_INIT = -1e30
NEG_INF = float("-inf")
VMEM_LIMIT_BYTES = 48 * 1024 * 1024
IDX_CHUNK = 1024

(PB_K, PB_SU, PB_SV, PB_MZ, PB_X, PB_B, PB_C, PB_LQ, PB_LK, PB_LV, PB_LO) = range(11)
PROJ_COLS = 11 * GROUP_WIDTH
LOG2_E = 1.4426950408889634
GATE_AF, GATE_DT, GATE_LI, GATE_LF = 0, 4, 8, 12
ROUTE_G, ROUTE_E = 0, 8

NT_DIMS = (((1,), (1,)), ((), ()))


def _cparams(*sem):
    return pltpu.CompilerParams(dimension_semantics=sem, vmem_limit_bytes=VMEM_LIMIT_BYTES)


def _dot(a, b):
    return jnp.dot(a, b, preferred_element_type=F32)


def _dot_nt(a, b):
    return lax.dot_general(a, b, NT_DIMS, preferred_element_type=F32)


def _dot_exact(a, b):
    return jnp.dot(a, b, preferred_element_type=F32, precision=lax.Precision.HIGHEST)


def _head_of_lane(shape, axis=1):
    return lax.broadcasted_iota(I32, shape, axis) // HEAD_DIM


def _sigmoid(x):
    return 1.0 / (1.0 + jnp.exp(-x))


def _silu(x):
    return x * _sigmoid(x)


def _log_sigmoid(x):
    return jnp.minimum(x, 0.0) - jnp.log1p(jnp.exp(-jnp.abs(x)))


def _softplus(x):
    return jnp.maximum(x, 0.0) + jnp.log1p(jnp.exp(-jnp.abs(x)))


def _gelu_tanh(x):
    return 0.5 * x * (1.0 + jnp.tanh(0.7978845608028654 * (x + 0.044715 * (x * x * x))))


def _expand_heads(cols, width=GROUP_WIDTH):
    rows = cols[0].shape[0]
    head = _head_of_lane((rows, width))
    out = jnp.broadcast_to(cols[N_HEADS - 1], (rows, width))
    for h in range(N_HEADS - 2, -1, -1):
        out = jnp.where(head == h, jnp.broadcast_to(cols[h], (rows, width)), out)
    return out


def _mask_head(x, h):
    return jnp.where(_head_of_lane(x.shape) == h, x, jnp.zeros_like(x))


def _head_rms_norm(y, gain):
    head = _head_of_lane(y.shape)
    sq = y * y
    cols = [jnp.sum(jnp.where(head == h, sq, 0.0), axis=1, keepdims=True) * (1.0 / HEAD_DIM)
            for h in range(N_HEADS)]
    return y * lax.rsqrt(_expand_heads(cols) + NORM_EPS) * gain


def _rms_norm(x, gain):
    ms = jnp.mean(x * x, axis=1, keepdims=True)
    return x * lax.rsqrt(ms + NORM_EPS) * gain


def _tril(n, strict=False):
    r = lax.broadcasted_iota(I32, (n, n), 0)
    c = lax.broadcasted_iota(I32, (n, n), 1)
    return (r > c) if strict else (r >= c)


def _ada_kernel(c_ref, w_ref, b_ref, o_ref):
    o_ref[0] = _dot_exact(_silu(c_ref[...]), w_ref[0]) + b_ref[0]


def _ada_modulation(c, w_ada, b_ada):
    depth, d, d6 = w_ada.shape
    bsz = c.shape[0]
    return pl.pallas_call(
        _ada_kernel,
        grid=(depth, d6 // d),
        in_specs=[pl.BlockSpec((bsz, d), lambda l, j: (0, 0)),
                  pl.BlockSpec((1, d, d), lambda l, j: (l, 0, j)),
                  pl.BlockSpec((1, 1, d), lambda l, j: (l, 0, j))],
        out_specs=pl.BlockSpec((1, bsz, d), lambda l, j: (l, 0, j)),
        out_shape=jax.ShapeDtypeStruct((depth, bsz, d6), F32),
        compiler_params=_cparams("arbitrary", "arbitrary"),
        name="ada_modulation",
    )(c, w_ada, b_ada.reshape(depth, 1, d6))


def _inproj_kernel(x_ref, mod_ref, wn_ref, wmain_ref, wqt_ref, wvt_ref, wg_ref,
                   proj_ref, qt_ref, vt_ref, gates_ref, h_scr):
    mod = mod_ref[0]
    h = _rms_norm(x_ref[...], wn_ref[...]) * (1.0 + mod[1:2, :]) + mod[0:1, :]
    h_scr[...] = h.astype(BF16)
    for c0 in range(0, PROJ_COLS, GROUP_WIDTH):
        proj_ref[:, c0:c0 + GROUP_WIDTH] = _dot(h_scr[...], wmain_ref[:, c0:c0 + GROUP_WIDTH]).astype(BF16)
    qt_ref[0] = (_dot_nt(wqt_ref[...], h_scr[...]) * LOG2_E).astype(BF16)
    vt_ref[0] = _dot_nt(wvt_ref[...], h_scr[...]).astype(BF16)
    gates_ref[...] = _dot(h_scr[...], wg_ref[...])


def _inproj(x2, mod, w_norm, wmain, wqt, wvt, wg, bsz, seq, tm=512):
    t, d = x2.shape
    spb = seq // tm
    tspec = pl.BlockSpec((1, GROUP_WIDTH, tm), lambda i: (i // spb, 0, i % spb))
    return pl.pallas_call(
        _inproj_kernel,
        grid=(t // tm,),
        in_specs=[pl.BlockSpec((tm, d), lambda i: (i, 0)),
                  pl.BlockSpec((1, 6, d), lambda i: (i // spb, 0, 0)),
                  pl.BlockSpec((1, d), lambda i: (0, 0)),
                  pl.BlockSpec((d, PROJ_COLS), lambda i: (0, 0)),
                  pl.BlockSpec((GROUP_WIDTH, d), lambda i: (0, 0)),
                  pl.BlockSpec((GROUP_WIDTH, d), lambda i: (0, 0)),
                  pl.BlockSpec((d, LANES), lambda i: (0, 0))],
        out_specs=[pl.BlockSpec((tm, PROJ_COLS), lambda i: (i, 0)), tspec, tspec,
                   pl.BlockSpec((tm, LANES), lambda i: (i, 0))],
        out_shape=[jax.ShapeDtypeStruct((t, PROJ_COLS), BF16),
                   jax.ShapeDtypeStruct((bsz, GROUP_WIDTH, seq), BF16),
                   jax.ShapeDtypeStruct((bsz, GROUP_WIDTH, seq), BF16),
                   jax.ShapeDtypeStruct((t, LANES), F32)],
        scratch_shapes=[pltpu.VMEM((tm, d), BF16)],
        compiler_params=_cparams("arbitrary"),
        name="norm_inproj",
    )(x2, mod, w_norm, wmain, wqt, wvt, wg)


def _fcum_kernel(g_ref, b_ref, f_ref, carry):
    @pl.when(pl.program_id(1) == 0)
    def _():
        carry[...] = jnp.zeros_like(carry)

    tb = g_ref.shape[0]
    ls = _log_sigmoid(g_ref[...] + b_ref[...])
    cum = _dot_exact(_tril(tb).astype(F32), ls) + carry[...]
    carry[...] = cum[tb - 1:tb, :]
    f_ref[...] = cum * LOG2_E


def _forget_cumsum(gates, f_bias_row, bsz, seq, tb=512):
    spb = seq // tb
    return pl.pallas_call(
        _fcum_kernel,
        grid=(bsz, spb),
        in_specs=[pl.BlockSpec((tb, LANES), lambda b, j: (b * spb + j, 0)),
                  pl.BlockSpec((1, LANES), lambda b, j: (0, 0))],
        out_specs=pl.BlockSpec((tb, LANES), lambda b, j: (b * spb + j, 0)),
        out_shape=jax.ShapeDtypeStruct((bsz * seq, LANES), F32),
        scratch_shapes=[pltpu.VMEM((1, LANES), F32)],
        compiler_params=_cparams("arbitrary", "arbitrary"),
        name="forget_cumsum",
    )(gates, f_bias_row)


def _attn_kernel(qi_ref, kj_ref, k_ref, qt_ref, vt_ref, f_ref, gain_ref, o_ref, qtm, m_s, l_s, acc):
    p = pl.program_id(1)
    qi, kj = qi_ref[p], kj_ref[p]
    tk = k_ref.shape[0]
    tq = qt_ref.shape[2]

    @pl.when(kj == 0)
    def _():
        qt = qt_ref[0]
        head = _head_of_lane(qt.shape, 0)
        for h in range(N_HEADS):
            qtm[h] = jnp.where(head == h, qt, jnp.zeros_like(qt))
        m_s[...] = jnp.full_like(m_s, NEG_INF)
        l_s[...] = jnp.zeros_like(l_s)
        acc[...] = jnp.zeros_like(acc)

    def step(diagonal):
        k = k_ref[...]
        f = f_ref[...]
        if diagonal:
            visible = lax.broadcasted_iota(I32, (tk, tq), 0) <= lax.broadcasted_iota(I32, (tk, tq), 1)
        for h in range(N_HEADS):
            s = _dot(k, qtm[h]) - f[:, GATE_AF + h:GATE_AF + h + 1]
            if diagonal:
                s = jnp.where(visible, s, NEG_INF)
            m_prev = m_s[h]
            m_new = jnp.maximum(m_prev, jnp.max(s, axis=0, keepdims=True))
            alpha = jnp.exp2(m_prev - m_new)
            pr = jnp.exp2(s - m_new)
            l_s[h] = alpha * l_s[h] + jnp.sum(pr, axis=0, keepdims=True)
            pv = _dot(vt_ref[0, h * HEAD_DIM:(h + 1) * HEAD_DIM, :], pr.astype(BF16))
            acc[h] = alpha * acc[h] + pv
            m_s[h] = m_new

    @pl.when(kj < qi)
    def _():
        step(False)

    @pl.when(kj == qi)
    def _():
        step(True)
        yt = jnp.concatenate([acc[h] / l_s[h] for h in range(N_HEADS)], axis=0)
        o_ref[...] = _head_rms_norm(yt.T, gain_ref[...]).astype(BF16)


def _attention(proj, qt, vt, f2, gain, bsz, seq, tq=512):
    nq = seq // tq
    pairs = [(i, j) for i in range(nq) for j in range(i + 1)]
    qi = jnp.asarray([a for a, _ in pairs], I32)
    kj = jnp.asarray([b for _, b in pairs], I32)
    t = bsz * seq
    grid_spec = pltpu.PrefetchScalarGridSpec(
        num_scalar_prefetch=2,
        grid=(bsz, len(pairs)),
        in_specs=[pl.BlockSpec((tq, GROUP_WIDTH), lambda b, p, qi, kj: (b * nq + kj[p], PB_K)),
                  pl.BlockSpec((1, GROUP_WIDTH, tq), lambda b, p, qi, kj: (b, 0, qi[p])),
                  pl.BlockSpec((1, GROUP_WIDTH, tq), lambda b, p, qi, kj: (b, 0, kj[p])),
                  pl.BlockSpec((tq, LANES), lambda b, p, qi, kj: (b * nq + kj[p], 0)),
                  pl.BlockSpec((1, GROUP_WIDTH), lambda b, p, qi, kj: (0, 0))],
        out_specs=pl.BlockSpec((tq, GROUP_WIDTH), lambda b, p, qi, kj: (b * nq + qi[p], 0)),
        scratch_shapes=[pltpu.VMEM((N_HEADS, GROUP_WIDTH, tq), BF16),
                        pltpu.VMEM((N_HEADS, 1, tq), F32),
                        pltpu.VMEM((N_HEADS, 1, tq), F32),
                        pltpu.VMEM((N_HEADS, HEAD_DIM, tq), F32)])
    return pl.pallas_call(
        _attn_kernel,
        grid_spec=grid_spec,
        out_shape=jax.ShapeDtypeStruct((t, GROUP_WIDTH), BF16),
        compiler_params=_cparams("arbitrary", "arbitrary"),
        name="fox_attention",
    )(qi, kj, proj, qt, vt, f2, gain)


def _sg_kernel(u_ref, v_ref, gain_ref, w_ref, b_ref, o_ref):
    tm = u_ref.shape[0]
    u = _gelu_tanh(u_ref[...].astype(F32))
    v = _head_rms_norm(_gelu_tanh(v_ref[...].astype(F32)), gain_ref[...])
    causal = _tril(CHUNK)
    ws = [jnp.where(causal, w_ref[h], 0.0).astype(BF16) for h in range(N_HEADS)]
    for c0 in range(0, tm, CHUNK):
        vc = v[c0:c0 + CHUNK, :].astype(BF16)
        mixed = b_ref[...]
        for h in range(N_HEADS):
            mixed = mixed + _dot(ws[h], _mask_head(vc, h))
        o_ref[c0:c0 + CHUNK, :] = (u[c0:c0 + CHUNK, :] * mixed).astype(BF16)


def _spatial_gating(proj, gain, sg_w, sg_bias_full, tm=512):
    t = proj.shape[0]
    return pl.pallas_call(
        _sg_kernel,
        grid=(t // tm,),
        in_specs=[pl.BlockSpec((tm, GROUP_WIDTH), lambda i: (i, PB_SU)),
                  pl.BlockSpec((tm, GROUP_WIDTH), lambda i: (i, PB_SV)),
                  pl.BlockSpec((1, GROUP_WIDTH), lambda i: (0, 0)),
                  pl.BlockSpec((N_HEADS, CHUNK, CHUNK), lambda i: (0, 0, 0)),
                  pl.BlockSpec((CHUNK, GROUP_WIDTH), lambda i: (0, 0))],
        out_specs=pl.BlockSpec((tm, GROUP_WIDTH), lambda i: (i, 0)),
        out_shape=jax.ShapeDtypeStruct((t, GROUP_WIDTH), BF16),
        compiler_params=_cparams("arbitrary"),
        name="spatial_gating",
    )(proj, proj, gain, sg_w, sg_bias_full)


def _ssd_kernel(z_ref, x_ref, b_ref, c_ref, g_ref, cw_ref, cb_ref, dtb_ref, alog_ref, dskip_ref, gain_ref,
                o_ref, conv_scr, xbc_scr, state):
    L = CHUNK
    W = GROUP_WIDTH
    tile = x_ref.shape[0]

    @pl.when(pl.program_id(1) == 0)
    def _():
        conv_scr[0:SUBLANES, :] = jnp.zeros((SUBLANES, 3 * W), F32)
        state[...] = jnp.zeros_like(state)

    conv_scr[SUBLANES:, 0:W] = x_ref[...].astype(F32)
    conv_scr[SUBLANES:, W:2 * W] = b_ref[...].astype(F32)
    conv_scr[SUBLANES:, 2 * W:] = c_ref[...].astype(F32)
    cw = cw_ref[...]
    conv = cb_ref[...] + cw[CONV_WIDTH - 1:CONV_WIDTH, :] * conv_scr[SUBLANES:, :]
    for s in range(1, CONV_WIDTH):
        conv = conv + cw[CONV_WIDTH - 1 - s:CONV_WIDTH - s, :] * conv_scr[SUBLANES - s:SUBLANES - s + tile, :]
    conv_scr[0:SUBLANES, :] = conv_scr[tile:tile + SUBLANES, :]
    xbc_scr[...] = _silu(conv)

    neg_a = -jnp.exp(alog_ref[...])
    for c0 in range(0, tile, L):
        rows = slice(c0, c0 + L)
        dt = _softplus(g_ref[rows, :] + dtb_ref[...])
        y = _ssd_chunk(xbc_scr[rows, 0:W], xbc_scr[rows, W:2 * W], xbc_scr[rows, 2 * W:], dt, dt * neg_a,
                       dskip_ref[...], state)
        y = y * _silu(z_ref[rows, :].astype(F32))
        o_ref[rows, :] = _head_rms_norm(y, gain_ref[...]).astype(BF16)


def _ssd_chunk(xs, bm, cm, dt, da, dskip, state):
    L = CHUNK
    a_cum = _dot_exact(_tril(L).astype(F32), da)
    a_row = a_cum.T
    dt_cols = [dt[:, GATE_DT + h:GATE_DT + h + 1] for h in range(N_HEADS)]
    a_cols = [a_cum[:, GATE_DT + h:GATE_DT + h + 1] for h in range(N_HEADS)]
    a_end = [a_cum[L - 1:L, GATE_DT + h:GATE_DT + h + 1] for h in range(N_HEADS)]
    xdt = xs * _expand_heads(dt_cols)
    xdt_b = xdt.astype(BF16)
    xw = (xdt * _expand_heads([jnp.exp(a_end[h] - a_cols[h]) for h in range(N_HEADS)])).astype(BF16)
    exp_a = _expand_heads([jnp.exp(a_cols[h]) for h in range(N_HEADS)])
    causal = _tril(L)
    half = lax.broadcasted_iota(I32, (1, LANES), 1) < HEAD_DIM

    y = dskip * xs
    y_off = []
    for g in range(SSM_GROUPS):
        bg = bm[:, g * LANES:(g + 1) * LANES]
        cg = cm[:, g * LANES:(g + 1) * LANES].astype(BF16)
        scores = _dot_nt(cg, bg.astype(BF16))
        for r in range(N_HEADS // SSM_GROUPS):
            h = g * (N_HEADS // SSM_GROUPS) + r
            seg = a_cols[h] - a_row[GATE_DT + h:GATE_DT + h + 1, :]
            decay = jnp.exp(jnp.where(causal, seg, NEG_INF))
            y = y + _dot((scores * decay).astype(BF16), _mask_head(xdt_b, h))
        st_in = state[g]
        y_off.append(_dot(cg, st_in.astype(BF16)))
        chunk_decay = jnp.where(half, jnp.exp(a_end[2 * g]), jnp.exp(a_end[2 * g + 1]))
        state[g] = chunk_decay * st_in + _dot(bg.T.astype(BF16), xw[:, g * LANES:(g + 1) * LANES])
    return y + jnp.concatenate(y_off, axis=1) * exp_a


def _ssd_mixer(proj, gates, conv_w, conv_b, dtb_row, alog_row, dskip_row, gain, bsz, seq, tile=MIX_TILE):
    t = proj.shape[0]
    nc = seq // tile
    row = lambda blk: pl.BlockSpec((tile, GROUP_WIDTH), lambda b, j, blk=blk: (b * nc + j, blk))
    const = lambda shape: pl.BlockSpec(shape, lambda b, j: (0,) * len(shape))
    return pl.pallas_call(
        _ssd_kernel,
        grid=(bsz, nc),
        in_specs=[row(PB_MZ), row(PB_X), row(PB_B), row(PB_C),
                  pl.BlockSpec((tile, LANES), lambda b, j: (b * nc + j, 0)),
                  const((CONV_WIDTH, 3 * GROUP_WIDTH)), const((1, 3 * GROUP_WIDTH)),
                  const((1, LANES)), const((1, LANES)), const((1, GROUP_WIDTH)), const((1, GROUP_WIDTH))],
        out_specs=pl.BlockSpec((tile, GROUP_WIDTH), lambda b, j: (b * nc + j, 0)),
        out_shape=jax.ShapeDtypeStruct((t, GROUP_WIDTH), BF16),
        scratch_shapes=[pltpu.VMEM((tile + SUBLANES, 3 * GROUP_WIDTH), F32),
                        pltpu.VMEM((tile, 3 * GROUP_WIDTH), F32),
                        pltpu.VMEM((SSM_GROUPS, LANES, LANES), F32)],
        compiler_params=_cparams("arbitrary", "arbitrary"),
        name="ssd_mixer",
    )(proj, proj, proj, proj, gates, conv_w, conv_b, dtb_row, alog_row, dskip_row, gain)


def _mlstm_kernel(q_ref, k_ref, v_ref, o_gate_ref, g_ref, bias_ref, gain_ref, o_ref, ct, nb, m_row):
    L = CHUNK
    W = GROUP_WIDTH

    @pl.when(pl.program_id(1) == 0)
    def _():
        ct[...] = jnp.zeros_like(ct)
        nb[...] = jnp.zeros_like(nb)
        m_row[...] = jnp.full_like(m_row, MLSTM_M_INIT)

    def chunk(c, carry):
        rows = pl.ds(pl.multiple_of(c * L, L), L)
        y = _mlstm_chunk(q_ref[rows, :], k_ref[rows, :], v_ref[rows, :], g_ref[rows, :] + bias_ref[...],
                         ct, nb, m_row)
        y = _sigmoid(o_gate_ref[rows, :].astype(F32)) * _head_rms_norm(y, gain_ref[...])
        o_ref[rows, :] = y.astype(BF16)
        return carry

    lax.fori_loop(0, q_ref.shape[0] // L, chunk, 0)


def _mlstm_chunk(q, k, v, gate, ct, nb, m_row):
    L = CHUNK
    W = GROUP_WIDTH
    a_full = _dot_exact(_tril(L).astype(F32), _log_sigmoid(gate))
    a_rows = a_full.T
    g_rows = gate.T
    causal = _tril(L)
    lane = lax.broadcasted_iota(I32, (1, LANES), 1)

    inter_q = _dot(q, ct[...].astype(BF16))
    n_q = _dot(q, nb[...].astype(BF16))
    m_old = m_row[...]
    num = jnp.zeros((L, W), F32)
    inter_cols, den_cols, ws_cols, scale_cols = [], [], [], []
    m_next = m_old
    for h in range(N_HEADS):
        a_col = a_full[:, GATE_LF + h:GATE_LF + h + 1]
        i_col = gate[:, GATE_LI + h:GATE_LI + h + 1]
        a_r = a_rows[GATE_LF + h:GATE_LF + h + 1, :]
        i_r = g_rows[GATE_LI + h:GATE_LI + h + 1, :]
        a_end = a_full[L - 1:L, GATE_LF + h:GATE_LF + h + 1]
        m_in = m_old[:, h:h + 1]
        log_d = jnp.where(causal, a_col - a_r + i_r, NEG_INF)
        log_inter = a_col + m_in
        m_t = jnp.maximum(jnp.max(log_d, axis=1, keepdims=True), log_inter)
        w = _dot_nt(_mask_head(q, h), k) * jnp.exp(log_d - m_t)
        inter = jnp.exp(log_inter - m_t)
        num = num + _dot(w.astype(BF16), _mask_head(v, h))
        den = jnp.sum(w, axis=1, keepdims=True) + inter * n_q[:, h:h + 1]
        inter_cols.append(inter)
        den_cols.append(jnp.maximum(jnp.abs(den), jnp.exp(-m_t)))
        g_col = a_end - a_col + i_col
        m_new = jnp.maximum(a_end + m_in, jnp.max(g_col, axis=0, keepdims=True))
        ws_cols.append(jnp.exp(g_col - m_new))
        scale_cols.append(jnp.exp(a_end + m_in - m_new))
        m_next = jnp.where(lane == h, m_new, m_next)
    hout = (num + _expand_heads(inter_cols) * inter_q) / _expand_heads(den_cols)

    kw_t = (k.astype(F32) * _expand_heads(ws_cols)).T.astype(BF16)
    scale_row = _expand_heads(scale_cols)
    same_head = _head_of_lane((W, W), 0) == _head_of_lane((W, W), 1)
    ct[...] = scale_row * ct[...] + jnp.where(same_head, _dot(kw_t, v), 0.0)
    col_is_head = _head_of_lane((W, LANES), 0) == lax.broadcasted_iota(I32, (W, LANES), 1)
    scale_n = scale_cols[N_HEADS - 1]
    for h in range(N_HEADS - 2, -1, -1):
        scale_n = jnp.where(lane == h, scale_cols[h], scale_n)
    nb[...] = scale_n * nb[...] + jnp.where(col_is_head, _dot(kw_t, jnp.ones((L, LANES), BF16)), 0.0)
    m_row[...] = m_next
    return hout


def _mlstm_mixer(proj, gates, bias_row, gain, bsz, seq, tile=MIX_TILE):
    t = proj.shape[0]
    nc = seq // tile
    row = lambda blk: pl.BlockSpec((tile, GROUP_WIDTH), lambda b, j, blk=blk: (b * nc + j, blk))
    const = lambda shape: pl.BlockSpec(shape, lambda b, j: (0,) * len(shape))
    return pl.pallas_call(
        _mlstm_kernel,
        grid=(bsz, nc),
        in_specs=[row(PB_LQ), row(PB_LK), row(PB_LV), row(PB_LO),
                  pl.BlockSpec((tile, LANES), lambda b, j: (b * nc + j, 0)),
                  const((1, LANES)), const((1, GROUP_WIDTH))],
        out_specs=pl.BlockSpec((tile, GROUP_WIDTH), lambda b, j: (b * nc + j, 0)),
        out_shape=jax.ShapeDtypeStruct((t, GROUP_WIDTH), BF16),
        scratch_shapes=[pltpu.VMEM((GROUP_WIDTH, GROUP_WIDTH), F32),
                        pltpu.VMEM((GROUP_WIDTH, LANES), F32),
                        pltpu.VMEM((1, LANES), F32)],
        compiler_params=_cparams("arbitrary", "arbitrary"),
        name="mlstm_mixer",
    )(proj, proj, proj, proj, gates, bias_row, gain)


def _outproj_router_kernel(x_ref, ya_ref, ys_ref, ym_ref, yl_ref, wo_ref, mod_ref, wn_ref, wr_ref, br_ref,
                           xo_ref, h2_ref, eid_ref, gcol_ref):
    W = GROUP_WIDTH
    tm = x_ref.shape[0]
    mod = mod_ref[0]
    out = _dot(ya_ref[...], wo_ref[0:W, :])
    out = out + _dot(ys_ref[...], wo_ref[W:2 * W, :])
    out = out + _dot(ym_ref[...], wo_ref[2 * W:3 * W, :])
    out = out + _dot(yl_ref[...], wo_ref[3 * W:4 * W, :])
    x1 = x_ref[...] + mod[2:3, :] * out
    xo_ref[...] = x1
    h2 = _rms_norm(x1, wn_ref[...]) * (1.0 + mod[4:5, :]) + mod[3:4, :]
    h2_ref[...] = h2

    logits_t = (_dot(h2.astype(BF16), wr_ref[...]) + br_ref[...]).T
    row8 = lax.broadcasted_iota(I32, (SUBLANES, tm), 0).astype(F32)
    gl = jnp.where(row8 < N_EXPERT_GROUPS, logits_t[ROUTE_G:ROUTE_G + SUBLANES, :], NEG_INF)
    g_max = jnp.max(gl, axis=0, keepdims=True)
    g_sel = jnp.min(jnp.where(gl == g_max, row8, SUBLANES), axis=0, keepdims=True)
    g_prob = 1.0 / jnp.sum(jnp.exp(gl - g_max), axis=0, keepdims=True)
    el = logits_t[ROUTE_E:ROUTE_E + EXPERTS_PER_GROUP, :]
    for g in range(1, N_EXPERT_GROUPS):
        lo = ROUTE_E + g * EXPERTS_PER_GROUP
        el = jnp.where(g_sel == g, logits_t[lo:lo + EXPERTS_PER_GROUP, :], el)
    m1 = jnp.max(el, axis=0, keepdims=True)
    i1 = jnp.min(jnp.where(el == m1, row8, SUBLANES), axis=0, keepdims=True)
    el2 = jnp.where(row8 == i1, NEG_INF, el)
    m2 = jnp.max(el2, axis=0, keepdims=True)
    i2 = jnp.min(jnp.where(el2 == m2, row8, SUBLANES), axis=0, keepdims=True)
    ratio = jnp.exp(m2 - m1)
    p1 = 1.0 / (1.0 + ratio)
    eid_ref[0:1, :] = (g_sel * EXPERTS_PER_GROUP + i1).astype(I32)
    eid_ref[1:2, :] = (g_sel * EXPERTS_PER_GROUP + i2).astype(I32)
    rows = lax.broadcasted_iota(I32, (LANES, tm), 0)
    gate_rows = jnp.where(rows == 0, g_prob * p1, jnp.where(rows == 1, g_prob * p1 * ratio, 0.0))
    gcol_ref[...] = gate_rows.T


def _outproj_router(x2, ys, w_out, mod, w_norm2, w_route, b_route, seq, tm=256):
    t, d = x2.shape
    spb = seq // tm
    ytile = pl.BlockSpec((tm, GROUP_WIDTH), lambda i: (i, 0))
    return pl.pallas_call(
        _outproj_router_kernel,
        grid=(t // tm,),
        in_specs=[pl.BlockSpec((tm, d), lambda i: (i, 0)), ytile, ytile, ytile, ytile,
                  pl.BlockSpec((d, d), lambda i: (0, 0)),
                  pl.BlockSpec((1, 6, d), lambda i: (i // spb, 0, 0)),
                  pl.BlockSpec((1, d), lambda i: (0, 0)),
                  pl.BlockSpec((d, LANES), lambda i: (0, 0)),
                  pl.BlockSpec((1, LANES), lambda i: (0, 0))],
        out_specs=[pl.BlockSpec((tm, d), lambda i: (i, 0)),
                   pl.BlockSpec((tm, d), lambda i: (i, 0)),
                   pl.BlockSpec((TOP_K, tm), lambda i: (0, i)),
                   pl.BlockSpec((tm, LANES), lambda i: (i, 0))],
        out_shape=[jax.ShapeDtypeStruct((t, d), F32),
                   jax.ShapeDtypeStruct((t, d), F32),
                   jax.ShapeDtypeStruct((TOP_K, t), I32),
                   jax.ShapeDtypeStruct((t, LANES), F32)],
        compiler_params=_cparams("arbitrary"),
        name="outproj_router",
    )(x2, *ys, w_out, mod, w_norm2, w_route, b_route)


def _rank_kernel(eid_ref, rank_ref, count_ref, carry):
    @pl.when(pl.program_id(0) == 0)
    def _():
        carry[...] = jnp.zeros_like(carry)

    tr = eid_ref.shape[1]
    expert = lax.broadcasted_iota(I32, (N_EXPERTS, tr), 0)
    before = (lax.broadcasted_iota(I32, (tr, tr), 0) < lax.broadcasted_iota(I32, (tr, tr), 1)).astype(BF16)
    base = carry[...]
    for k in range(TOP_K):
        onehot = (expert == eid_ref[k:k + 1, :]).astype(F32)
        prefix = _dot(onehot.astype(BF16), before)
        rank_ref[k:k + 1, :] = jnp.sum(onehot * (base + prefix), axis=0, keepdims=True).astype(I32)
        base = base + jnp.sum(onehot, axis=1, keepdims=True)
    carry[...] = base
    count_ref[...] = jnp.broadcast_to(base, count_ref.shape)


def _expert_ranks(eids, tr=512):
    t = eids.shape[1]
    return pl.pallas_call(
        _rank_kernel,
        grid=(t // tr,),
        in_specs=[pl.BlockSpec((TOP_K, tr), lambda i: (0, i))],
        out_specs=[pl.BlockSpec((TOP_K, tr), lambda i: (0, i)),
                   pl.BlockSpec((N_EXPERTS, LANES), lambda i: (0, 0))],
        out_shape=[jax.ShapeDtypeStruct((TOP_K, t), I32),
                   jax.ShapeDtypeStruct((N_EXPERTS, LANES), F32)],
        scratch_shapes=[pltpu.VMEM((N_EXPERTS, 1), F32)],
        compiler_params=_cparams("arbitrary"),
        name="expert_ranks",
    )(eids)


def _dest_kernel(pstart_ref, eid_ref, rank_ref, dest_ref):
    e = eid_ref[...]
    dest = rank_ref[...]
    for j in range(N_EXPERTS):
        dest = dest + jnp.where(e == j, pstart_ref[j], 0)
    dest_ref[...] = dest


def _dest_rows(p_starts, eids, ranks, tm=2048):
    t = eids.shape[1]
    grid_spec = pltpu.PrefetchScalarGridSpec(
        num_scalar_prefetch=1,
        grid=(t // tm,),
        in_specs=[pl.BlockSpec((TOP_K, tm), lambda i, ps: (0, i)),
                  pl.BlockSpec((TOP_K, tm), lambda i, ps: (0, i))],
        out_specs=pl.BlockSpec((TOP_K, tm), lambda i, ps: (0, i)))
    return pl.pallas_call(
        _dest_kernel,
        grid_spec=grid_spec,
        out_shape=jax.ShapeDtypeStruct((TOP_K, t), I32),
        compiler_params=_cparams("arbitrary"),
        name="dest_rows",
    )(p_starts, eids, ranks)


ROW_TILE = IDX_CHUNK // TOP_K


def _load_tile_indices(idx_hbm, idx_smem, idx_sem):
    copy = pltpu.make_async_copy(idx_hbm.at[pl.ds(pl.program_id(0) * IDX_CHUNK, IDX_CHUNK)], idx_smem, idx_sem)
    copy.start()
    copy.wait()


def _dispatch_kernel(pend_ref, padded_ref, nu_ref, idx_hbm, h_ref, xb_hbm, idx_smem, zero_blk, idx_sem, row_sem,
                     zero_sem):
    def zero_block(start):
        return pltpu.make_async_copy(zero_blk, xb_hbm.at[pl.ds(pl.multiple_of(start, MOE_BLOCK), MOE_BLOCK)],
                                     zero_sem)

    @pl.when(pl.program_id(0) == 0)
    def _():
        zero_blk[...] = jnp.zeros_like(zero_blk)
        for e in range(N_EXPERTS):
            @pl.when(padded_ref[e] > 0)
            def _(e=e):
                zero_block(pend_ref[e] - MOE_BLOCK).start()
        for e in range(N_EXPERTS):
            @pl.when(padded_ref[e] > 0)
            def _(e=e):
                zero_block(pend_ref[e] - MOE_BLOCK).wait()

        def zero_unused(b, carry):
            copy = zero_block(b * MOE_BLOCK)
            copy.start()
            copy.wait()
            return carry

        lax.fori_loop(nu_ref[0], xb_hbm.shape[0] // MOE_BLOCK, zero_unused, 0)

    _load_tile_indices(idx_hbm, idx_smem, idx_sem)

    def row_copy(r, dst_row):
        return pltpu.make_async_copy(h_ref.at[pl.ds(r, 1)], xb_hbm.at[pl.ds(dst_row, 1)], row_sem)

    def issue(r, carry):
        for k in range(TOP_K):
            row_copy(r, idx_smem[k * ROW_TILE + r]).start()
        return carry

    lax.fori_loop(0, ROW_TILE, issue, 0, unroll=8)

    def drain(r, carry):
        for k in range(TOP_K):
            row_copy(0, 0).wait()
        return carry

    lax.fori_loop(0, ROW_TILE, drain, 0, unroll=8)


def _dispatch(p_ends, padded, n_used, idx_tiles, h2, dst_rows):
    t, d = h2.shape
    any_spec = pl.BlockSpec(memory_space=pl.ANY)
    grid_spec = pltpu.PrefetchScalarGridSpec(
        num_scalar_prefetch=3,
        grid=(t // ROW_TILE,),
        in_specs=[any_spec, pl.BlockSpec((ROW_TILE, d), lambda i, pe, pd, nu: (i, 0))],
        out_specs=any_spec,
        scratch_shapes=[pltpu.SMEM((IDX_CHUNK,), I32), pltpu.VMEM((MOE_BLOCK, d), h2.dtype),
                        pltpu.SemaphoreType.DMA, pltpu.SemaphoreType.DMA, pltpu.SemaphoreType.DMA])
    return pl.pallas_call(
        _dispatch_kernel,
        grid_spec=grid_spec,
        out_shape=jax.ShapeDtypeStruct((dst_rows, d), h2.dtype),
        compiler_params=_cparams("arbitrary"),
        name="moe_dispatch",
    )(p_ends, padded, n_used, idx_tiles, h2)


def _expert_kernel(be_ref, nu_ref, x_ref, wg_ref, wu_ref, wd_ref, y_ref, wg_b, wu_b, wd_b):
    b = pl.program_id(0)

    @pl.when(b < nu_ref[0])
    def _():
        @pl.when(jnp.logical_or(b == 0, be_ref[b] != be_ref[jnp.maximum(b - 1, 0)]))
        def _():
            wg_b[...] = wg_ref[0].astype(BF16)
            wu_b[...] = wu_ref[0].astype(BF16)
            wd_b[...] = wd_ref[0].astype(BF16)

        x = x_ref[...].astype(BF16)
        a = _silu(_dot(x, wg_b[...])) * _dot(x, wu_b[...])
        y_ref[...] = _dot(a.astype(BF16), wd_b[...])


def _expert_mlp(block_e, n_used, xb, w_gate, w_up, w_down):
    p, d = xb.shape
    de = w_gate.shape[2]
    blk = lambda b, be, nu: (jnp.minimum(b, nu[0] - 1), 0)
    grid_spec = pltpu.PrefetchScalarGridSpec(
        num_scalar_prefetch=2,
        grid=(p // MOE_BLOCK,),
        in_specs=[pl.BlockSpec((MOE_BLOCK, d), blk),
                  pl.BlockSpec((1, d, de), lambda b, be, nu: (be[b], 0, 0)),
                  pl.BlockSpec((1, d, de), lambda b, be, nu: (be[b], 0, 0)),
                  pl.BlockSpec((1, de, d), lambda b, be, nu: (be[b], 0, 0))],
        out_specs=pl.BlockSpec((MOE_BLOCK, d), blk),
        scratch_shapes=[pltpu.VMEM((d, de), BF16), pltpu.VMEM((d, de), BF16), pltpu.VMEM((de, d), BF16)])
    return pl.pallas_call(
        _expert_kernel,
        grid_spec=grid_spec,
        out_shape=jax.ShapeDtypeStruct((p, d), F32),
        compiler_params=_cparams("arbitrary"),
        name="expert_mlp",
    )(block_e, n_used, xb, w_gate, w_up, w_down)


def _combine_kernel(idx_hbm, yb_hbm, x_ref, gcol_ref, mod_ref, wnf_ref, o_ref, ybuf, idx_smem, idx_sem, row_sem,
                    *, final):
    _load_tile_indices(idx_hbm, idx_smem, idx_sem)

    def row_copy(src_row, k, r):
        return pltpu.make_async_copy(yb_hbm.at[pl.ds(src_row, 1)], ybuf.at[k, pl.ds(r, 1)], row_sem)

    def issue(r, carry):
        for k in range(TOP_K):
            row_copy(idx_smem[k * ROW_TILE + r], k, r).start()
        return carry

    lax.fori_loop(0, ROW_TILE, issue, 0, unroll=8)

    def drain(r, carry):
        for k in range(TOP_K):
            row_copy(0, k, 0).wait()
        return carry

    lax.fori_loop(0, ROW_TILE, drain, 0, unroll=8)

    gc = gcol_ref[...]
    moe = gc[:, 0:1] * ybuf[0] + gc[:, 1:2] * ybuf[1]
    x2 = x_ref[...] + mod_ref[0][5:6, :] * moe
    o_ref[...] = _rms_norm(x2, wnf_ref[...]) if final else x2


def _combine(idx_tiles, yb, x2, gcol, mod, w_norm_final, seq, final):
    t, d = x2.shape
    tm = ROW_TILE
    spb = seq // tm
    any_spec = pl.BlockSpec(memory_space=pl.ANY)
    return pl.pallas_call(
        functools.partial(_combine_kernel, final=final),
        grid=(t // tm,),
        in_specs=[any_spec, any_spec,
                  pl.BlockSpec((tm, d), lambda i: (i, 0)),
                  pl.BlockSpec((tm, LANES), lambda i: (i, 0)),
                  pl.BlockSpec((1, 6, d), lambda i: (i // spb, 0, 0)),
                  pl.BlockSpec((1, d), lambda i: (0, 0))],
        out_specs=pl.BlockSpec((tm, d), lambda i: (i, 0)),
        out_shape=jax.ShapeDtypeStruct((t, d), F32),
        scratch_shapes=[pltpu.VMEM((TOP_K, tm, d), F32), pltpu.SMEM((IDX_CHUNK,), I32),
                        pltpu.SemaphoreType.DMA, pltpu.SemaphoreType.DMA],
        compiler_params=_cparams("arbitrary"),
        name="moe_combine",
    )(idx_tiles, yb, x2, gcol, mod, w_norm_final)


def _lane_row(pieces, width=LANES):
    row = jnp.zeros((width,), F32)
    for off, vec in pieces.items():
        row = row.at[off:off + vec.shape[0]].set(vec.astype(F32))
    return row.reshape(1, width)


def _split_w_in(w_in):
    gw, nh = GROUP_WIDTH, N_HEADS
    widths = [gw, gw, gw, nh, gw, gw, gw, 3 * gw, nh, gw, gw, gw, gw, nh, nh]
    cuts, acc = [], 0
    for w in widths[:-1]:
        acc += w
        cuts.append(acc)
    (aq, ak, av, af, su, sv, mz, mxbc, mdt, lq, lk, lv, lo, li, lf) = jnp.split(w_in, cuts, axis=1)
    scale = HEAD_DIM ** -0.5
    wmain = jnp.concatenate([ak, su, sv, mz, mxbc, lq, lk * scale, lv, lo], axis=1).astype(BF16)
    wqt = (aq * scale).T.astype(BF16)
    wvt = av.T.astype(BF16)
    wg = jnp.zeros((w_in.shape[0], LANES), F32)
    for off, w in ((GATE_AF, af), (GATE_DT, mdt), (GATE_LI, li), (GATE_LF, lf)):
        wg = wg.at[:, off:off + nh].set(w)
    return wmain, wqt, wvt, wg.astype(BF16)


def _moe_layer(x1, h2, eids, gcol, mod_l, w_gate, w_up, w_down, layer, w_norm_final, seq, final):
    t, d = x1.shape
    ranks, counts = _expert_ranks(eids)
    counts = counts[:, 0].astype(I32)
    padded = ((counts + MOE_BLOCK - 1) // MOE_BLOCK) * MOE_BLOCK
    p_ends = jnp.cumsum(padded)
    p_starts = (p_ends - padded).astype(I32)
    n_blocks = (t * TOP_K) // MOE_BLOCK + N_EXPERTS
    blocks = jnp.arange(n_blocks, dtype=I32)
    block_e = jnp.sum((p_ends[None, :] <= (blocks * MOE_BLOCK)[:, None]).astype(I32), axis=1)
    block_e = jnp.minimum(block_e, N_EXPERTS - 1)
    n_used = (p_ends[-1:] // MOE_BLOCK).astype(I32)
    block_e = jnp.where(blocks < n_used, block_e, block_e[n_used[0] - 1])
    dest = _dest_rows(p_starts, eids, ranks)
    idx_tiles = dest.reshape(TOP_K, t // ROW_TILE, ROW_TILE).transpose(1, 0, 2).reshape(-1)
    xb = _dispatch(p_ends.astype(I32), padded.astype(I32), n_used, idx_tiles, h2, n_blocks * MOE_BLOCK)
    yb = _expert_mlp(block_e + layer * N_EXPERTS, n_used, xb, w_gate, w_up, w_down)
    return _combine(idx_tiles, yb, x1, gcol, mod_l, w_norm_final, seq, final)


def kernel(x, c, w_in, w_out, w_mix_norm, attn_f_bias, sg_w, sg_b, ssm_conv_w, ssm_conv_b, ssm_dt_bias,
           ssm_a_log, ssm_d, mlstm_i_bias, mlstm_f_bias, w_ada, b_ada, w_norm1, w_norm2, w_router_group,
           b_router_group, w_router_expert, b_router_expert, w_expert_gate, w_expert_up, w_expert_down,
           w_norm_final):
    bsz, seq, d = x.shape
    depth = w_in.shape[0]
    gw = GROUP_WIDTH
    mod = _ada_modulation(c, w_ada, b_ada).reshape(depth, bsz, 6, d)
    x2 = x.reshape(bsz * seq, d)
    wnf = w_norm_final.reshape(1, d)
    w_eg = w_expert_gate.reshape((depth * N_EXPERTS,) + w_expert_gate.shape[2:])
    w_eu = w_expert_up.reshape((depth * N_EXPERTS,) + w_expert_up.shape[2:])
    w_ed = w_expert_down.reshape((depth * N_EXPERTS,) + w_expert_down.shape[2:])
    for l in range(depth):
        wmain, wqt, wvt, wg = _split_w_in(w_in[l])
        gains = w_mix_norm[l].reshape(N_HEADS, 1, gw)
        proj, qt, vt, gates = _inproj(x2, mod[l], w_norm1[l].reshape(1, d), wmain, wqt, wvt, wg, bsz, seq)
        f2 = _forget_cumsum(gates, _lane_row({GATE_AF: attn_f_bias[l]}), bsz, seq)
        y_attn = _attention(proj, qt, vt, f2, gains[0], bsz, seq)
        sg_bias_full = jnp.repeat(sg_b[l].T, HEAD_DIM, axis=1)
        y_sg = _spatial_gating(proj, gains[1], sg_w[l], sg_bias_full)
        y_ssm = _ssd_mixer(proj, gates, ssm_conv_w[l], ssm_conv_b[l].reshape(1, -1),
                           _lane_row({GATE_DT: ssm_dt_bias[l]}), _lane_row({GATE_DT: ssm_a_log[l]}),
                           jnp.repeat(ssm_d[l], HEAD_DIM).reshape(1, gw), gains[2], bsz, seq)
        y_ml = _mlstm_mixer(proj, gates, _lane_row({GATE_LI: mlstm_i_bias[l], GATE_LF: mlstm_f_bias[l]}),
                            gains[3], bsz, seq)
        w_route = jnp.zeros((d, LANES), F32)
        w_route = w_route.at[:, ROUTE_G:ROUTE_G + N_EXPERT_GROUPS].set(w_router_group[l])
        w_route = w_route.at[:, ROUTE_E:ROUTE_E + N_EXPERTS].set(w_router_expert[l]).astype(BF16)
        b_route = _lane_row({ROUTE_G: b_router_group[l], ROUTE_E: b_router_expert[l]})
        x1, h2, eids, gcol = _outproj_router(x2, (y_attn, y_sg, y_ssm, y_ml), w_out[l].astype(BF16), mod[l],
                                             w_norm2[l].reshape(1, d), w_route, b_route, seq)
        x2 = _moe_layer(x1, h2, eids, gcol, mod[l], w_eg, w_eu, w_ed, l, wnf, seq, final=(l == depth - 1))
    return x2.reshape(bsz, seq, d)
```

```python
import functools

import jax
import jax.numpy as jnp
from jax import lax
from jax.experimental import pallas as pl
from jax.experimental.pallas import tpu as pltpu

F32 = jnp.float32
BF16 = jnp.bfloat16
I32 = jnp.int32

LANES = 128
SUBLANES = 8
HEAD_DIM = 64
N_HEADS = 4
GROUP_WIDTH = 256
CHUNK = 128
MIX_TILE = 4 * CHUNK
SSM_GROUPS = 2
CONV_WIDTH = 4
N_EXPERT_GROUPS = 4
EXPERTS_PER_GROUP = 8
N_EXPERTS = N_EXPERT_GROUPS * EXPERTS_PER_GROUP
TOP_K = 2
MOE_BLOCK = 256
NORM_EPS = 1e-6
MLSTM_M_INIT = -1e30
NEG_INF = float("-inf")
VMEM_LIMIT_BYTES = 48 * 1024 * 1024
IDX_CHUNK = 1024

(PB_K, PB_SU, PB_SV, PB_MZ, PB_X, PB_B, PB_C, PB_LQ, PB_LK, PB_LV, PB_LO) = range(11)
PROJ_COLS = 11 * GROUP_WIDTH
LOG2_E = 1.4426950408889634
GATE_AF, GATE_DT, GATE_LI, GATE_LF = 0, 4, 8, 12
ROUTE_G, ROUTE_E = 0, 8

NT_DIMS = (((1,), (1,)), ((), ()))


def _cparams(*sem):
    return pltpu.CompilerParams(dimension_semantics=sem, vmem_limit_bytes=VMEM_LIMIT_BYTES)


def _dot(a, b):
    return jnp.dot(a, b, preferred_element_type=F32)


def _dot_nt(a, b):
    return lax.dot_general(a, b, NT_DIMS, preferred_element_type=F32)


def _dot_exact(a, b):
    return jnp.dot(a, b, preferred_element_type=F32, precision=lax.Precision.HIGHEST)


def _head_of_lane(shape, axis=1):
    return lax.broadcasted_iota(I32, shape, axis) // HEAD_DIM


def _sigmoid(x):
    return 1.0 / (1.0 + jnp.exp(-x))


def _silu(x):
    return x * _sigmoid(x)


def _log_sigmoid(x):
    return jnp.minimum(x, 0.0) - jnp.log1p(jnp.exp(-jnp.abs(x)))


def _softplus(x):
    return jnp.maximum(x, 0.0) + jnp.log1p(jnp.exp(-jnp.abs(x)))


def _gelu_tanh(x):
    return 0.5 * x * (1.0 + jnp.tanh(0.7978845608028654 * (x + 0.044715 * (x * x * x))))


def _expand_heads(cols, width=GROUP_WIDTH):
    rows = cols[0].shape[0]
    head = _head_of_lane((rows, width))
    out = jnp.broadcast_to(cols[N_HEADS - 1], (rows, width))
    for h in range(N_HEADS - 2, -1, -1):
        out = jnp.where(head == h, jnp.broadcast_to(cols[h], (rows, width)), out)
    return out


def _mask_head(x, h):
    return jnp.where(_head_of_lane(x.shape) == h, x, jnp.zeros_like(x))


def _head_rms_norm(y, gain):
    head = _head_of_lane(y.shape)
    sq = y * y
    cols = [jnp.sum(jnp.where(head == h, sq, 0.0), axis=1, keepdims=True) * (1.0 / HEAD_DIM)
            for h in range(N_HEADS)]
    return y * lax.rsqrt(_expand_heads(cols) + NORM_EPS) * gain


def _rms_norm(x, gain):
    ms = jnp.mean(x * x, axis=1, keepdims=True)
    return x * lax.rsqrt(ms + NORM_EPS) * gain


def _tril(n, strict=False):
    r = lax.broadcasted_iota(I32, (n, n), 0)
    c = lax.broadcasted_iota(I32, (n, n), 1)
    return (r > c) if strict else (r >= c)


def _ada_kernel(c_ref, w_ref, b_ref, o_ref):
    o_ref[0] = _dot_exact(_silu(c_ref[...]), w_ref[0]) + b_ref[0]


def _ada_modulation(c, w_ada, b_ada):
    depth, d, d6 = w_ada.shape
    bsz = c.shape[0]
    return pl.pallas_call(
        _ada_kernel,
        grid=(depth, d6 // d),
        in_specs=[pl.BlockSpec((bsz, d), lambda l, j: (0, 0)),
                  pl.BlockSpec((1, d, d), lambda l, j: (l, 0, j)),
                  pl.BlockSpec((1, 1, d), lambda l, j: (l, 0, j))],
        out_specs=pl.BlockSpec((1, bsz, d), lambda l, j: (l, 0, j)),
        out_shape=jax.ShapeDtypeStruct((depth, bsz, d6), F32),
        compiler_params=_cparams("arbitrary", "arbitrary"),
        name="ada_modulation",
    )(c, w_ada, b_ada.reshape(depth, 1, d6))


def _inproj_kernel(x_ref, mod_ref, wn_ref, wmain_ref, wqt_ref, wvt_ref, wg_ref,
                   proj_ref, qt_ref, vt_ref, gates_ref, h_scr):
    mod = mod_ref[0]
    h = _rms_norm(x_ref[...], wn_ref[...]) * (1.0 + mod[1:2, :]) + mod[0:1, :]
    h_scr[...] = h.astype(BF16)
    for c0 in range(0, PROJ_COLS, GROUP_WIDTH):
        proj_ref[:, c0:c0 + GROUP_WIDTH] = _dot(h_scr[...], wmain_ref[:, c0:c0 + GROUP_WIDTH]).astype(BF16)
    qt_ref[0] = (_dot_nt(wqt_ref[...], h_scr[...]) * LOG2_E).astype(BF16)
    vt_ref[0] = _dot_nt(wvt_ref[...], h_scr[...]).astype(BF16)
    gates_ref[...] = _dot(h_scr[...], wg_ref[...])


def _inproj(x2, mod, w_norm, wmain, wqt, wvt, wg, bsz, seq, tm=512):
    t, d = x2.shape
    spb = seq // tm
    tspec = pl.BlockSpec((1, GROUP_WIDTH, tm), lambda i: (i // spb, 0, i % spb))
    return pl.pallas_call(
        _inproj_kernel,
        grid=(t // tm,),
        in_specs=[pl.BlockSpec((tm, d), lambda i: (i, 0)),
                  pl.BlockSpec((1, 6, d), lambda i: (i // spb, 0, 0)),
                  pl.BlockSpec((1, d), lambda i: (0, 0)),
                  pl.BlockSpec((d, PROJ_COLS), lambda i: (0, 0)),
                  pl.BlockSpec((GROUP_WIDTH, d), lambda i: (0, 0)),
                  pl.BlockSpec((GROUP_WIDTH, d), lambda i: (0, 0)),
                  pl.BlockSpec((d, LANES), lambda i: (0, 0))],
        out_specs=[pl.BlockSpec((tm, PROJ_COLS), lambda i: (i, 0)), tspec, tspec,
                   pl.BlockSpec((tm, LANES), lambda i: (i, 0))],
        out_shape=[jax.ShapeDtypeStruct((t, PROJ_COLS), BF16),
                   jax.ShapeDtypeStruct((bsz, GROUP_WIDTH, seq), BF16),
                   jax.ShapeDtypeStruct((bsz, GROUP_WIDTH, seq), BF16),
                   jax.ShapeDtypeStruct((t, LANES), F32)],
        scratch_shapes=[pltpu.VMEM((tm, d), BF16)],
        compiler_params=_cparams("arbitrary"),
        name="norm_inproj",
    )(x2, mod, w_norm, wmain, wqt, wvt, wg)


def _fcum_kernel(g_ref, b_ref, f_ref, carry):
    @pl.when(pl.program_id(1) == 0)
    def _():
        carry[...] = jnp.zeros_like(carry)

    tb = g_ref.shape[0]
    ls = _log_sigmoid(g_ref[...] + b_ref[...])
    cum = _dot_exact(_tril(tb).astype(F32), ls) + carry[...]
    carry[...] = cum[tb - 1:tb, :]
    f_ref[...] = cum * LOG2_E


def _forget_cumsum(gates, f_bias_row, bsz, seq, tb=512):
    spb = seq // tb
    return pl.pallas_call(
        _fcum_kernel,
        grid=(bsz, spb),
        in_specs=[pl.BlockSpec((tb, LANES), lambda b, j: (b * spb + j, 0)),
                  pl.BlockSpec((1, LANES), lambda b, j: (0, 0))],
        out_specs=pl.BlockSpec((tb, LANES), lambda b, j: (b * spb + j, 0)),
        out_shape=jax.ShapeDtypeStruct((bsz * seq, LANES), F32),
        scratch_shapes=[pltpu.VMEM((1, LANES), F32)],
        compiler_params=_cparams("arbitrary", "arbitrary"),
        name="forget_cumsum",
    )(gates, f_bias_row)


def _attn_kernel(qi_ref, kj_ref, k_ref, qt_ref, vt_ref, f_ref, gain_ref, o_ref, qtm, m_s, l_s, acc):
    p = pl.program_id(1)
    qi, kj = qi_ref[p], kj_ref[p]
    tk = k_ref.shape[0]
    tq = qt_ref.shape[2]

    @pl.when(kj == 0)
    def _():
        qt = qt_ref[0]
        head = _head_of_lane(qt.shape, 0)
        for h in range(N_HEADS):
            qtm[h] = jnp.where(head == h, qt, jnp.zeros_like(qt))
        m_s[...] = jnp.full_like(m_s, NEG_INF)
        l_s[...] = jnp.zeros_like(l_s)
        acc[...] = jnp.zeros_like(acc)

    def step(diagonal):
        k = k_ref[...]
        f = f_ref[...]
        if diagonal:
            visible = lax.broadcasted_iota(I32, (tk, tq), 0) <= lax.broadcasted_iota(I32, (tk, tq), 1)
        scores, m_news, alphas, probs = [], [], [], []
        for h in range(N_HEADS):
            s = _dot(k, qtm[h]) - f[:, GATE_AF + h:GATE_AF + h + 1]
            if diagonal:
                s = jnp.where(visible, s, NEG_INF)
            scores.append(s)
            m_prev = m_s[h]
            m_new = jnp.maximum(m_prev, jnp.max(s, axis=0, keepdims=True))
            m_news.append(m_new)
            alphas.append(jnp.exp2(m_prev - m_new))
            m_s[h] = m_new
        for h in range(N_HEADS):
            pr = jnp.exp2(scores[h] - m_news[h])
            l_s[h] = alphas[h] * l_s[h] + jnp.sum(pr, axis=0, keepdims=True)
            probs.append(pr.astype(BF16))
        for h in range(N_HEADS):
            pv = _dot(vt_ref[0, h * HEAD_DIM:(h + 1) * HEAD_DIM, :], probs[h])
            acc[h] = alphas[h] * acc[h] + pv

    @pl.when(kj < qi)
    def _():
        step(False)

    @pl.when(kj == qi)
    def _():
        step(True)
        yt = jnp.concatenate([acc[h] / l_s[h] for h in range(N_HEADS)], axis=0)
        o_ref[...] = _head_rms_norm(yt.T, gain_ref[...]).astype(BF16)


def _attention(proj, qt, vt, f2, gain, bsz, seq, tq=1024):
    nq = seq // tq
    pairs = [(i, j) for i in range(nq) for j in range(i + 1)]
    qi = jnp.asarray([a for a, _ in pairs], I32)
    kj = jnp.asarray([b for _, b in pairs], I32)
    t = bsz * seq
    grid_spec = pltpu.PrefetchScalarGridSpec(
        num_scalar_prefetch=2,
        grid=(bsz, len(pairs)),
        in_specs=[pl.BlockSpec((tq, GROUP_WIDTH), lambda b, p, qi, kj: (b * nq + kj[p], PB_K)),
                  pl.BlockSpec((1, GROUP_WIDTH, tq), lambda b, p, qi, kj: (b, 0, qi[p])),
                  pl.BlockSpec((1, GROUP_WIDTH, tq), lambda b, p, qi, kj: (b, 0, kj[p])),
                  pl.BlockSpec((tq, LANES), lambda b, p, qi, kj: (b * nq + kj[p], 0)),
                  pl.BlockSpec((1, GROUP_WIDTH), lambda b, p, qi, kj: (0, 0))],
        out_specs=pl.BlockSpec((tq, GROUP_WIDTH), lambda b, p, qi, kj: (b * nq + qi[p], 0)),
        scratch_shapes=[pltpu.VMEM((N_HEADS, GROUP_WIDTH, tq), BF16),
                        pltpu.VMEM((N_HEADS, 1, tq), F32),
                        pltpu.VMEM((N_HEADS, 1, tq), F32),
                        pltpu.VMEM((N_HEADS, HEAD_DIM, tq), F32)])
    return pl.pallas_call(
        _attn_kernel,
        grid_spec=grid_spec,
        out_shape=jax.ShapeDtypeStruct((t, GROUP_WIDTH), BF16),
        compiler_params=_cparams("arbitrary", "arbitrary"),
        name="fox_attention",
    )(qi, kj, proj, qt, vt, f2, gain)


def _sg_kernel(u_ref, v_ref, gain_ref, w_ref, b_ref, o_ref):
    tm = u_ref.shape[0]
    u = _gelu_tanh(u_ref[...].astype(F32))
    v = _head_rms_norm(_gelu_tanh(v_ref[...].astype(F32)), gain_ref[...])
    causal = _tril(CHUNK)
    ws = [jnp.where(causal, w_ref[h], 0.0).astype(BF16) for h in range(N_HEADS)]
    for c0 in range(0, tm, CHUNK):
        vc = v[c0:c0 + CHUNK, :].astype(BF16)
        mixed = b_ref[...]
        for h in range(N_HEADS):
            mixed = mixed + _dot(ws[h], _mask_head(vc, h))
        o_ref[c0:c0 + CHUNK, :] = (u[c0:c0 + CHUNK, :] * mixed).astype(BF16)


def _spatial_gating(proj, gain, sg_w, sg_bias_full, tm=512):
    t = proj.shape[0]
    return pl.pallas_call(
        _sg_kernel,
        grid=(t // tm,),
        in_specs=[pl.BlockSpec((tm, GROUP_WIDTH), lambda i: (i, PB_SU)),
                  pl.BlockSpec((tm, GROUP_WIDTH), lambda i: (i, PB_SV)),
                  pl.BlockSpec((1, GROUP_WIDTH), lambda i: (0, 0)),
                  pl.BlockSpec((N_HEADS, CHUNK, CHUNK), lambda i: (0, 0, 0)),
                  pl.BlockSpec((CHUNK, GROUP_WIDTH), lambda i: (0, 0))],
        out_specs=pl.BlockSpec((tm, GROUP_WIDTH), lambda i: (i, 0)),
        out_shape=jax.ShapeDtypeStruct((t, GROUP_WIDTH), BF16),
        compiler_params=_cparams("arbitrary"),
        name="spatial_gating",
    )(proj, proj, gain, sg_w, sg_bias_full)


def _ssd_kernel(z_ref, x_ref, b_ref, c_ref, g_ref, cw_ref, cb_ref, dtb_ref, alog_ref, dskip_ref, gain_ref,
                o_ref, conv_scr, xbc_scr, state):
    L = CHUNK
    W = GROUP_WIDTH
    tile = x_ref.shape[0]

    @pl.when(pl.program_id(1) == 0)
    def _():
        conv_scr[0:SUBLANES, :] = jnp.zeros((SUBLANES, 3 * W), F32)
        state[...] = jnp.zeros_like(state)

    conv_scr[SUBLANES:, 0:W] = x_ref[...].astype(F32)
    conv_scr[SUBLANES:, W:2 * W] = b_ref[...].astype(F32)
    conv_scr[SUBLANES:, 2 * W:] = c_ref[...].astype(F32)
    cw = cw_ref[...]
    conv = cb_ref[...] + cw[CONV_WIDTH - 1:CONV_WIDTH, :] * conv_scr[SUBLANES:, :]
    for s in range(1, CONV_WIDTH):
        conv = conv + cw[CONV_WIDTH - 1 - s:CONV_WIDTH - s, :] * conv_scr[SUBLANES - s:SUBLANES - s + tile, :]
    conv_scr[0:SUBLANES, :] = conv_scr[tile:tile + SUBLANES, :]
    xbc_scr[...] = _silu(conv)

    neg_a = -jnp.exp(alog_ref[...])
    for c0 in range(0, tile, L):
        rows = slice(c0, c0 + L)
        dt = _softplus(g_ref[rows, :] + dtb_ref[...])
        y = _ssd_chunk(xbc_scr[rows, 0:W], xbc_scr[rows, W:2 * W], xbc_scr[rows, 2 * W:], dt, dt * neg_a,
                       dskip_ref[...], state)
        y = y * _silu(z_ref[rows, :].astype(F32))
        o_ref[rows, :] = _head_rms_norm(y, gain_ref[...]).astype(BF16)


def _ssd_chunk(xs, bm, cm, dt, da, dskip, state):
    L = CHUNK
    a_cum = _dot_exact(_tril(L).astype(F32), da)
    a_row = a_cum.T
    dt_cols = [dt[:, GATE_DT + h:GATE_DT + h + 1] for h in range(N_HEADS)]
    a_cols = [a_cum[:, GATE_DT + h:GATE_DT + h + 1] for h in range(N_HEADS)]
    a_end = [a_cum[L - 1:L, GATE_DT + h:GATE_DT + h + 1] for h in range(N_HEADS)]
    xdt = xs * _expand_heads(dt_cols)
    xdt_b = xdt.astype(BF16)
    xw = (xdt * _expand_heads([jnp.exp(a_end[h] - a_cols[h]) for h in range(N_HEADS)])).astype(BF16)
    exp_a = _expand_heads([jnp.exp(a_cols[h]) for h in range(N_HEADS)])
    causal = _tril(L)
    half = lax.broadcasted_iota(I32, (1, LANES), 1) < HEAD_DIM

    y = dskip * xs
    y_off = []
    for g in range(SSM_GROUPS):
        bg = bm[:, g * LANES:(g + 1) * LANES]
        cg = cm[:, g * LANES:(g + 1) * LANES].astype(BF16)
        scores = _dot_nt(cg, bg.astype(BF16))
        for r in range(N_HEADS // SSM_GROUPS):
            h = g * (N_HEADS // SSM_GROUPS) + r
            seg = a_cols[h] - a_row[GATE_DT + h:GATE_DT + h + 1, :]
            decay = jnp.exp(jnp.where(causal, seg, NEG_INF))
            y = y + _dot((scores * decay).astype(BF16), _mask_head(xdt_b, h))
        st_in = state[g]
        y_off.append(_dot(cg, st_in.astype(BF16)))
        chunk_decay = jnp.where(half, jnp.exp(a_end[2 * g]), jnp.exp(a_end[2 * g + 1]))
        state[g] = chunk_decay * st_in + _dot(bg.T.astype(BF16), xw[:, g * LANES:(g + 1) * LANES])
    return y + jnp.concatenate(y_off, axis=1) * exp_a


def _ssd_mixer(proj, gates, conv_w, conv_b, dtb_row, alog_row, dskip_row, gain, bsz, seq, tile=MIX_TILE):
    t = proj.shape[0]
    nc = seq // tile
    row = lambda blk: pl.BlockSpec((tile, GROUP_WIDTH), lambda b, j, blk=blk: (b * nc + j, blk))
    const = lambda shape: pl.BlockSpec(shape, lambda b, j: (0,) * len(shape))
    return pl.pallas_call(
        _ssd_kernel,
        grid=(bsz, nc),
        in_specs=[row(PB_MZ), row(PB_X), row(PB_B), row(PB_C),
                  pl.BlockSpec((tile, LANES), lambda b, j: (b * nc + j, 0)),
                  const((CONV_WIDTH, 3 * GROUP_WIDTH)), const((1, 3 * GROUP_WIDTH)),
                  const((1, LANES)), const((1, LANES)), const((1, GROUP_WIDTH)), const((1, GROUP_WIDTH))],
        out_specs=pl.BlockSpec((tile, GROUP_WIDTH), lambda b, j: (b * nc + j, 0)),
        out_shape=jax.ShapeDtypeStruct((t, GROUP_WIDTH), BF16),
        scratch_shapes=[pltpu.VMEM((tile + SUBLANES, 3 * GROUP_WIDTH), F32),
                        pltpu.VMEM((tile, 3 * GROUP_WIDTH), F32),
                        pltpu.VMEM((SSM_GROUPS, LANES, LANES), F32)],
        compiler_params=_cparams("arbitrary", "arbitrary"),
        name="ssd_mixer",
    )(proj, proj, proj, proj, gates, conv_w, conv_b, dtb_row, alog_row, dskip_row, gain)


def _mlstm_kernel(q_ref, k_ref, v_ref, o_gate_ref, g_ref, bias_ref, gain_ref, o_ref, ct, nb, m_row):
    L = CHUNK
    W = GROUP_WIDTH

    @pl.when(pl.program_id(1) == 0)
    def _():
        ct[...] = jnp.zeros_like(ct)
        nb[...] = jnp.zeros_like(nb)
        m_row[...] = jnp.full_like(m_row, MLSTM_M_INIT)

    def chunk(c, carry):
        rows = pl.ds(pl.multiple_of(c * L, L), L)
        y = _mlstm_chunk(q_ref[rows, :], k_ref[rows, :], v_ref[rows, :], g_ref[rows, :] + bias_ref[...],
                         ct, nb, m_row)
        y = _sigmoid(o_gate_ref[rows, :].astype(F32)) * _head_rms_norm(y, gain_ref[...])
        o_ref[rows, :] = y.astype(BF16)
        return carry

    lax.fori_loop(0, q_ref.shape[0] // L, chunk, 0)


def _mlstm_chunk(q, k, v, gate, ct, nb, m_row):
    L = CHUNK
    W = GROUP_WIDTH
    a_full = _dot_exact(_tril(L).astype(F32), _log_sigmoid(gate))
    a_rows = a_full.T
    g_rows = gate.T
    causal = _tril(L)
    lane = lax.broadcasted_iota(I32, (1, LANES), 1)

    inter_q = _dot(q, ct[...].astype(BF16))
    n_q = _dot(q, nb[...].astype(BF16))
    m_old = m_row[...]
    num = jnp.zeros((L, W), F32)
    inter_cols, den_cols, ws_cols, scale_cols = [], [], [], []
    m_next = m_old
    for h in range(N_HEADS):
        a_col = a_full[:, GATE_LF + h:GATE_LF + h + 1]
        i_col = gate[:, GATE_LI + h:GATE_LI + h + 1]
        a_r = a_rows[GATE_LF + h:GATE_LF + h + 1, :]
        i_r = g_rows[GATE_LI + h:GATE_LI + h + 1, :]
        a_end = a_full[L - 1:L, GATE_LF + h:GATE_LF + h + 1]
        m_in = m_old[:, h:h + 1]
        log_d = jnp.where(causal, a_col - a_r + i_r, NEG_INF)
        log_inter = a_col + m_in
        m_t = jnp.maximum(jnp.max(log_d, axis=1, keepdims=True), log_inter)
        w = _dot_nt(_mask_head(q, h), k) * jnp.exp(log_d - m_t)
        inter = jnp.exp(log_inter - m_t)
        num = num + _dot(w.astype(BF16), _mask_head(v, h))
        den = jnp.sum(w, axis=1, keepdims=True) + inter * n_q[:, h:h + 1]
        inter_cols.append(inter)
        den_cols.append(jnp.maximum(jnp.abs(den), jnp.exp(-m_t)))
        g_col = a_end - a_col + i_col
        m_new = jnp.maximum(a_end + m_in, jnp.max(g_col, axis=0, keepdims=True))
        ws_cols.append(jnp.exp(g_col - m_new))
        scale_cols.append(jnp.exp(a_end + m_in - m_new))
        m_next = jnp.where(lane == h, m_new, m_next)
    hout = (num + _expand_heads(inter_cols) * inter_q) / _expand_heads(den_cols)

    kw_t = (k.astype(F32) * _expand_heads(ws_cols)).T.astype(BF16)
    scale_row = _expand_heads(scale_cols)
    same_head = _head_of_lane((W, W), 0) == _head_of_lane((W, W), 1)
    ct[...] = scale_row * ct[...] + jnp.where(same_head, _dot(kw_t, v), 0.0)
    col_is_head = _head_of_lane((W, LANES), 0) == lax.broadcasted_iota(I32, (W, LANES), 1)
    scale_n = scale_cols[N_HEADS - 1]
    for h in range(N_HEADS - 2, -1, -1):
        scale_n = jnp.where(lane == h, scale_cols[h], scale_n)
    nb[...] = scale_n * nb[...] + jnp.where(col_is_head, _dot(kw_t, jnp.ones((L, LANES), BF16)), 0.0)
    m_row[...] = m_next
    return hout


def _mlstm_mixer(proj, gates, bias_row, gain, bsz, seq, tile=MIX_TILE):
    t = proj.shape[0]
    nc = seq // tile
    row = lambda blk: pl.BlockSpec((tile, GROUP_WIDTH), lambda b, j, blk=blk: (b * nc + j, blk))
    const = lambda shape: pl.BlockSpec(shape, lambda b, j: (0,) * len(shape))
    return pl.pallas_call(
        _mlstm_kernel,
        grid=(bsz, nc),
        in_specs=[row(PB_LQ), row(PB_LK), row(PB_LV), row(PB_LO),
                  pl.BlockSpec((tile, LANES), lambda b, j: (b * nc + j, 0)),
                  const((1, LANES)), const((1, GROUP_WIDTH))],
        out_specs=pl.BlockSpec((tile, GROUP_WIDTH), lambda b, j: (b * nc + j, 0)),
        out_shape=jax.ShapeDtypeStruct((t, GROUP_WIDTH), BF16),
        scratch_shapes=[pltpu.VMEM((GROUP_WIDTH, GROUP_WIDTH), F32),
                        pltpu.VMEM((GROUP_WIDTH, LANES), F32),
                        pltpu.VMEM((1, LANES), F32)],
        compiler_params=_cparams("arbitrary", "arbitrary"),
        name="mlstm_mixer",
    )(proj, proj, proj, proj, gates, bias_row, gain)


def _outproj_router_kernel(x_ref, ya_ref, ys_ref, ym_ref, yl_ref, wo_ref, mod_ref, wn_ref, wr_ref, br_ref,
                           xo_ref, h2_ref, eid_ref, gcol_ref):
    W = GROUP_WIDTH
    tm = x_ref.shape[0]
    mod = mod_ref[0]
    out = _dot(ya_ref[...], wo_ref[0:W, :])
    out = out + _dot(ys_ref[...], wo_ref[W:2 * W, :])
    out = out + _dot(ym_ref[...], wo_ref[2 * W:3 * W, :])
    out = out + _dot(yl_ref[...], wo_ref[3 * W:4 * W, :])
    x1 = x_ref[...] + mod[2:3, :] * out
    xo_ref[...] = x1
    h2 = _rms_norm(x1, wn_ref[...]) * (1.0 + mod[4:5, :]) + mod[3:4, :]
    h2_ref[...] = h2

    logits_t = (_dot(h2.astype(BF16), wr_ref[...]) + br_ref[...]).T
    row8 = lax.broadcasted_iota(I32, (SUBLANES, tm), 0).astype(F32)
    gl = jnp.where(row8 < N_EXPERT_GROUPS, logits_t[ROUTE_G:ROUTE_G + SUBLANES, :], NEG_INF)
    g_max = jnp.max(gl, axis=0, keepdims=True)
    g_sel = jnp.min(jnp.where(gl == g_max, row8, SUBLANES), axis=0, keepdims=True)
    g_prob = 1.0 / jnp.sum(jnp.exp(gl - g_max), axis=0, keepdims=True)
    el = logits_t[ROUTE_E:ROUTE_E + EXPERTS_PER_GROUP, :]
    for g in range(1, N_EXPERT_GROUPS):
        lo = ROUTE_E + g * EXPERTS_PER_GROUP
        el = jnp.where(g_sel == g, logits_t[lo:lo + EXPERTS_PER_GROUP, :], el)
    m1 = jnp.max(el, axis=0, keepdims=True)
    i1 = jnp.min(jnp.where(el == m1, row8, SUBLANES), axis=0, keepdims=True)
    el2 = jnp.where(row8 == i1, NEG_INF, el)
    m2 = jnp.max(el2, axis=0, keepdims=True)
    i2 = jnp.min(jnp.where(el2 == m2, row8, SUBLANES), axis=0, keepdims=True)
    ratio = jnp.exp(m2 - m1)
    p1 = 1.0 / (1.0 + ratio)
    eid_ref[0:1, :] = (g_sel * EXPERTS_PER_GROUP + i1).astype(I32)
    eid_ref[1:2, :] = (g_sel * EXPERTS_PER_GROUP + i2).astype(I32)
    rows = lax.broadcasted_iota(I32, (LANES, tm), 0)
    gate_rows = jnp.where(rows == 0, g_prob * p1, jnp.where(rows == 1, g_prob * p1 * ratio, 0.0))
    gcol_ref[...] = gate_rows.T


def _outproj_router(x2, ys, w_out, mod, w_norm2, w_route, b_route, seq, tm=256):
    t, d = x2.shape
    spb = seq // tm
    ytile = pl.BlockSpec((tm, GROUP_WIDTH), lambda i: (i, 0))
    return pl.pallas_call(
        _outproj_router_kernel,
        grid=(t // tm,),
        in_specs=[pl.BlockSpec((tm, d), lambda i: (i, 0)), ytile, ytile, ytile, ytile,
                  pl.BlockSpec((d, d), lambda i: (0, 0)),
                  pl.BlockSpec((1, 6, d), lambda i: (i // spb, 0, 0)),
                  pl.BlockSpec((1, d), lambda i: (0, 0)),
                  pl.BlockSpec((d, LANES), lambda i: (0, 0)),
                  pl.BlockSpec((1, LANES), lambda i: (0, 0))],
        out_specs=[pl.BlockSpec((tm, d), lambda i: (i, 0)),
                   pl.BlockSpec((tm, d), lambda i: (i, 0)),
                   pl.BlockSpec((TOP_K, tm), lambda i: (0, i)),
                   pl.BlockSpec((tm, LANES), lambda i: (i, 0))],
        out_shape=[jax.ShapeDtypeStruct((t, d), F32),
                   jax.ShapeDtypeStruct((t, d), F32),
                   jax.ShapeDtypeStruct((TOP_K, t), I32),
                   jax.ShapeDtypeStruct((t, LANES), F32)],
        compiler_params=_cparams("arbitrary"),
        name="outproj_router",
    )(x2, *ys, w_out, mod, w_norm2, w_route, b_route)


def _rank_kernel(eid_ref, rank_ref, count_ref, carry):
    @pl.when(pl.program_id(0) == 0)
    def _():
        carry[...] = jnp.zeros_like(carry)

    tr = eid_ref.shape[1]
    expert = lax.broadcasted_iota(I32, (N_EXPERTS, tr), 0)
    before = (lax.broadcasted_iota(I32, (tr, tr), 0) < lax.broadcasted_iota(I32, (tr, tr), 1)).astype(BF16)
    base = carry[...]
    for k in range(TOP_K):
        onehot = (expert == eid_ref[k:k + 1, :]).astype(F32)
        prefix = _dot(onehot.astype(BF16), before)
        rank_ref[k:k + 1, :] = jnp.sum(onehot * (base + prefix), axis=0, keepdims=True).astype(I32)
        base = base + jnp.sum(onehot, axis=1, keepdims=True)
    carry[...] = base
    count_ref[...] = jnp.broadcast_to(base, count_ref.shape)


def _expert_ranks(eids, tr=512):
    t = eids.shape[1]
    return pl.pallas_call(
        _rank_kernel,
        grid=(t // tr,),
        in_specs=[pl.BlockSpec((TOP_K, tr), lambda i: (0, i))],
        out_specs=[pl.BlockSpec((TOP_K, tr), lambda i: (0, i)),
                   pl.BlockSpec((N_EXPERTS, LANES), lambda i: (0, 0))],
        out_shape=[jax.ShapeDtypeStruct((TOP_K, t), I32),
                   jax.ShapeDtypeStruct((N_EXPERTS, LANES), F32)],
        scratch_shapes=[pltpu.VMEM((N_EXPERTS, 1), F32)],
        compiler_params=_cparams("arbitrary"),
        name="expert_ranks",
    )(eids)


def _dest_kernel(pstart_ref, eid_ref, rank_ref, dest_ref):
    e = eid_ref[...]
    dest = rank_ref[...]
    for j in range(N_EXPERTS):
        dest = dest + jnp.where(e == j, pstart_ref[j], 0)
    dest_ref[...] = dest


def _dest_rows(p_starts, eids, ranks, tm=2048):
    t = eids.shape[1]
    grid_spec = pltpu.PrefetchScalarGridSpec(
        num_scalar_prefetch=1,
        grid=(t // tm,),
        in_specs=[pl.BlockSpec((TOP_K, tm), lambda i, ps: (0, i)),
                  pl.BlockSpec((TOP_K, tm), lambda i, ps: (0, i))],
        out_specs=pl.BlockSpec((TOP_K, tm), lambda i, ps: (0, i)))
    return pl.pallas_call(
        _dest_kernel,
        grid_spec=grid_spec,
        out_shape=jax.ShapeDtypeStruct((TOP_K, t), I32),
        compiler_params=_cparams("arbitrary"),
        name="dest_rows",
    )(p_starts, eids, ranks)


ROW_TILE = IDX_CHUNK // TOP_K


def _load_tile_indices(idx_hbm, idx_smem, idx_sem):
    copy = pltpu.make_async_copy(idx_hbm.at[pl.ds(pl.program_id(0) * IDX_CHUNK, IDX_CHUNK)], idx_smem, idx_sem)
    copy.start()
    copy.wait()


def _dispatch_kernel(pend_ref, padded_ref, nu_ref, idx_hbm, h_ref, xb_hbm, idx_smem, zero_blk, idx_sem, row_sem,
                     zero_sem):
    def zero_block(start):
        return pltpu.make_async_copy(zero_blk, xb_hbm.at[pl.ds(pl.multiple_of(start, MOE_BLOCK), MOE_BLOCK)],
                                     zero_sem)

    @pl.when(pl.program_id(0) == 0)
    def _():
        zero_blk[...] = jnp.zeros_like(zero_blk)
        for e in range(N_EXPERTS):
            @pl.when(padded_ref[e] > 0)
            def _(e=e):
                zero_block(pend_ref[e] - MOE_BLOCK).start()
        for e in range(N_EXPERTS):
            @pl.when(padded_ref[e] > 0)
            def _(e=e):
                zero_block(pend_ref[e] - MOE_BLOCK).wait()

        def zero_unused(b, carry):
            copy = zero_block(b * MOE_BLOCK)
            copy.start()
            copy.wait()
            return carry

        lax.fori_loop(nu_ref[0], xb_hbm.shape[0] // MOE_BLOCK, zero_unused, 0)

    _load_tile_indices(idx_hbm, idx_smem, idx_sem)

    def row_copy(r, dst_row):
        return pltpu.make_async_copy(h_ref.at[pl.ds(r, 1)], xb_hbm.at[pl.ds(dst_row, 1)], row_sem)

    def issue(r, carry):
        for k in range(TOP_K):
            row_copy(r, idx_smem[k * ROW_TILE + r]).start(priority=k)
        return carry

    lax.fori_loop(0, ROW_TILE, issue, 0, unroll=8)

    def drain(r, carry):
        for k in range(TOP_K):
            row_copy(0, 0).wait()
        return carry

    lax.fori_loop(0, ROW_TILE, drain, 0, unroll=8)


def _dispatch(p_ends, padded, n_used, idx_tiles, h2, dst_rows):
    t, d = h2.shape
    any_spec = pl.BlockSpec(memory_space=pl.ANY)
    grid_spec = pltpu.PrefetchScalarGridSpec(
        num_scalar_prefetch=3,
        grid=(t // ROW_TILE,),
        in_specs=[any_spec, pl.BlockSpec((ROW_TILE, d), lambda i, pe, pd, nu: (i, 0))],
        out_specs=any_spec,
        scratch_shapes=[pltpu.SMEM((IDX_CHUNK,), I32), pltpu.VMEM((MOE_BLOCK, d), h2.dtype),
                        pltpu.SemaphoreType.DMA, pltpu.SemaphoreType.DMA, pltpu.SemaphoreType.DMA])
    return pl.pallas_call(
        _dispatch_kernel,
        grid_spec=grid_spec,
        out_shape=jax.ShapeDtypeStruct((dst_rows, d), h2.dtype),
        compiler_params=_cparams("arbitrary"),
        name="moe_dispatch",
    )(p_ends, padded, n_used, idx_tiles, h2)


def _expert_kernel(be_ref, nu_ref, x_ref, wg_ref, wu_ref, wd_ref, y_ref, wg_b, wu_b, wd_b):
    b = pl.program_id(0)

    @pl.when(b < nu_ref[0])
    def _():
        @pl.when(jnp.logical_or(b == 0, be_ref[b] != be_ref[jnp.maximum(b - 1, 0)]))
        def _():
            wg_b[...] = wg_ref[0].astype(BF16)
            wu_b[...] = wu_ref[0].astype(BF16)
            wd_b[...] = wd_ref[0].astype(BF16)

        x = x_ref[...].astype(BF16)
        a = _silu(_dot(x, wg_b[...])) * _dot(x, wu_b[...])
        y_ref[...] = _dot(a.astype(BF16), wd_b[...])


def _expert_mlp(block_e, n_used, xb, w_gate, w_up, w_down):
    p, d = xb.shape
    de = w_gate.shape[2]
    blk = lambda b, be, nu: (jnp.minimum(b, nu[0] - 1), 0)
    grid_spec = pltpu.PrefetchScalarGridSpec(
        num_scalar_prefetch=2,
        grid=(p // MOE_BLOCK,),
        in_specs=[pl.BlockSpec((MOE_BLOCK, d), blk),
                  pl.BlockSpec((1, d, de), lambda b, be, nu: (be[b], 0, 0)),
                  pl.BlockSpec((1, d, de), lambda b, be, nu: (be[b], 0, 0)),
                  pl.BlockSpec((1, de, d), lambda b, be, nu: (be[b], 0, 0))],
        out_specs=pl.BlockSpec((MOE_BLOCK, d), blk),
        scratch_shapes=[pltpu.VMEM((d, de), BF16), pltpu.VMEM((d, de), BF16), pltpu.VMEM((de, d), BF16)])
    return pl.pallas_call(
        _expert_kernel,
        grid_spec=grid_spec,
        out_shape=jax.ShapeDtypeStruct((p, d), F32),
        compiler_params=_cparams("arbitrary"),
        name="expert_mlp",
    )(block_e, n_used, xb, w_gate, w_up, w_down)


def _combine_kernel(idx_hbm, yb_hbm, x_ref, gcol_ref, mod_ref, wnf_ref, o_ref, ybuf, idx_smem, idx_sem, row_sem,
                    *, final):
    _load_tile_indices(idx_hbm, idx_smem, idx_sem)

    def row_copy(src_row, k, r):
        return pltpu.make_async_copy(yb_hbm.at[pl.ds(src_row, 1)], ybuf.at[k, pl.ds(r, 1)], row_sem)

    def issue(r, carry):
        for k in range(TOP_K):
            row_copy(idx_smem[k * ROW_TILE + r], k, r).start(priority=k)
        return carry

    lax.fori_loop(0, ROW_TILE, issue, 0, unroll=8)

    def drain(r, carry):
        for k in range(TOP_K):
            row_copy(0, k, 0).wait()
        return carry

    lax.fori_loop(0, ROW_TILE, drain, 0, unroll=8)

    gc = gcol_ref[...]
    moe = gc[:, 0:1] * ybuf[0] + gc[:, 1:2] * ybuf[1]
    x2 = x_ref[...] + mod_ref[0][5:6, :] * moe
    o_ref[...] = _rms_norm(x2, wnf_ref[...]) if final else x2


def _combine(idx_tiles, yb, x2, gcol, mod, w_norm_final, seq, final):
    t, d = x2.shape
    tm = ROW_TILE
    spb = seq // tm
    any_spec = pl.BlockSpec(memory_space=pl.ANY)
    return pl.pallas_call(
        functools.partial(_combine_kernel, final=final),
        grid=(t // tm,),
        in_specs=[any_spec, any_spec,
                  pl.BlockSpec((tm, d), lambda i: (i, 0)),
                  pl.BlockSpec((tm, LANES), lambda i: (i, 0)),
                  pl.BlockSpec((1, 6, d), lambda i: (i // spb, 0, 0)),
                  pl.BlockSpec((1, d), lambda i: (0, 0))],
        out_specs=pl.BlockSpec((tm, d), lambda i: (i, 0)),
        out_shape=jax.ShapeDtypeStruct((t, d), F32),
        scratch_shapes=[pltpu.VMEM((TOP_K, tm, d), F32), pltpu.SMEM((IDX_CHUNK,), I32),
                        pltpu.SemaphoreType.DMA, pltpu.SemaphoreType.DMA],
        compiler_params=_cparams("arbitrary"),
        name="moe_combine",
    )(idx_tiles, yb, x2, gcol, mod, w_norm_final)


def _lane_row(pieces, width=LANES):
    row = jnp.zeros((width,), F32)
    for off, vec in pieces.items():
        row = row.at[off:off + vec.shape[0]].set(vec.astype(F32))
    return row.reshape(1, width)


def _split_w_in(w_in):
    gw, nh = GROUP_WIDTH, N_HEADS
    widths = [gw, gw, gw, nh, gw, gw, gw, 3 * gw, nh, gw, gw, gw, gw, nh, nh]
    cuts, acc = [], 0
    for w in widths[:-1]:
        acc += w
        cuts.append(acc)
    (aq, ak, av, af, su, sv, mz, mxbc, mdt, lq, lk, lv, lo, li, lf) = jnp.split(w_in, cuts, axis=1)
    scale = HEAD_DIM ** -0.5
    wmain = jnp.concatenate([ak, su, sv, mz, mxbc, lq, lk * scale, lv, lo], axis=1).astype(BF16)
    wqt = (aq * scale).T.astype(BF16)
    wvt = av.T.astype(BF16)
    wg = jnp.zeros((w_in.shape[0], LANES), F32)
    for off, w in ((GATE_AF, af), (GATE_DT, mdt), (GATE_LI, li), (GATE_LF, lf)):
        wg = wg.at[:, off:off + nh].set(w)
    return wmain, wqt, wvt, wg.astype(BF16)


def _moe_layer(x1, h2, eids, gcol, mod_l, w_gate, w_up, w_down, layer, w_norm_final, seq, final):
    t, d = x1.shape
    ranks, counts = _expert_ranks(eids)
    counts = counts[:, 0].astype(I32)
    padded = ((counts + MOE_BLOCK - 1) // MOE_BLOCK) * MOE_BLOCK
    p_ends = jnp.cumsum(padded)
    p_starts = (p_ends - padded).astype(I32)
    n_blocks = (t * TOP_K) // MOE_BLOCK + N_EXPERTS
    blocks = jnp.arange(n_blocks, dtype=I32)
    block_e = jnp.sum((p_ends[None, :] <= (blocks * MOE_BLOCK)[:, None]).astype(I32), axis=1)
    block_e = jnp.minimum(block_e, N_EXPERTS - 1)
    n_used = (p_ends[-1:] // MOE_BLOCK).astype(I32)
    block_e = jnp.where(blocks < n_used, block_e, block_e[n_used[0] - 1])
    dest = _dest_rows(p_starts, eids, ranks)
    idx_tiles = dest.reshape(TOP_K, t // ROW_TILE, ROW_TILE).transpose(1, 0, 2).reshape(-1)
    xb = _dispatch(p_ends.astype(I32), padded.astype(I32), n_used, idx_tiles, h2, n_blocks * MOE_BLOCK)
    yb = _expert_mlp(block_e + layer * N_EXPERTS, n_used, xb, w_gate, w_up, w_down)
    return _combine(idx_tiles, yb, x1, gcol, mod_l, w_norm_final, seq, final)


def kernel(x, c, w_in, w_out, w_mix_norm, attn_f_bias, sg_w, sg_b, ssm_conv_w, ssm_conv_b, ssm_dt_bias,
           ssm_a_log, ssm_d, mlstm_i_bias, mlstm_f_bias, w_ada, b_ada, w_norm1, w_norm2, w_router_group,
           b_router_group, w_router_expert, b_router_expert, w_expert_gate, w_expert_up, w_expert_down,
           w_norm_final):
    bsz, seq, d = x.shape
    depth = w_in.shape[0]
    gw = GROUP_WIDTH
    mod = _ada_modulation(c, w_ada, b_ada).reshape(depth, bsz, 6, d)
    x2 = x.reshape(bsz * seq, d)
    wnf = w_norm_final.reshape(1, d)
    w_eg = w_expert_gate.reshape((depth * N_EXPERTS,) + w_expert_gate.shape[2:])
    w_eu = w_expert_up.reshape((depth * N_EXPERTS,) + w_expert_up.shape[2:])
    w_ed = w_expert_down.reshape((depth * N_EXPERTS,) + w_expert_down.shape[2:])
    for l in range(depth):
        wmain, wqt, wvt, wg = _split_w_in(w_in[l])
        gains = w_mix_norm[l].reshape(N_HEADS, 1, gw)
        proj, qt, vt, gates = _inproj(x2, mod[l], w_norm1[l].reshape(1, d), wmain, wqt, wvt, wg, bsz, seq)
        f2 = _forget_cumsum(gates, _lane_row({GATE_AF: attn_f_bias[l]}), bsz, seq)
        y_attn = _attention(proj, qt, vt, f2, gains[0], bsz, seq)
        sg_bias_full = jnp.repeat(sg_b[l].T, HEAD_DIM, axis=1)
        y_sg = _spatial_gating(proj, gains[1], sg_w[l], sg_bias_full)
        y_ssm = _ssd_mixer(proj, gates, ssm_conv_w[l], ssm_conv_b[l].reshape(1, -1),
                           _lane_row({GATE_DT: ssm_dt_bias[l]}), _lane_row({GATE_DT: ssm_a_log[l]}),
                           jnp.repeat(ssm_d[l], HEAD_DIM).reshape(1, gw), gains[2], bsz, seq)
        y_ml = _mlstm_mixer(proj, gates, _lane_row({GATE_LI: mlstm_i_bias[l], GATE_LF: mlstm_f_bias[l]}),
                            gains[3], bsz, seq)
        w_route = jnp.zeros((d, LANES), F32)
        w_route = w_route.at[:, ROUTE_G:ROUTE_G + N_EXPERT_GROUPS].set(w_router_group[l])
        w_route = w_route.at[:, ROUTE_E:ROUTE_E + N_EXPERTS].set(w_router_expert[l]).astype(BF16)
        b_route = _lane_row({ROUTE_G: b_router_group[l], ROUTE_E: b_router_expert[l]})
        x1, h2, eids, gcol = _outproj_router(x2, (y_attn, y_sg, y_ssm, y_ml), w_out[l].astype(BF16), mod[l],
                                             w_norm2[l].reshape(1, d), w_route, b_route, seq)
        x2 = _moe_layer(x1, h2, eids, gcol, mod[l], w_eg, w_eu, w_ed, l, wnf, seq, final=(l == depth - 1))
    return x2.reshape(bsz, seq, d)
```

```python
import functools

import jax
import jax.numpy as jnp
from jax import lax
from jax.experimental import pallas as pl
from jax.experimental.pallas import tpu as pltpu

F32 = jnp.float32
BF16 = jnp.bfloat16
I32 = jnp.int32

LANES = 128
SUBLANES = 8
HEAD_DIM = 64
N_HEADS = 4
GROUP_WIDTH = 256
CHUNK = 128
MIX_TILE = 4 * CHUNK
SSM_GROUPS = 2
CONV_WIDTH = 4
N_EXPERT_GROUPS = 4
EXPERTS_PER_GROUP = 8
N_EXPERTS = N_EXPERT_GROUPS * EXPERTS_PER_GROUP
TOP_K = 2
MOE_BLOCK = 256
NORM_EPS = 1e-6
MLSTM_M_INIT = -1e30
NEG_INF = float("-inf")
VMEM_LIMIT_BYTES = 48 * 1024 * 1024
IDX_CHUNK = 1024

(PB_K, PB_SU, PB_SV, PB_MZ, PB_X, PB_B, PB_C, PB_LQ, PB_LK, PB_LV, PB_LO) = range(11)
PROJ_COLS = 11 * GROUP_WIDTH
LOG2_E = 1.4426950408889634
GATE_AF, GATE_DT, GATE_LI, GATE_LF = 0, 4, 8, 12
ROUTE_G, ROUTE_E = 0, 8

NT_DIMS = (((1,), (1,)), ((), ()))


def _cparams(*sem):
    return pltpu.CompilerParams(dimension_semantics=sem, vmem_limit_bytes=VMEM_LIMIT_BYTES)


def _dot(a, b):
    return jnp.dot(a, b, preferred_element_type=F32)


def _dot_nt(a, b):
    return lax.dot_general(a, b, NT_DIMS, preferred_element_type=F32)


def _dot_exact(a, b):
    return jnp.dot(a, b, preferred_element_type=F32, precision=lax.Precision.HIGHEST)


def _head_of_lane(shape, axis=1):
    return lax.broadcasted_iota(I32, shape, axis) // HEAD_DIM


def _sigmoid(x):
    return 1.0 / (1.0 + jnp.exp(-x))


def _silu(x):
    return x * _sigmoid(x)


def _log_sigmoid(x):
    return jnp.minimum(x, 0.0) - jnp.log1p(jnp.exp(-jnp.abs(x)))


def _softplus(x):
    return jnp.maximum(x, 0.0) + jnp.log1p(jnp.exp(-jnp.abs(x)))


def _gelu_tanh(x):
    return 0.5 * x * (1.0 + jnp.tanh(0.7978845608028654 * (x + 0.044715 * (x * x * x))))


def _expand_heads(cols, width=GROUP_WIDTH):
    rows = cols[0].shape[0]
    head = _head_of_lane((rows, width))
    out = jnp.broadcast_to(cols[N_HEADS - 1], (rows, width))
    for h in range(N_HEADS - 2, -1, -1):
        out = jnp.where(head == h, jnp.broadcast_to(cols[h], (rows, width)), out)
    return out


def _mask_head(x, h):
    return jnp.where(_head_of_lane(x.shape) == h, x, jnp.zeros_like(x))


def _head_rms_norm(y, gain):
    head = _head_of_lane(y.shape)
    sq = y * y
    cols = [jnp.sum(jnp.where(head == h, sq, 0.0), axis=1, keepdims=True) * (1.0 / HEAD_DIM)
            for h in range(N_HEADS)]
    return y * lax.rsqrt(_expand_heads(cols) + NORM_EPS) * gain


def _rms_norm(x, gain):
    ms = jnp.mean(x * x, axis=1, keepdims=True)
    return x * lax.rsqrt(ms + NORM_EPS) * gain


def _pack_bf16_pairs(x):
    half = x.shape[1] // 2
    bits = lax.bitcast_convert_type(x.astype(BF16).astype(F32), jnp.uint32)
    return bits[:, :half] | lax.shift_right_logical(bits[:, half:], jnp.uint32(16))


def _unpack_bf16_pairs(words):
    first = lax.bitcast_convert_type(words & jnp.uint32(0xFFFF0000), F32)
    second = lax.bitcast_convert_type(lax.shift_left(words, jnp.uint32(16)), F32)
    return first, second


def _tril(n, strict=False):
    r = lax.broadcasted_iota(I32, (n, n), 0)
    c = lax.broadcasted_iota(I32, (n, n), 1)
    return (r > c) if strict else (r >= c)


def _ada_kernel(c_ref, w_ref, b_ref, o_ref):
    o_ref[0] = _dot_exact(_silu(c_ref[...]), w_ref[0]) + b_ref[0]


def _ada_modulation(c, w_ada, b_ada):
    depth, d, d6 = w_ada.shape
    bsz = c.shape[0]
    return pl.pallas_call(
        _ada_kernel,
        grid=(depth, d6 // d),
        in_specs=[pl.BlockSpec((bsz, d), lambda l, j: (0, 0)),
                  pl.BlockSpec((1, d, d), lambda l, j: (l, 0, j)),
                  pl.BlockSpec((1, 1, d), lambda l, j: (l, 0, j))],
        out_specs=pl.BlockSpec((1, bsz, d), lambda l, j: (l, 0, j)),
        out_shape=jax.ShapeDtypeStruct((depth, bsz, d6), F32),
        compiler_params=_cparams("arbitrary", "arbitrary"),
        name="ada_modulation",
    )(c, w_ada, b_ada.reshape(depth, 1, d6))


def _inproj_kernel(x_ref, mod_ref, wn_ref, wmain_ref, wqt_ref, wvt_ref, wg_ref,
                   proj_ref, qt_ref, vt_ref, gates_ref, h_scr):
    mod = mod_ref[0]
    h = _rms_norm(x_ref[...], wn_ref[...]) * (1.0 + mod[1:2, :]) + mod[0:1, :]
    h_scr[...] = h.astype(BF16)
    for c0 in range(0, PROJ_COLS, GROUP_WIDTH):
        proj_ref[:, c0:c0 + GROUP_WIDTH] = _dot(h_scr[...], wmain_ref[:, c0:c0 + GROUP_WIDTH]).astype(BF16)
    qt_ref[0] = (_dot_nt(wqt_ref[...], h_scr[...]) * LOG2_E).astype(BF16)
    vt_ref[0] = _dot_nt(wvt_ref[...], h_scr[...]).astype(BF16)
    gates_ref[...] = _dot(h_scr[...], wg_ref[...])


def _inproj(x2, mod, w_norm, wmain, wqt, wvt, wg, bsz, seq, tm=512):
    t, d = x2.shape
    spb = seq // tm
    tspec = pl.BlockSpec((1, GROUP_WIDTH, tm), lambda i: (i // spb, 0, i % spb))
    return pl.pallas_call(
        _inproj_kernel,
        grid=(t // tm,),
        in_specs=[pl.BlockSpec((tm, d), lambda i: (i, 0)),
                  pl.BlockSpec((1, 6, d), lambda i: (i // spb, 0, 0)),
                  pl.BlockSpec((1, d), lambda i: (0, 0)),
                  pl.BlockSpec((d, PROJ_COLS), lambda i: (0, 0)),
                  pl.BlockSpec((GROUP_WIDTH, d), lambda i: (0, 0)),
                  pl.BlockSpec((GROUP_WIDTH, d), lambda i: (0, 0)),
                  pl.BlockSpec((d, LANES), lambda i: (0, 0))],
        out_specs=[pl.BlockSpec((tm, PROJ_COLS), lambda i: (i, 0)), tspec, tspec,
                   pl.BlockSpec((tm, LANES), lambda i: (i, 0))],
        out_shape=[jax.ShapeDtypeStruct((t, PROJ_COLS), BF16),
                   jax.ShapeDtypeStruct((bsz, GROUP_WIDTH, seq), BF16),
                   jax.ShapeDtypeStruct((bsz, GROUP_WIDTH, seq), BF16),
                   jax.ShapeDtypeStruct((t, LANES), F32)],
        scratch_shapes=[pltpu.VMEM((tm, d), BF16)],
        compiler_params=_cparams("arbitrary"),
        name="norm_inproj",
    )(x2, mod, w_norm, wmain, wqt, wvt, wg)


def _fcum_kernel(g_ref, b_ref, f_ref, carry):
    @pl.when(pl.program_id(1) == 0)
    def _():
        carry[...] = jnp.zeros_like(carry)

    tb = g_ref.shape[0]
    ls = _log_sigmoid(g_ref[...] + b_ref[...])
    cum = _dot_exact(_tril(tb).astype(F32), ls) + carry[...]
    carry[...] = cum[tb - 1:tb, :]
    f2 = cum * LOG2_E
    hi = f2.astype(BF16)
    rest = f2 - hi.astype(F32)
    mid = rest.astype(BF16)
    lo = (rest - mid.astype(F32)).astype(BF16)
    r = lax.broadcasted_iota(I32, (LANES, GROUP_WIDTH), 0) - GATE_AF
    c = lax.broadcasted_iota(I32, (LANES, GROUP_WIDTH), 1)
    slab = jnp.zeros((tb, GROUP_WIDTH), F32)
    for j, piece in enumerate((hi, mid, lo)):
        place = (r >= 0) & (r < N_HEADS) & (c == ((r + 1) % N_HEADS) * HEAD_DIM + j)
        slab = slab + _dot(piece, jnp.where(place, -1.0, 0.0).astype(BF16))
    f_ref[...] = slab.astype(BF16)


def _forget_cumsum(gates, f_bias_row, bsz, seq, tb=512):
    spb = seq // tb
    return pl.pallas_call(
        _fcum_kernel,
        grid=(bsz, spb),
        in_specs=[pl.BlockSpec((tb, LANES), lambda b, j: (b * spb + j, 0)),
                  pl.BlockSpec((1, LANES), lambda b, j: (0, 0))],
        out_specs=pl.BlockSpec((tb, GROUP_WIDTH), lambda b, j: (b * spb + j, 0)),
        out_shape=jax.ShapeDtypeStruct((bsz * seq, GROUP_WIDTH), BF16),
        scratch_shapes=[pltpu.VMEM((1, LANES), F32)],
        compiler_params=_cparams("arbitrary", "arbitrary"),
        name="forget_cumsum",
    )(gates, f_bias_row)


ONES_ROWS = 16


def _attn_kernel(qi_ref, kj_ref, k_ref, qt_ref, vt_ref, f_ref, gain_ref, o_ref, qtm, m_s, l_s, acc):
    p = pl.program_id(1)
    qi, kj = qi_ref[p], kj_ref[p]
    tk = k_ref.shape[0]
    tq = qt_ref.shape[2]

    @pl.when(kj == 0)
    def _():
        qt = qt_ref[0]
        row = lax.broadcasted_iota(I32, qt.shape, 0)
        for h in range(N_HEADS):
            slot = ((h + 1) % N_HEADS) * HEAD_DIM
            ones_rows = jnp.where((row >= slot) & (row < slot + 3), 1.0, 0.0).astype(BF16)
            qtm[h] = jnp.where(row // HEAD_DIM == h, qt, ones_rows)
        m_s[...] = jnp.full_like(m_s, NEG_INF)
        l_s[...] = jnp.zeros_like(l_s)
        acc[...] = jnp.zeros_like(acc)

    def step(diagonal):
        k = k_ref[...]
        bias = f_ref[...]
        head = _head_of_lane(k.shape)
        ones = jnp.ones((ONES_ROWS, tk), BF16)
        if diagonal:
            visible = lax.broadcasted_iota(I32, (tk, tq), 0) <= lax.broadcasted_iota(I32, (tk, tq), 1)
        scores, m_news, alphas, probs = {}, {}, {}, {}

        def score(h):
            s = _dot(jnp.where(head == h, k, bias), qtm[h])
            if diagonal:
                s = jnp.where(visible, s, NEG_INF)
            scores[h] = s
            m_prev = m_s[h]
            m_news[h] = jnp.maximum(m_prev, jnp.max(s, axis=0, keepdims=True))
            alphas[h] = jnp.exp2(m_prev - m_news[h])
            m_s[h] = m_news[h]

        def prob(h):
            probs[h] = jnp.exp2(scores[h] - m_news[h]).astype(BF16)

        def weighted_sum(h):
            vt_ext = jnp.concatenate([vt_ref[0, h * HEAD_DIM:(h + 1) * HEAD_DIM, :], ones], axis=0)
            pv = _dot(vt_ext, probs[h])
            acc[h] = alphas[h] * acc[h] + pv[0:HEAD_DIM, :]
            l_s[h] = alphas[h] * l_s[h] + pv[HEAD_DIM:HEAD_DIM + 1, :]

        for stage in range(N_HEADS + 2):
            if stage < N_HEADS:
                score(stage)
            if 0 <= stage - 1 < N_HEADS:
                prob(stage - 1)
            if 0 <= stage - 2 < N_HEADS:
                weighted_sum(stage - 2)

    @pl.when(kj < qi)
    def _():
        step(False)

    @pl.when(kj == qi)
    def _():
        step(True)
        yt = jnp.concatenate([acc[h] / l_s[h] for h in range(N_HEADS)], axis=0)
        o_ref[...] = _head_rms_norm(yt.T, gain_ref[...]).astype(BF16)


def _attention(proj, qt, vt, f2, gain, bsz, seq, tq=1024):
    nq = seq // tq
    pairs = [(i, j) for i in range(nq) for j in range(i + 1)]
    qi = jnp.asarray([a for a, _ in pairs], I32)
    kj = jnp.asarray([b for _, b in pairs], I32)
    t = bsz * seq
    grid_spec = pltpu.PrefetchScalarGridSpec(
        num_scalar_prefetch=2,
        grid=(bsz, len(pairs)),
        in_specs=[pl.BlockSpec((tq, GROUP_WIDTH), lambda b, p, qi, kj: (b * nq + kj[p], PB_K)),
                  pl.BlockSpec((1, GROUP_WIDTH, tq), lambda b, p, qi, kj: (b, 0, qi[p])),
                  pl.BlockSpec((1, GROUP_WIDTH, tq), lambda b, p, qi, kj: (b, 0, kj[p])),
                  pl.BlockSpec((tq, GROUP_WIDTH), lambda b, p, qi, kj: (b * nq + kj[p], 0)),
                  pl.BlockSpec((1, GROUP_WIDTH), lambda b, p, qi, kj: (0, 0))],
        out_specs=pl.BlockSpec((tq, GROUP_WIDTH), lambda b, p, qi, kj: (b * nq + qi[p], 0)),
        scratch_shapes=[pltpu.VMEM((N_HEADS, GROUP_WIDTH, tq), BF16),
                        pltpu.VMEM((N_HEADS, 1, tq), F32),
                        pltpu.VMEM((N_HEADS, 1, tq), F32),
                        pltpu.VMEM((N_HEADS, HEAD_DIM, tq), F32)])
    return pl.pallas_call(
        _attn_kernel,
        grid_spec=grid_spec,
        out_shape=jax.ShapeDtypeStruct((t, GROUP_WIDTH), BF16),
        compiler_params=_cparams("arbitrary", "arbitrary"),
        name="fox_attention",
    )(qi, kj, proj, qt, vt, f2, gain)


def _sg_kernel(u_ref, v_ref, gain_ref, w_ref, b_ref, o_ref):
    tm = u_ref.shape[0]
    u = _gelu_tanh(u_ref[...].astype(F32))
    v = _head_rms_norm(_gelu_tanh(v_ref[...].astype(F32)), gain_ref[...])
    causal = _tril(CHUNK)
    ws = [jnp.where(causal, w_ref[h], 0.0).astype(BF16) for h in range(N_HEADS)]
    for c0 in range(0, tm, CHUNK):
        vc = v[c0:c0 + CHUNK, :].astype(BF16)
        mixed = b_ref[...]
        for h in range(N_HEADS):
            mixed = mixed + _dot(ws[h], _mask_head(vc, h))
        o_ref[c0:c0 + CHUNK, :] = (u[c0:c0 + CHUNK, :] * mixed).astype(BF16)


def _spatial_gating(proj, gain, sg_w, sg_bias_full, tm=512):
    t = proj.shape[0]
    return pl.pallas_call(
        _sg_kernel,
        grid=(t // tm,),
        in_specs=[pl.BlockSpec((tm, GROUP_WIDTH), lambda i: (i, PB_SU)),
                  pl.BlockSpec((tm, GROUP_WIDTH), lambda i: (i, PB_SV)),
                  pl.BlockSpec((1, GROUP_WIDTH), lambda i: (0, 0)),
                  pl.BlockSpec((N_HEADS, CHUNK, CHUNK), lambda i: (0, 0, 0)),
                  pl.BlockSpec((CHUNK, GROUP_WIDTH), lambda i: (0, 0))],
        out_specs=pl.BlockSpec((tm, GROUP_WIDTH), lambda i: (i, 0)),
        out_shape=jax.ShapeDtypeStruct((t, GROUP_WIDTH), BF16),
        compiler_params=_cparams("arbitrary"),
        name="spatial_gating",
    )(proj, proj, gain, sg_w, sg_bias_full)


def _ssd_kernel(z_ref, x_ref, b_ref, c_ref, g_ref, cw_ref, cb_ref, dtb_ref, alog_ref, dskip_ref, gain_ref,
                o_ref, conv_scr, xbc_scr, state):
    L = CHUNK
    W = GROUP_WIDTH
    tile = x_ref.shape[0]

    @pl.when(pl.program_id(1) == 0)
    def _():
        conv_scr[0:SUBLANES, :] = jnp.zeros((SUBLANES, 3 * W), F32)
        state[...] = jnp.zeros_like(state)

    conv_scr[SUBLANES:, 0:W] = x_ref[...].astype(F32)
    conv_scr[SUBLANES:, W:2 * W] = b_ref[...].astype(F32)
    conv_scr[SUBLANES:, 2 * W:] = c_ref[...].astype(F32)
    cw = cw_ref[...]
    conv = cb_ref[...] + cw[CONV_WIDTH - 1:CONV_WIDTH, :] * conv_scr[SUBLANES:, :]
    for s in range(1, CONV_WIDTH):
        conv = conv + cw[CONV_WIDTH - 1 - s:CONV_WIDTH - s, :] * conv_scr[SUBLANES - s:SUBLANES - s + tile, :]
    conv_scr[0:SUBLANES, :] = conv_scr[tile:tile + SUBLANES, :]
    xbc_scr[...] = _silu(conv)

    neg_a = -jnp.exp(alog_ref[...])
    for c0 in range(0, tile, L):
        rows = slice(c0, c0 + L)
        dt = _softplus(g_ref[rows, :] + dtb_ref[...])
        y = _ssd_chunk(xbc_scr[rows, 0:W], xbc_scr[rows, W:2 * W], xbc_scr[rows, 2 * W:], dt, dt * neg_a,
                       dskip_ref[...], state)
        y = y * _silu(z_ref[rows, :].astype(F32))
        o_ref[rows, :] = _head_rms_norm(y, gain_ref[...]).astype(BF16)


def _ssd_chunk(xs, bm, cm, dt, da, dskip, state):
    L = CHUNK
    a_cum = _dot_exact(_tril(L).astype(F32), da)
    a_row = a_cum.T
    dt_cols = [dt[:, GATE_DT + h:GATE_DT + h + 1] for h in range(N_HEADS)]
    a_cols = [a_cum[:, GATE_DT + h:GATE_DT + h + 1] for h in range(N_HEADS)]
    a_end = [a_cum[L - 1:L, GATE_DT + h:GATE_DT + h + 1] for h in range(N_HEADS)]
    xdt = xs * _expand_heads(dt_cols)
    xdt_b = xdt.astype(BF16)
    xw = (xdt * _expand_heads([jnp.exp(a_end[h] - a_cols[h]) for h in range(N_HEADS)])).astype(BF16)
    exp_a = _expand_heads([jnp.exp(a_cols[h]) for h in range(N_HEADS)])
    causal = _tril(L)
    half = lax.broadcasted_iota(I32, (1, LANES), 1) < HEAD_DIM

    y = dskip * xs
    y_off = []
    for g in range(SSM_GROUPS):
        bg = bm[:, g * LANES:(g + 1) * LANES]
        cg = cm[:, g * LANES:(g + 1) * LANES].astype(BF16)
        scores = _dot_nt(cg, bg.astype(BF16))
        for r in range(N_HEADS // SSM_GROUPS):
            h = g * (N_HEADS // SSM_GROUPS) + r
            seg = a_cols[h] - a_row[GATE_DT + h:GATE_DT + h + 1, :]
            decay = jnp.exp(jnp.where(causal, seg, NEG_INF))
            y = y + _dot((scores * decay).astype(BF16), _mask_head(xdt_b, h))
        st_in = state[g]
        y_off.append(_dot(cg, st_in.astype(BF16)))
        chunk_decay = jnp.where(half, jnp.exp(a_end[2 * g]), jnp.exp(a_end[2 * g + 1]))
        state[g] = chunk_decay * st_in + _dot(bg.T.astype(BF16), xw[:, g * LANES:(g + 1) * LANES])
    return y + jnp.concatenate(y_off, axis=1) * exp_a


def _ssd_mixer(proj, gates, conv_w, conv_b, dtb_row, alog_row, dskip_row, gain, bsz, seq, tile=MIX_TILE):
    t = proj.shape[0]
    nc = seq // tile
    row = lambda blk: pl.BlockSpec((tile, GROUP_WIDTH), lambda b, j, blk=blk: (b * nc + j, blk))
    const = lambda shape: pl.BlockSpec(shape, lambda b, j: (0,) * len(shape))
    return pl.pallas_call(
        _ssd_kernel,
        grid=(bsz, nc),
        in_specs=[row(PB_MZ), row(PB_X), row(PB_B), row(PB_C),
                  pl.BlockSpec((tile, LANES), lambda b, j: (b * nc + j, 0)),
                  const((CONV_WIDTH, 3 * GROUP_WIDTH)), const((1, 3 * GROUP_WIDTH)),
                  const((1, LANES)), const((1, LANES)), const((1, GROUP_WIDTH)), const((1, GROUP_WIDTH))],
        out_specs=pl.BlockSpec((tile, GROUP_WIDTH), lambda b, j: (b * nc + j, 0)),
        out_shape=jax.ShapeDtypeStruct((t, GROUP_WIDTH), BF16),
        scratch_shapes=[pltpu.VMEM((tile + SUBLANES, 3 * GROUP_WIDTH), F32),
                        pltpu.VMEM((tile, 3 * GROUP_WIDTH), F32),
                        pltpu.VMEM((SSM_GROUPS, LANES, LANES), F32)],
        compiler_params=_cparams("arbitrary", "arbitrary"),
        name="ssd_mixer",
    )(proj, proj, proj, proj, gates, conv_w, conv_b, dtb_row, alog_row, dskip_row, gain)


def _mlstm_kernel(q_ref, k_ref, v_ref, o_gate_ref, g_ref, bias_ref, gain_ref, o_ref, ct, nb, m_row):
    L = CHUNK
    W = GROUP_WIDTH

    @pl.when(pl.program_id(1) == 0)
    def _():
        ct[...] = jnp.zeros_like(ct)
        nb[...] = jnp.zeros_like(nb)
        m_row[...] = jnp.full_like(m_row, MLSTM_M_INIT)

    def chunk(c, carry):
        rows = pl.ds(pl.multiple_of(c * L, L), L)
        y = _mlstm_chunk(q_ref[rows, :], k_ref[rows, :], v_ref[rows, :], g_ref[rows, :] + bias_ref[...],
                         ct, nb, m_row)
        y = _sigmoid(o_gate_ref[rows, :].astype(F32)) * _head_rms_norm(y, gain_ref[...])
        o_ref[rows, :] = y.astype(BF16)
        return carry

    lax.fori_loop(0, q_ref.shape[0] // L, chunk, 0)


def _mlstm_chunk(q, k, v, gate, ct, nb, m_row):
    L = CHUNK
    W = GROUP_WIDTH
    a_full = _dot_exact(_tril(L).astype(F32), _log_sigmoid(gate))
    a_rows = a_full.T
    g_rows = gate.T
    causal = _tril(L)
    lane = lax.broadcasted_iota(I32, (1, LANES), 1)

    inter_q = _dot(q, ct[...].astype(BF16))
    n_q = _dot(q, nb[...].astype(BF16))
    m_old = m_row[...]
    num = jnp.zeros((L, W), F32)
    inter_cols, den_cols, ws_cols, scale_cols = [], [], [], []
    m_next = m_old
    for h in range(N_HEADS):
        a_col = a_full[:, GATE_LF + h:GATE_LF + h + 1]
        i_col = gate[:, GATE_LI + h:GATE_LI + h + 1]
        a_r = a_rows[GATE_LF + h:GATE_LF + h + 1, :]
        i_r = g_rows[GATE_LI + h:GATE_LI + h + 1, :]
        a_end = a_full[L - 1:L, GATE_LF + h:GATE_LF + h + 1]
        m_in = m_old[:, h:h + 1]
        log_d = jnp.where(causal, a_col - a_r + i_r, NEG_INF)
        log_inter = a_col + m_in
        m_t = jnp.maximum(jnp.max(log_d, axis=1, keepdims=True), log_inter)
        w = _dot_nt(_mask_head(q, h), k) * jnp.exp(log_d - m_t)
        inter = jnp.exp(log_inter - m_t)
        num = num + _dot(w.astype(BF16), _mask_head(v, h))
        den = jnp.sum(w, axis=1, keepdims=True) + inter * n_q[:, h:h + 1]
        inter_cols.append(inter)
        den_cols.append(jnp.maximum(jnp.abs(den), jnp.exp(-m_t)))
        g_col = a_end - a_col + i_col
        m_new = jnp.maximum(a_end + m_in, jnp.max(g_col, axis=0, keepdims=True))
        ws_cols.append(jnp.exp(g_col - m_new))
        scale_cols.append(jnp.exp(a_end + m_in - m_new))
        m_next = jnp.where(lane == h, m_new, m_next)
    hout = (num + _expand_heads(inter_cols) * inter_q) / _expand_heads(den_cols)

    kw_t = (k.astype(F32) * _expand_heads(ws_cols)).T.astype(BF16)
    scale_row = _expand_heads(scale_cols)
    same_head = _head_of_lane((W, W), 0) == _head_of_lane((W, W), 1)
    ct[...] = scale_row * ct[...] + jnp.where(same_head, _dot(kw_t, v), 0.0)
    col_is_head = _head_of_lane((W, LANES), 0) == lax.broadcasted_iota(I32, (W, LANES), 1)
    scale_n = scale_cols[N_HEADS - 1]
    for h in range(N_HEADS - 2, -1, -1):
        scale_n = jnp.where(lane == h, scale_cols[h], scale_n)
    nb[...] = scale_n * nb[...] + jnp.where(col_is_head, _dot(kw_t, jnp.ones((L, LANES), BF16)), 0.0)
    m_row[...] = m_next
    return hout


def _mlstm_mixer(proj, gates, bias_row, gain, bsz, seq, tile=MIX_TILE):
    t = proj.shape[0]
    nc = seq // tile
    row = lambda blk: pl.BlockSpec((tile, GROUP_WIDTH), lambda b, j, blk=blk: (b * nc + j, blk))
    const = lambda shape: pl.BlockSpec(shape, lambda b, j: (0,) * len(shape))
    return pl.pallas_call(
        _mlstm_kernel,
        grid=(bsz, nc),
        in_specs=[row(PB_LQ), row(PB_LK), row(PB_LV), row(PB_LO),
                  pl.BlockSpec((tile, LANES), lambda b, j: (b * nc + j, 0)),
                  const((1, LANES)), const((1, GROUP_WIDTH))],
        out_specs=pl.BlockSpec((tile, GROUP_WIDTH), lambda b, j: (b * nc + j, 0)),
        out_shape=jax.ShapeDtypeStruct((t, GROUP_WIDTH), BF16),
        scratch_shapes=[pltpu.VMEM((GROUP_WIDTH, GROUP_WIDTH), F32),
                        pltpu.VMEM((GROUP_WIDTH, LANES), F32),
                        pltpu.VMEM((1, LANES), F32)],
        compiler_params=_cparams("arbitrary", "arbitrary"),
        name="mlstm_mixer",
    )(proj, proj, proj, proj, gates, bias_row, gain)


def _outproj_router_kernel(x_ref, ya_ref, ys_ref, ym_ref, yl_ref, wo_ref, mod_ref, wn_ref, wr_ref, br_ref,
                           xo_ref, h2_ref, eid_ref, gcol_ref):
    W = GROUP_WIDTH
    tm = x_ref.shape[0]
    mod = mod_ref[0]
    out = _dot(ya_ref[...], wo_ref[0:W, :])
    out = out + _dot(ys_ref[...], wo_ref[W:2 * W, :])
    out = out + _dot(ym_ref[...], wo_ref[2 * W:3 * W, :])
    out = out + _dot(yl_ref[...], wo_ref[3 * W:4 * W, :])
    x1 = x_ref[...] + mod[2:3, :] * out
    xo_ref[...] = x1
    h2 = _rms_norm(x1, wn_ref[...]) * (1.0 + mod[4:5, :]) + mod[3:4, :]
    h2_ref[...] = _pack_bf16_pairs(h2)

    logits_t = (_dot(h2.astype(BF16), wr_ref[...]) + br_ref[...]).T
    row8 = lax.broadcasted_iota(I32, (SUBLANES, tm), 0).astype(F32)
    gl = jnp.where(row8 < N_EXPERT_GROUPS, logits_t[ROUTE_G:ROUTE_G + SUBLANES, :], NEG_INF)
    g_max = jnp.max(gl, axis=0, keepdims=True)
    g_sel = jnp.min(jnp.where(gl == g_max, row8, SUBLANES), axis=0, keepdims=True)
    g_prob = 1.0 / jnp.sum(jnp.exp(gl - g_max), axis=0, keepdims=True)
    el = logits_t[ROUTE_E:ROUTE_E + EXPERTS_PER_GROUP, :]
    for g in range(1, N_EXPERT_GROUPS):
        lo = ROUTE_E + g * EXPERTS_PER_GROUP
        el = jnp.where(g_sel == g, logits_t[lo:lo + EXPERTS_PER_GROUP, :], el)
    m1 = jnp.max(el, axis=0, keepdims=True)
    i1 = jnp.min(jnp.where(el == m1, row8, SUBLANES), axis=0, keepdims=True)
    el2 = jnp.where(row8 == i1, NEG_INF, el)
    m2 = jnp.max(el2, axis=0, keepdims=True)
    i2 = jnp.min(jnp.where(el2 == m2, row8, SUBLANES), axis=0, keepdims=True)
    ratio = jnp.exp(m2 - m1)
    p1 = 1.0 / (1.0 + ratio)
    eid_ref[0:1, :] = (g_sel * EXPERTS_PER_GROUP + i1).astype(I32)
    eid_ref[1:2, :] = (g_sel * EXPERTS_PER_GROUP + i2).astype(I32)
    rows = lax.broadcasted_iota(I32, (LANES, tm), 0)
    gate_rows = jnp.where(rows == 0, g_prob * p1, jnp.where(rows == 1, g_prob * p1 * ratio, 0.0))
    gcol_ref[...] = gate_rows.T


def _outproj_router(x2, ys, w_out, mod, w_norm2, w_route, b_route, seq, tm=256):
    t, d = x2.shape
    spb = seq // tm
    ytile = pl.BlockSpec((tm, GROUP_WIDTH), lambda i: (i, 0))
    return pl.pallas_call(
        _outproj_router_kernel,
        grid=(t // tm,),
        in_specs=[pl.BlockSpec((tm, d), lambda i: (i, 0)), ytile, ytile, ytile, ytile,
                  pl.BlockSpec((d, d), lambda i: (0, 0)),
                  pl.BlockSpec((1, 6, d), lambda i: (i // spb, 0, 0)),
                  pl.BlockSpec((1, d), lambda i: (0, 0)),
                  pl.BlockSpec((d, LANES), lambda i: (0, 0)),
                  pl.BlockSpec((1, LANES), lambda i: (0, 0))],
        out_specs=[pl.BlockSpec((tm, d), lambda i: (i, 0)),
                   pl.BlockSpec((tm, d // 2), lambda i: (i, 0)),
                   pl.BlockSpec((TOP_K, tm), lambda i: (0, i)),
                   pl.BlockSpec((tm, LANES), lambda i: (i, 0))],
        out_shape=[jax.ShapeDtypeStruct((t, d), F32),
                   jax.ShapeDtypeStruct((t, d // 2), jnp.uint32),
                   jax.ShapeDtypeStruct((TOP_K, t), I32),
                   jax.ShapeDtypeStruct((t, LANES), F32)],
        compiler_params=_cparams("arbitrary"),
        name="outproj_router",
    )(x2, *ys, w_out, mod, w_norm2, w_route, b_route)


def _rank_kernel(eid_ref, rank_ref, count_ref, carry):
    @pl.when(pl.program_id(0) == 0)
    def _():
        carry[...] = jnp.zeros_like(carry)

    tr = eid_ref.shape[1]
    expert = lax.broadcasted_iota(I32, (N_EXPERTS, tr), 0)
    before = (lax.broadcasted_iota(I32, (tr, tr), 0) < lax.broadcasted_iota(I32, (tr, tr), 1)).astype(BF16)
    base = carry[...]
    for k in range(TOP_K):
        onehot = (expert == eid_ref[k:k + 1, :]).astype(F32)
        prefix = _dot(onehot.astype(BF16), before)
        rank_ref[k:k + 1, :] = jnp.sum(onehot * (base + prefix), axis=0, keepdims=True).astype(I32)
        base = base + jnp.sum(onehot, axis=1, keepdims=True)
    carry[...] = base
    count_ref[...] = jnp.broadcast_to(base, count_ref.shape)


def _expert_ranks(eids, tr=512):
    t = eids.shape[1]
    return pl.pallas_call(
        _rank_kernel,
        grid=(t // tr,),
        in_specs=[pl.BlockSpec((TOP_K, tr), lambda i: (0, i))],
        out_specs=[pl.BlockSpec((TOP_K, tr), lambda i: (0, i)),
                   pl.BlockSpec((N_EXPERTS, LANES), lambda i: (0, 0))],
        out_shape=[jax.ShapeDtypeStruct((TOP_K, t), I32),
                   jax.ShapeDtypeStruct((N_EXPERTS, LANES), F32)],
        scratch_shapes=[pltpu.VMEM((N_EXPERTS, 1), F32)],
        compiler_params=_cparams("arbitrary"),
        name="expert_ranks",
    )(eids)


def _dest_kernel(pstart_ref, eid_ref, rank_ref, dest_ref):
    e = eid_ref[...]
    dest = rank_ref[...]
    for j in range(N_EXPERTS):
        dest = dest + jnp.where(e == j, pstart_ref[j], 0)
    dest_ref[...] = dest


def _dest_rows(p_starts, eids, ranks, tm=2048):
    t = eids.shape[1]
    grid_spec = pltpu.PrefetchScalarGridSpec(
        num_scalar_prefetch=1,
        grid=(t // tm,),
        in_specs=[pl.BlockSpec((TOP_K, tm), lambda i, ps: (0, i)),
                  pl.BlockSpec((TOP_K, tm), lambda i, ps: (0, i))],
        out_specs=pl.BlockSpec((TOP_K, tm), lambda i, ps: (0, i)))
    return pl.pallas_call(
        _dest_kernel,
        grid_spec=grid_spec,
        out_shape=jax.ShapeDtypeStruct((TOP_K, t), I32),
        compiler_params=_cparams("arbitrary"),
        name="dest_rows",
    )(p_starts, eids, ranks)


ROW_TILE = IDX_CHUNK // TOP_K


def _load_tile_indices(idx_hbm, idx_smem, idx_sem, tile):
    copy = pltpu.make_async_copy(idx_hbm.at[pl.ds(tile * IDX_CHUNK, IDX_CHUNK)], idx_smem, idx_sem)
    copy.start()
    copy.wait()


def _dispatch_kernel(pend_ref, padded_ref, nu_ref, idx_hbm, h_ref, xb_hbm, idx_smem, zero_blk, idx_sem, row_sem,
                     zero_sem):
    def zero_block(start):
        return pltpu.make_async_copy(zero_blk, xb_hbm.at[pl.ds(pl.multiple_of(start, MOE_BLOCK), MOE_BLOCK)],
                                     zero_sem)

    @pl.when(pl.program_id(0) == 0)
    def _():
        zero_blk[...] = jnp.zeros_like(zero_blk)
        for e in range(N_EXPERTS):
            @pl.when(padded_ref[e] > 0)
            def _(e=e):
                zero_block(pend_ref[e] - MOE_BLOCK).start()
        for e in range(N_EXPERTS):
            @pl.when(padded_ref[e] > 0)
            def _(e=e):
                zero_block(pend_ref[e] - MOE_BLOCK).wait()

        def zero_unused(b, carry):
            copy = zero_block(b * MOE_BLOCK)
            copy.start()
            copy.wait()
            return carry

        lax.fori_loop(nu_ref[0], xb_hbm.shape[0] // MOE_BLOCK, zero_unused, 0)

    _load_tile_indices(idx_hbm, idx_smem, idx_sem, pl.program_id(0))

    def row_copy(r, dst_row):
        return pltpu.make_async_copy(h_ref.at[pl.ds(r, 1)], xb_hbm.at[pl.ds(dst_row, 1)], row_sem)

    def issue(r, carry):
        for k in range(TOP_K):
            row_copy(r, idx_smem[k * ROW_TILE + r]).start(priority=k)
        return carry

    lax.fori_loop(0, ROW_TILE, issue, 0, unroll=8)

    def drain(r, carry):
        for k in range(TOP_K):
            row_copy(0, 0).wait()
        return carry

    lax.fori_loop(0, ROW_TILE, drain, 0, unroll=8)


def _dispatch(p_ends, padded, n_used, idx_tiles, h2, dst_rows):
    t, d = h2.shape
    any_spec = pl.BlockSpec(memory_space=pl.ANY)
    grid_spec = pltpu.PrefetchScalarGridSpec(
        num_scalar_prefetch=3,
        grid=(t // ROW_TILE,),
        in_specs=[any_spec, pl.BlockSpec((ROW_TILE, d), lambda i, pe, pd, nu: (i, 0))],
        out_specs=any_spec,
        scratch_shapes=[pltpu.SMEM((IDX_CHUNK,), I32), pltpu.VMEM((MOE_BLOCK, d), h2.dtype),
                        pltpu.SemaphoreType.DMA, pltpu.SemaphoreType.DMA, pltpu.SemaphoreType.DMA])
    return pl.pallas_call(
        _dispatch_kernel,
        grid_spec=grid_spec,
        out_shape=jax.ShapeDtypeStruct((dst_rows, d), h2.dtype),
        compiler_params=_cparams("arbitrary"),
        name="moe_dispatch",
    )(p_ends, padded, n_used, idx_tiles, h2)


def _expert_kernel(be_ref, nu_ref, x_ref, wg_ref, wu_ref, wd_ref, y_ref, wg_b, wu_b, wd_b):
    b = pl.program_id(0)

    @pl.when(b < nu_ref[0])
    def _():
        @pl.when(jnp.logical_or(b == 0, be_ref[b] != be_ref[jnp.maximum(b - 1, 0)]))
        def _():
            wg_b[...] = wg_ref[0].astype(BF16)
            wu_b[...] = wu_ref[0].astype(BF16)
            wd_b[...] = wd_ref[0].astype(BF16)

        x = jnp.concatenate(_unpack_bf16_pairs(x_ref[...]), axis=1).astype(BF16)
        a = _silu(_dot(x, wg_b[...])) * _dot(x, wu_b[...])
        y_ref[...] = _pack_bf16_pairs(_dot(a.astype(BF16), wd_b[...]))


def _expert_mlp(block_e, n_used, xb, w_gate, w_up, w_down):
    p, words = xb.shape
    d, de = w_gate.shape[1:]
    blk = lambda b, be, nu: (jnp.minimum(b, nu[0] - 1), 0)
    grid_spec = pltpu.PrefetchScalarGridSpec(
        num_scalar_prefetch=2,
        grid=(p // MOE_BLOCK,),
        in_specs=[pl.BlockSpec((MOE_BLOCK, words), blk),
                  pl.BlockSpec((1, d, de), lambda b, be, nu: (be[b], 0, 0)),
                  pl.BlockSpec((1, d, de), lambda b, be, nu: (be[b], 0, 0)),
                  pl.BlockSpec((1, de, d), lambda b, be, nu: (be[b], 0, 0))],
        out_specs=pl.BlockSpec((MOE_BLOCK, words), blk),
        scratch_shapes=[pltpu.VMEM((d, de), BF16), pltpu.VMEM((d, de), BF16), pltpu.VMEM((de, d), BF16)])
    return pl.pallas_call(
        _expert_kernel,
        grid_spec=grid_spec,
        out_shape=jax.ShapeDtypeStruct((p, words), jnp.uint32),
        compiler_params=_cparams("arbitrary"),
        name="expert_mlp",
    )(block_e, n_used, xb, w_gate, w_up, w_down)


def _combine_kernel(idx_hbm, yb_hbm, x_ref, gcol_ref, mod_ref, wnf_ref, o_ref, ybuf, idx_smem, idx_sem, row_sems,
                    *, final):
    i = pl.program_id(0)

    def row_copy(src_row, slot, k, r):
        return pltpu.make_async_copy(yb_hbm.at[pl.ds(src_row, 1)], ybuf.at[slot, k, pl.ds(r, 1)],
                                     row_sems.at[slot])

    def issue_tile(tile):
        _load_tile_indices(idx_hbm, idx_smem, idx_sem, tile)
        slot = tile % 2

        def issue(r, carry):
            for k in range(TOP_K):
                row_copy(idx_smem[k * ROW_TILE + r], slot, k, r).start(priority=k)
            return carry

        lax.fori_loop(0, ROW_TILE, issue, 0, unroll=8)

    @pl.when(i == 0)
    def _():
        issue_tile(i)

    @pl.when(i + 1 < pl.num_programs(0))
    def _():
        issue_tile(i + 1)

    slot = i % 2

    def drain(r, carry):
        for k in range(TOP_K):
            row_copy(0, slot, k, 0).wait()
        return carry

    lax.fori_loop(0, ROW_TILE, drain, 0, unroll=8)

    gc = gcol_ref[...]
    y0 = _unpack_bf16_pairs(ybuf[slot, 0])
    y1 = _unpack_bf16_pairs(ybuf[slot, 1])
    moe = jnp.concatenate([gc[:, 0:1] * y0[0] + gc[:, 1:2] * y1[0], gc[:, 0:1] * y0[1] + gc[:, 1:2] * y1[1]],
                          axis=1)
    x2 = x_ref[...] + mod_ref[0][5:6, :] * moe
    o_ref[...] = _rms_norm(x2, wnf_ref[...]) if final else x2


def _combine(idx_tiles, yb, x2, gcol, mod, w_norm_final, seq, final):
    t, d = x2.shape
    tm = ROW_TILE
    spb = seq // tm
    any_spec = pl.BlockSpec(memory_space=pl.ANY)
    return pl.pallas_call(
        functools.partial(_combine_kernel, final=final),
        grid=(t // tm,),
        in_specs=[any_spec, any_spec,
                  pl.BlockSpec((tm, d), lambda i: (i, 0)),
                  pl.BlockSpec((tm, LANES), lambda i: (i, 0)),
                  pl.BlockSpec((1, 6, d), lambda i: (i // spb, 0, 0)),
                  pl.BlockSpec((1, d), lambda i: (0, 0))],
        out_specs=pl.BlockSpec((tm, d), lambda i: (i, 0)),
        out_shape=jax.ShapeDtypeStruct((t, d), F32),
        scratch_shapes=[pltpu.VMEM((2, TOP_K, tm, d // 2), jnp.uint32), pltpu.SMEM((IDX_CHUNK,), I32),
                        pltpu.SemaphoreType.DMA, pltpu.SemaphoreType.DMA((2,))],
        compiler_params=_cparams("arbitrary"),
        name="moe_combine",
    )(idx_tiles, yb, x2, gcol, mod, w_norm_final)


def _lane_row(pieces, width=LANES):
    row = jnp.zeros((width,), F32)
    for off, vec in pieces.items():
        row = row.at[off:off + vec.shape[0]].set(vec.astype(F32))
    return row.reshape(1, width)


def _split_w_in(w_in):
    gw, nh = GROUP_WIDTH, N_HEADS
    widths = [gw, gw, gw, nh, gw, gw, gw, 3 * gw, nh, gw, gw, gw, gw, nh, nh]
    cuts, acc = [], 0
    for w in widths[:-1]:
        acc += w
        cuts.append(acc)
    (aq, ak, av, af, su, sv, mz, mxbc, mdt, lq, lk, lv, lo, li, lf) = jnp.split(w_in, cuts, axis=1)
    scale = HEAD_DIM ** -0.5
    wmain = jnp.concatenate([ak, su, sv, mz, mxbc, lq, lk * scale, lv, lo], axis=1).astype(BF16)
    wqt = (aq * scale).T.astype(BF16)
    wvt = av.T.astype(BF16)
    wg = jnp.zeros((w_in.shape[0], LANES), F32)
    for off, w in ((GATE_AF, af), (GATE_DT, mdt), (GATE_LI, li), (GATE_LF, lf)):
        wg = wg.at[:, off:off + nh].set(w)
    return wmain, wqt, wvt, wg.astype(BF16)


def _moe_layer(x1, h2, eids, gcol, mod_l, w_gate, w_up, w_down, layer, w_norm_final, seq, final):
    t, d = x1.shape
    ranks, counts = _expert_ranks(eids)
    counts = counts[:, 0].astype(I32)
    padded = ((counts + MOE_BLOCK - 1) // MOE_BLOCK) * MOE_BLOCK
    p_ends = jnp.cumsum(padded)
    p_starts = (p_ends - padded).astype(I32)
    n_blocks = (t * TOP_K) // MOE_BLOCK + N_EXPERTS
    blocks = jnp.arange(n_blocks, dtype=I32)
    block_e = jnp.sum((p_ends[None, :] <= (blocks * MOE_BLOCK)[:, None]).astype(I32), axis=1)
    block_e = jnp.minimum(block_e, N_EXPERTS - 1)
    n_used = (p_ends[-1:] // MOE_BLOCK).astype(I32)
    block_e = jnp.where(blocks < n_used, block_e, block_e[n_used[0] - 1])
    dest = _dest_rows(p_starts, eids, ranks)
    idx_tiles = dest.reshape(TOP_K, t // ROW_TILE, ROW_TILE).transpose(1, 0, 2).reshape(-1)
    xb = _dispatch(p_ends.astype(I32), padded.astype(I32), n_used, idx_tiles, h2, n_blocks * MOE_BLOCK)
    yb = _expert_mlp(block_e + layer * N_EXPERTS, n_used, xb, w_gate, w_up, w_down)
    return _combine(idx_tiles, yb, x1, gcol, mod_l, w_norm_final, seq, final)


def kernel(x, c, w_in, w_out, w_mix_norm, attn_f_bias, sg_w, sg_b, ssm_conv_w, ssm_conv_b, ssm_dt_bias,
           ssm_a_log, ssm_d, mlstm_i_bias, mlstm_f_bias, w_ada, b_ada, w_norm1, w_norm2, w_router_group,
           b_router_group, w_router_expert, b_router_expert, w_expert_gate, w_expert_up, w_expert_down,
           w_norm_final):
    bsz, seq, d = x.shape
    depth = w_in.shape[0]
    gw = GROUP_WIDTH
    mod = _ada_modulation(c, w_ada, b_ada).reshape(depth, bsz, 6, d)
    x2 = x.reshape(bsz * seq, d)
    wnf = w_norm_final.reshape(1, d)
    w_eg = w_expert_gate.reshape((depth * N_EXPERTS,) + w_expert_gate.shape[2:])
    w_eu = w_expert_up.reshape((depth * N_EXPERTS,) + w_expert_up.shape[2:])
    w_ed = w_expert_down.reshape((depth * N_EXPERTS,) + w_expert_down.shape[2:])
    for l in range(depth):
        wmain, wqt, wvt, wg = _split_w_in(w_in[l])
        gains = w_mix_norm[l].reshape(N_HEADS, 1, gw)
        proj, qt, vt, gates = _inproj(x2, mod[l], w_norm1[l].reshape(1, d), wmain, wqt, wvt, wg, bsz, seq)
        f2 = _forget_cumsum(gates, _lane_row({GATE_AF: attn_f_bias[l]}), bsz, seq)
        y_attn = _attention(proj, qt, vt, f2, gains[0], bsz, seq)
        sg_bias_full = jnp.repeat(sg_b[l].T, HEAD_DIM, axis=1)
        y_sg = _spatial_gating(proj, gains[1], sg_w[l], sg_bias_full)
        y_ssm = _ssd_mixer(proj, gates, ssm_conv_w[l], ssm_conv_b[l].reshape(1, -1),
                           _lane_row({GATE_DT: ssm_dt_bias[l]}), _lane_row({GATE_DT: ssm_a_log[l]}),
                           jnp.repeat(ssm_d[l], HEAD_DIM).reshape(1, gw), gains[2], bsz, seq)
        y_ml = _mlstm_mixer(proj, gates, _lane_row({GATE_LI: mlstm_i_bias[l], GATE_LF: mlstm_f_bias[l]}),
                            gains[3], bsz, seq)
        w_route = jnp.zeros((d, LANES), F32)
        w_route = w_route.at[:, ROUTE_G:ROUTE_G + N_EXPERT_GROUPS].set(w_router_group[l])
        w_route = w_route.at[:, ROUTE_E:ROUTE_E + N_EXPERTS].set(w_router_expert[l]).astype(BF16)
        b_route = _lane_row({ROUTE_G: b_router_group[l], ROUTE_E: b_router_expert[l]})
        x1, h2, eids, gcol = _outproj_router(x2, (y_attn, y_sg, y_ssm, y_ml), w_out[l].astype(BF16), mod[l],
                                             w_norm2[l].reshape(1, d), w_route, b_route, seq)
        x2 = _moe_layer(x1, h2, eids, gcol, mod[l], w_eg, w_eu, w_ed, l, wnf, seq, final=(l == depth - 1))
    return x2.reshape(bsz, seq, d)
```

```python
import functools

import jax
import jax.numpy as jnp
from jax import lax
from jax.experimental import pallas as pl
from jax.experimental.pallas import tpu as pltpu

F32 = jnp.float32
BF16 = jnp.bfloat16
I32 = jnp.int32

LANES = 128
SUBLANES = 8
HEAD_DIM = 64
N_HEADS = 4
GROUP_WIDTH = 256
CHUNK = 128
MIX_TILE = 4 * CHUNK
SSM_GROUPS = 2
CONV_WIDTH = 4
N_EXPERT_GROUPS = 4
EXPERTS_PER_GROUP = 8
N_EXPERTS = N_EXPERT_GROUPS * EXPERTS_PER_GROUP
TOP_K = 2
MOE_BLOCK = 256
NORM_EPS = 1e-6
MLSTM_M_INIT = -1e30
NEG_INF = float("-inf")
VMEM_LIMIT_BYTES = 48 * 1024 * 1024
IDX_CHUNK = 1024

(PB_K, PB_SU, PB_SV, PB_MZ, PB_X, PB_B, PB_C, PB_LQ, PB_LK, PB_LV, PB_LO) = range(11)
PROJ_COLS = 11 * GROUP_WIDTH
LOG2_E = 1.4426950408889634
GATE_AF, GATE_DT, GATE_LI, GATE_LF = 0, 4, 8, 12
ROUTE_G, ROUTE_E = 0, 8

NT_DIMS = (((1,), (1,)), ((), ()))


def _cparams(*sem):
    return pltpu.CompilerParams(dimension_semantics=sem, vmem_limit_bytes=VMEM_LIMIT_BYTES)


def _dot(a, b):
    return jnp.dot(a, b, preferred_element_type=F32)


def _dot_nt(a, b):
    return lax.dot_general(a, b, NT_DIMS, preferred_element_type=F32)


def _dot_exact(a, b):
    return jnp.dot(a, b, preferred_element_type=F32, precision=lax.Precision.HIGHEST)


def _head_of_lane(shape, axis=1):
    return lax.broadcasted_iota(I32, shape, axis) // HEAD_DIM


def _sigmoid(x):
    return 1.0 / (1.0 + jnp.exp(-x))


def _silu(x):
    return x * _sigmoid(x)


def _log_sigmoid(x):
    return jnp.minimum(x, 0.0) - jnp.log1p(jnp.exp(-jnp.abs(x)))


def _softplus(x):
    return jnp.maximum(x, 0.0) + jnp.log1p(jnp.exp(-jnp.abs(x)))


def _gelu_tanh(x):
    return 0.5 * x * (1.0 + jnp.tanh(0.7978845608028654 * (x + 0.044715 * (x * x * x))))


def _expand_heads(cols, width=GROUP_WIDTH):
    rows = cols[0].shape[0]
    head = _head_of_lane((rows, width))
    out = jnp.broadcast_to(cols[N_HEADS - 1], (rows, width))
    for h in range(N_HEADS - 2, -1, -1):
        out = jnp.where(head == h, jnp.broadcast_to(cols[h], (rows, width)), out)
    return out


def _mask_head(x, h):
    return jnp.where(_head_of_lane(x.shape) == h, x, jnp.zeros_like(x))


def _head_rms_norm(y, gain):
    head = _head_of_lane(y.shape)
    sq = y * y
    cols = [jnp.sum(jnp.where(head == h, sq, 0.0), axis=1, keepdims=True) * (1.0 / HEAD_DIM)
            for h in range(N_HEADS)]
    return y * lax.rsqrt(_expand_heads(cols) + NORM_EPS) * gain


def _rms_norm(x, gain):
    ms = jnp.mean(x * x, axis=1, keepdims=True)
    return x * lax.rsqrt(ms + NORM_EPS) * gain


def _pack_bf16_pairs(x):
    half = x.shape[1] // 2
    bits = lax.bitcast_convert_type(x.astype(BF16).astype(F32), jnp.uint32)
    return bits[:, :half] | lax.shift_right_logical(bits[:, half:], jnp.uint32(16))


def _unpack_bf16_pairs(words):
    first = lax.bitcast_convert_type(words & jnp.uint32(0xFFFF0000), F32)
    second = lax.bitcast_convert_type(lax.shift_left(words, jnp.uint32(16)), F32)
    return first, second


def _tril(n, strict=False):
    r = lax.broadcasted_iota(I32, (n, n), 0)
    c = lax.broadcasted_iota(I32, (n, n), 1)
    return (r > c) if strict else (r >= c)


def _ada_kernel(c_ref, w_ref, b_ref, o_ref):
    o_ref[0] = _dot_exact(_silu(c_ref[...]), w_ref[0]) + b_ref[0]


def _ada_modulation(c, w_ada, b_ada):
    depth, d, d6 = w_ada.shape
    bsz = c.shape[0]
    return pl.pallas_call(
        _ada_kernel,
        grid=(depth, d6 // d),
        in_specs=[pl.BlockSpec((bsz, d), lambda l, j: (0, 0)),
                  pl.BlockSpec((1, d, d), lambda l, j: (l, 0, j)),
                  pl.BlockSpec((1, 1, d), lambda l, j: (l, 0, j))],
        out_specs=pl.BlockSpec((1, bsz, d), lambda l, j: (l, 0, j)),
        out_shape=jax.ShapeDtypeStruct((depth, bsz, d6), F32),
        compiler_params=_cparams("arbitrary", "arbitrary"),
        name="ada_modulation",
    )(c, w_ada, b_ada.reshape(depth, 1, d6))


def _inproj_kernel(x_ref, mod_ref, wn_ref, wmain_ref, wqt_ref, wvt_ref, wg_ref,
                   proj_ref, qt_ref, vt_ref, gates_ref, h_scr):
    mod = mod_ref[0]
    h = _rms_norm(x_ref[...], wn_ref[...]) * (1.0 + mod[1:2, :]) + mod[0:1, :]
    h_scr[...] = h.astype(BF16)
    for c0 in range(0, PROJ_COLS, GROUP_WIDTH):
        proj_ref[:, c0:c0 + GROUP_WIDTH] = _dot(h_scr[...], wmain_ref[:, c0:c0 + GROUP_WIDTH]).astype(BF16)
    qt_ref[0] = (_dot_nt(wqt_ref[...], h_scr[...]) * LOG2_E).astype(BF16)
    vt_ref[0] = _dot_nt(wvt_ref[...], h_scr[...]).astype(BF16)
    gates_ref[...] = _dot(h_scr[...], wg_ref[...])


def _inproj(x2, mod, w_norm, wmain, wqt, wvt, wg, bsz, seq, tm=512):
    t, d = x2.shape
    spb = seq // tm
    tspec = pl.BlockSpec((1, GROUP_WIDTH, tm), lambda i: (i // spb, 0, i % spb))
    return pl.pallas_call(
        _inproj_kernel,
        grid=(t // tm,),
        in_specs=[pl.BlockSpec((tm, d), lambda i: (i, 0)),
                  pl.BlockSpec((1, 6, d), lambda i: (i // spb, 0, 0)),
                  pl.BlockSpec((1, d), lambda i: (0, 0)),
                  pl.BlockSpec((d, PROJ_COLS), lambda i: (0, 0)),
                  pl.BlockSpec((GROUP_WIDTH, d), lambda i: (0, 0)),
                  pl.BlockSpec((GROUP_WIDTH, d), lambda i: (0, 0)),
                  pl.BlockSpec((d, LANES), lambda i: (0, 0))],
        out_specs=[pl.BlockSpec((tm, PROJ_COLS), lambda i: (i, 0)), tspec, tspec,
                   pl.BlockSpec((tm, LANES), lambda i: (i, 0))],
        out_shape=[jax.ShapeDtypeStruct((t, PROJ_COLS), BF16),
                   jax.ShapeDtypeStruct((bsz, GROUP_WIDTH, seq), BF16),
                   jax.ShapeDtypeStruct((bsz, GROUP_WIDTH, seq), BF16),
                   jax.ShapeDtypeStruct((t, LANES), F32)],
        scratch_shapes=[pltpu.VMEM((tm, d), BF16)],
        compiler_params=_cparams("arbitrary"),
        name="norm_inproj",
    )(x2, mod, w_norm, wmain, wqt, wvt, wg)


def _fcum_kernel(g_ref, b_ref, f_ref, carry):
    @pl.when(pl.program_id(1) == 0)
    def _():
        carry[...] = jnp.zeros_like(carry)

    tb = g_ref.shape[0]
    ls = _log_sigmoid(g_ref[...] + b_ref[...])
    cum = _dot_exact(_tril(tb).astype(F32), ls) + carry[...]
    carry[...] = cum[tb - 1:tb, :]
    f2 = cum * LOG2_E
    hi = f2.astype(BF16)
    rest = f2 - hi.astype(F32)
    mid = rest.astype(BF16)
    lo = (rest - mid.astype(F32)).astype(BF16)
    r = lax.broadcasted_iota(I32, (LANES, GROUP_WIDTH), 0) - GATE_AF
    c = lax.broadcasted_iota(I32, (LANES, GROUP_WIDTH), 1)
    slab = jnp.zeros((tb, GROUP_WIDTH), F32)
    for j, piece in enumerate((hi, mid, lo)):
        place = (r >= 0) & (r < N_HEADS) & (c == ((r + 1) % N_HEADS) * HEAD_DIM + j)
        slab = slab + _dot(piece, jnp.where(place, -1.0, 0.0).astype(BF16))
    f_ref[...] = slab.astype(BF16)


def _forget_cumsum(gates, f_bias_row, bsz, seq, tb=512):
    spb = seq // tb
    return pl.pallas_call(
        _fcum_kernel,
        grid=(bsz, spb),
        in_specs=[pl.BlockSpec((tb, LANES), lambda b, j: (b * spb + j, 0)),
                  pl.BlockSpec((1, LANES), lambda b, j: (0, 0))],
        out_specs=pl.BlockSpec((tb, GROUP_WIDTH), lambda b, j: (b * spb + j, 0)),
        out_shape=jax.ShapeDtypeStruct((bsz * seq, GROUP_WIDTH), BF16),
        scratch_shapes=[pltpu.VMEM((1, LANES), F32)],
        compiler_params=_cparams("arbitrary", "arbitrary"),
        name="forget_cumsum",
    )(gates, f_bias_row)


ONES_ROWS = 16


def _attn_kernel(qi_ref, kj_ref, k_ref, qt_ref, vt_ref, f_ref, gain_ref, o_ref, qtm, m_s, l_s, acc):
    p = pl.program_id(1)
    qi, kj = qi_ref[p], kj_ref[p]
    tk = k_ref.shape[0]
    tq = qt_ref.shape[2]

    @pl.when(kj == 0)
    def _():
        qt = qt_ref[0]
        row = lax.broadcasted_iota(I32, qt.shape, 0)
        for h in range(N_HEADS):
            slot = ((h + 1) % N_HEADS) * HEAD_DIM
            ones_rows = jnp.where((row >= slot) & (row < slot + 3), 1.0, 0.0).astype(BF16)
            qtm[h] = jnp.where(row // HEAD_DIM == h, qt, ones_rows)
        m_s[...] = jnp.full_like(m_s, NEG_INF)
        l_s[...] = jnp.zeros_like(l_s)
        acc[...] = jnp.zeros_like(acc)

    def step(diagonal):
        k = k_ref[...]
        bias = f_ref[...]
        head = _head_of_lane(k.shape)
        ones = jnp.ones((ONES_ROWS, tk), BF16)
        if diagonal:
            visible = lax.broadcasted_iota(I32, (tk, tq), 0) <= lax.broadcasted_iota(I32, (tk, tq), 1)
        scores, m_news, alphas, probs = {}, {}, {}, {}

        def score(h):
            s = _dot(jnp.where(head == h, k, bias), qtm[h])
            if diagonal:
                s = jnp.where(visible, s, NEG_INF)
            scores[h] = s
            m_prev = m_s[h]
            m_news[h] = jnp.maximum(m_prev, jnp.max(s, axis=0, keepdims=True))
            alphas[h] = jnp.exp2(m_prev - m_news[h])
            m_s[h] = m_news[h]

        def prob(h):
            probs[h] = jnp.exp2(scores[h] - m_news[h]).astype(BF16)

        def weighted_sum(h):
            vt_ext = jnp.concatenate([vt_ref[0, h * HEAD_DIM:(h + 1) * HEAD_DIM, :], ones], axis=0)
            pv = _dot(vt_ext, probs[h])
            acc[h] = alphas[h] * acc[h] + pv[0:HEAD_DIM, :]
            l_s[h] = alphas[h] * l_s[h] + pv[HEAD_DIM:HEAD_DIM + 1, :]

        for stage in range(N_HEADS + 2):
            if stage < N_HEADS:
                score(stage)
            if 0 <= stage - 1 < N_HEADS:
                prob(stage - 1)
            if 0 <= stage - 2 < N_HEADS:
                weighted_sum(stage - 2)

    @pl.when(kj < qi)
    def _():
        step(False)

    @pl.when(kj == qi)
    def _():
        step(True)
        yt = jnp.concatenate([acc[h] / l_s[h] for h in range(N_HEADS)], axis=0)
        o_ref[...] = _head_rms_norm(yt.T, gain_ref[...]).astype(BF16)


def _attention(proj, qt, vt, f2, gain, bsz, seq, tq=1024):
    nq = seq // tq
    pairs = [(i, j) for i in range(nq) for j in range(i + 1)]
    qi = jnp.asarray([a for a, _ in pairs], I32)
    kj = jnp.asarray([b for _, b in pairs], I32)
    t = bsz * seq
    grid_spec = pltpu.PrefetchScalarGridSpec(
        num_scalar_prefetch=2,
        grid=(bsz, len(pairs)),
        in_specs=[pl.BlockSpec((tq, GROUP_WIDTH), lambda b, p, qi, kj: (b * nq + kj[p], PB_K)),
                  pl.BlockSpec((1, GROUP_WIDTH, tq), lambda b, p, qi, kj: (b, 0, qi[p])),
                  pl.BlockSpec((1, GROUP_WIDTH, tq), lambda b, p, qi, kj: (b, 0, kj[p])),
                  pl.BlockSpec((tq, GROUP_WIDTH), lambda b, p, qi, kj: (b * nq + kj[p], 0)),
                  pl.BlockSpec((1, GROUP_WIDTH), lambda b, p, qi, kj: (0, 0))],
        out_specs=pl.BlockSpec((tq, GROUP_WIDTH), lambda b, p, qi, kj: (b * nq + qi[p], 0)),
        scratch_shapes=[pltpu.VMEM((N_HEADS, GROUP_WIDTH, tq), BF16),
                        pltpu.VMEM((N_HEADS, 1, tq), F32),
                        pltpu.VMEM((N_HEADS, 1, tq), F32),
                        pltpu.VMEM((N_HEADS, HEAD_DIM, tq), F32)])
    return pl.pallas_call(
        _attn_kernel,
        grid_spec=grid_spec,
        out_shape=jax.ShapeDtypeStruct((t, GROUP_WIDTH), BF16),
        compiler_params=_cparams("arbitrary", "arbitrary"),
        name="fox_attention",
    )(qi, kj, proj, qt, vt, f2, gain)


def _sg_kernel(u_ref, v_ref, gain_ref, w_ref, b_ref, o_ref):
    tm = u_ref.shape[0]
    u = _gelu_tanh(u_ref[...].astype(F32))
    v = _head_rms_norm(_gelu_tanh(v_ref[...].astype(F32)), gain_ref[...])
    causal = _tril(CHUNK)
    ws = [jnp.where(causal, w_ref[h], 0.0).astype(BF16) for h in range(N_HEADS)]
    for c0 in range(0, tm, CHUNK):
        vc = v[c0:c0 + CHUNK, :].astype(BF16)
        mixed = b_ref[...]
        for h in range(N_HEADS):
            mixed = mixed + _dot(ws[h], _mask_head(vc, h))
        o_ref[c0:c0 + CHUNK, :] = (u[c0:c0 + CHUNK, :] * mixed).astype(BF16)


def _spatial_gating(proj, gain, sg_w, sg_bias_full, tm=512):
    t = proj.shape[0]
    return pl.pallas_call(
        _sg_kernel,
        grid=(t // tm,),
        in_specs=[pl.BlockSpec((tm, GROUP_WIDTH), lambda i: (i, PB_SU)),
                  pl.BlockSpec((tm, GROUP_WIDTH), lambda i: (i, PB_SV)),
                  pl.BlockSpec((1, GROUP_WIDTH), lambda i: (0, 0)),
                  pl.BlockSpec((N_HEADS, CHUNK, CHUNK), lambda i: (0, 0, 0)),
                  pl.BlockSpec((CHUNK, GROUP_WIDTH), lambda i: (0, 0))],
        out_specs=pl.BlockSpec((tm, GROUP_WIDTH), lambda i: (i, 0)),
        out_shape=jax.ShapeDtypeStruct((t, GROUP_WIDTH), BF16),
        compiler_params=_cparams("arbitrary"),
        name="spatial_gating",
    )(proj, proj, gain, sg_w, sg_bias_full)


def _ssd_kernel(z_ref, x_ref, b_ref, c_ref, g_ref, cw_ref, cb_ref, dtb_ref, alog_ref, dskip_ref, gain_ref,
                o_ref, conv_scr, xbc_scr, state):
    L = CHUNK
    W = GROUP_WIDTH
    tile = x_ref.shape[0]

    @pl.when(pl.program_id(1) == 0)
    def _():
        conv_scr[0:SUBLANES, :] = jnp.zeros((SUBLANES, 3 * W), F32)
        state[...] = jnp.zeros_like(state)

    conv_scr[SUBLANES:, 0:W] = x_ref[...].astype(F32)
    conv_scr[SUBLANES:, W:2 * W] = b_ref[...].astype(F32)
    conv_scr[SUBLANES:, 2 * W:] = c_ref[...].astype(F32)
    cw = cw_ref[...]
    conv = cb_ref[...] + cw[CONV_WIDTH - 1:CONV_WIDTH, :] * conv_scr[SUBLANES:, :]
    for s in range(1, CONV_WIDTH):
        conv = conv + cw[CONV_WIDTH - 1 - s:CONV_WIDTH - s, :] * conv_scr[SUBLANES - s:SUBLANES - s + tile, :]
    conv_scr[0:SUBLANES, :] = conv_scr[tile:tile + SUBLANES, :]
    xbc_scr[...] = _silu(conv)

    neg_a = -jnp.exp(alog_ref[...])
    for c0 in range(0, tile, L):
        rows = slice(c0, c0 + L)
        dt = _softplus(g_ref[rows, :] + dtb_ref[...])
        y = _ssd_chunk(xbc_scr[rows, 0:W], xbc_scr[rows, W:2 * W], xbc_scr[rows, 2 * W:], dt, dt * neg_a,
                       dskip_ref[...], state)
        y = y * _silu(z_ref[rows, :].astype(F32))
        o_ref[rows, :] = _head_rms_norm(y, gain_ref[...]).astype(BF16)


def _ssd_chunk(xs, bm, cm, dt, da, dskip, state):
    L = CHUNK
    a_cum = _dot_exact(_tril(L).astype(F32), da)
    a_row = a_cum.T
    dt_cols = [dt[:, GATE_DT + h:GATE_DT + h + 1] for h in range(N_HEADS)]
    a_cols = [a_cum[:, GATE_DT + h:GATE_DT + h + 1] for h in range(N_HEADS)]
    a_end = [a_cum[L - 1:L, GATE_DT + h:GATE_DT + h + 1] for h in range(N_HEADS)]
    xdt = xs * _expand_heads(dt_cols)
    xdt_b = xdt.astype(BF16)
    xw = (xdt * _expand_heads([jnp.exp(a_end[h] - a_cols[h]) for h in range(N_HEADS)])).astype(BF16)
    exp_a = _expand_heads([jnp.exp(a_cols[h]) for h in range(N_HEADS)])
    causal = _tril(L)
    half = lax.broadcasted_iota(I32, (1, LANES), 1) < HEAD_DIM

    y = dskip * xs
    y_off = []
    for g in range(SSM_GROUPS):
        bg = bm[:, g * LANES:(g + 1) * LANES]
        cg = cm[:, g * LANES:(g + 1) * LANES].astype(BF16)
        scores = _dot_nt(cg, bg.astype(BF16))
        for r in range(N_HEADS // SSM_GROUPS):
            h = g * (N_HEADS // SSM_GROUPS) + r
            seg = a_cols[h] - a_row[GATE_DT + h:GATE_DT + h + 1, :]
            decay = jnp.exp(jnp.where(causal, seg, NEG_INF))
            y = y + _dot((scores * decay).astype(BF16), _mask_head(xdt_b, h))
        st_in = state[g]
        y_off.append(_dot(cg, st_in.astype(BF16)))
        chunk_decay = jnp.where(half, jnp.exp(a_end[2 * g]), jnp.exp(a_end[2 * g + 1]))
        state[g] = chunk_decay * st_in + _dot(bg.T.astype(BF16), xw[:, g * LANES:(g + 1) * LANES])
    return y + jnp.concatenate(y_off, axis=1) * exp_a


def _ssd_mixer(proj, gates, conv_w, conv_b, dtb_row, alog_row, dskip_row, gain, bsz, seq, tile=MIX_TILE):
    t = proj.shape[0]
    nc = seq // tile
    row = lambda blk: pl.BlockSpec((tile, GROUP_WIDTH), lambda b, j, blk=blk: (b * nc + j, blk))
    const = lambda shape: pl.BlockSpec(shape, lambda b, j: (0,) * len(shape))
    return pl.pallas_call(
        _ssd_kernel,
        grid=(bsz, nc),
        in_specs=[row(PB_MZ), row(PB_X), row(PB_B), row(PB_C),
                  pl.BlockSpec((tile, LANES), lambda b, j: (b * nc + j, 0)),
                  const((CONV_WIDTH, 3 * GROUP_WIDTH)), const((1, 3 * GROUP_WIDTH)),
                  const((1, LANES)), const((1, LANES)), const((1, GROUP_WIDTH)), const((1, GROUP_WIDTH))],
        out_specs=pl.BlockSpec((tile, GROUP_WIDTH), lambda b, j: (b * nc + j, 0)),
        out_shape=jax.ShapeDtypeStruct((t, GROUP_WIDTH), BF16),
        scratch_shapes=[pltpu.VMEM((tile + SUBLANES, 3 * GROUP_WIDTH), F32),
                        pltpu.VMEM((tile, 3 * GROUP_WIDTH), F32),
                        pltpu.VMEM((SSM_GROUPS, LANES, LANES), F32)],
        compiler_params=_cparams("arbitrary", "arbitrary"),
        name="ssd_mixer",
    )(proj, proj, proj, proj, gates, conv_w, conv_b, dtb_row, alog_row, dskip_row, gain)


def _mlstm_kernel(q_ref, k_ref, v_ref, o_gate_ref, g_ref, bias_ref, gain_ref, o_ref, ct, nb, m_row):
    L = CHUNK
    W = GROUP_WIDTH

    @pl.when(pl.program_id(1) == 0)
    def _():
        ct[...] = jnp.zeros_like(ct)
        nb[...] = jnp.zeros_like(nb)
        m_row[...] = jnp.full_like(m_row, MLSTM_M_INIT)

    def chunk(c, carry):
        rows = pl.ds(pl.multiple_of(c * L, L), L)
        y = _mlstm_chunk(q_ref[rows, :], k_ref[rows, :], v_ref[rows, :], g_ref[rows, :] + bias_ref[...],
                         ct, nb, m_row)
        y = _sigmoid(o_gate_ref[rows, :].astype(F32)) * _head_rms_norm(y, gain_ref[...])
        o_ref[rows, :] = y.astype(BF16)
        return carry

    lax.fori_loop(0, q_ref.shape[0] // L, chunk, 0)


def _mlstm_chunk(q, k, v, gate, ct, nb, m_row):
    L = CHUNK
    W = GROUP_WIDTH
    a_full = _dot_exact(_tril(L).astype(F32), _log_sigmoid(gate))
    a_rows = a_full.T
    g_rows = gate.T
    causal = _tril(L)
    lane = lax.broadcasted_iota(I32, (1, LANES), 1)

    inter_q = _dot(q, ct[...].astype(BF16))
    n_q = _dot(q, nb[...].astype(BF16))
    m_old = m_row[...]
    num = jnp.zeros((L, W), F32)
    inter_cols, den_cols, ws_cols, scale_cols = [], [], [], []
    m_next = m_old
    for h in range(N_HEADS):
        a_col = a_full[:, GATE_LF + h:GATE_LF + h + 1]
        i_col = gate[:, GATE_LI + h:GATE_LI + h + 1]
        a_r = a_rows[GATE_LF + h:GATE_LF + h + 1, :]
        i_r = g_rows[GATE_LI + h:GATE_LI + h + 1, :]
        a_end = a_full[L - 1:L, GATE_LF + h:GATE_LF + h + 1]
        m_in = m_old[:, h:h + 1]
        log_d = jnp.where(causal, a_col - a_r + i_r, NEG_INF)
        log_inter = a_col + m_in
        m_t = jnp.maximum(jnp.max(log_d, axis=1, keepdims=True), log_inter)
        w = _dot_nt(_mask_head(q, h), k) * jnp.exp(log_d - m_t)
        inter = jnp.exp(log_inter - m_t)
        num = num + _dot(w.astype(BF16), _mask_head(v, h))
        den = jnp.sum(w, axis=1, keepdims=True) + inter * n_q[:, h:h + 1]
        inter_cols.append(inter)
        den_cols.append(jnp.maximum(jnp.abs(den), jnp.exp(-m_t)))
        g_col = a_end - a_col + i_col
        m_new = jnp.maximum(a_end + m_in, jnp.max(g_col, axis=0, keepdims=True))
        ws_cols.append(jnp.exp(g_col - m_new))
        scale_cols.append(jnp.exp(a_end + m_in - m_new))
        m_next = jnp.where(lane == h, m_new, m_next)
    hout = (num + _expand_heads(inter_cols) * inter_q) / _expand_heads(den_cols)

    kw_t = (k.astype(F32) * _expand_heads(ws_cols)).T.astype(BF16)
    scale_row = _expand_heads(scale_cols)
    same_head = _head_of_lane((W, W), 0) == _head_of_lane((W, W), 1)
    ct[...] = scale_row * ct[...] + jnp.where(same_head, _dot(kw_t, v), 0.0)
    col_is_head = _head_of_lane((W, LANES), 0) == lax.broadcasted_iota(I32, (W, LANES), 1)
    scale_n = scale_cols[N_HEADS - 1]
    for h in range(N_HEADS - 2, -1, -1):
        scale_n = jnp.where(lane == h, scale_cols[h], scale_n)
    nb[...] = scale_n * nb[...] + jnp.where(col_is_head, _dot(kw_t, jnp.ones((L, LANES), BF16)), 0.0)
    m_row[...] = m_next
    return hout


def _mlstm_mixer(proj, gates, bias_row, gain, bsz, seq, tile=MIX_TILE):
    t = proj.shape[0]
    nc = seq // tile
    row = lambda blk: pl.BlockSpec((tile, GROUP_WIDTH), lambda b, j, blk=blk: (b * nc + j, blk))
    const = lambda shape: pl.BlockSpec(shape, lambda b, j: (0,) * len(shape))
    return pl.pallas_call(
        _mlstm_kernel,
        grid=(bsz, nc),
        in_specs=[row(PB_LQ), row(PB_LK), row(PB_LV), row(PB_LO),
                  pl.BlockSpec((tile, LANES), lambda b, j: (b * nc + j, 0)),
                  const((1, LANES)), const((1, GROUP_WIDTH))],
        out_specs=pl.BlockSpec((tile, GROUP_WIDTH), lambda b, j: (b * nc + j, 0)),
        out_shape=jax.ShapeDtypeStruct((t, GROUP_WIDTH), BF16),
        scratch_shapes=[pltpu.VMEM((GROUP_WIDTH, GROUP_WIDTH), F32),
                        pltpu.VMEM((GROUP_WIDTH, LANES), F32),
                        pltpu.VMEM((1, LANES), F32)],
        compiler_params=_cparams("arbitrary", "arbitrary"),
        name="mlstm_mixer",
    )(proj, proj, proj, proj, gates, bias_row, gain)


def _outproj_router_kernel(x_ref, ya_ref, ys_ref, ym_ref, yl_ref, wo_ref, mod_ref, wn_ref, wr_ref, br_ref,
                           xo_ref, h2_ref, eid_ref, gcol_ref):
    W = GROUP_WIDTH
    tm = x_ref.shape[0]
    mod = mod_ref[0]
    out = _dot(ya_ref[...], wo_ref[0:W, :])
    out = out + _dot(ys_ref[...], wo_ref[W:2 * W, :])
    out = out + _dot(ym_ref[...], wo_ref[2 * W:3 * W, :])
    out = out + _dot(yl_ref[...], wo_ref[3 * W:4 * W, :])
    x1 = x_ref[...] + mod[2:3, :] * out
    xo_ref[...] = x1
    h2 = _rms_norm(x1, wn_ref[...]) * (1.0 + mod[4:5, :]) + mod[3:4, :]
    h2_ref[...] = _pack_bf16_pairs(h2)

    logits_t = (_dot(h2.astype(BF16), wr_ref[...]) + br_ref[...]).T
    row8 = lax.broadcasted_iota(I32, (SUBLANES, tm), 0).astype(F32)
    gl = jnp.where(row8 < N_EXPERT_GROUPS, logits_t[ROUTE_G:ROUTE_G + SUBLANES, :], NEG_INF)
    g_max = jnp.max(gl, axis=0, keepdims=True)
    g_sel = jnp.min(jnp.where(gl == g_max, row8, SUBLANES), axis=0, keepdims=True)
    g_prob = 1.0 / jnp.sum(jnp.exp(gl - g_max), axis=0, keepdims=True)
    el = logits_t[ROUTE_E:ROUTE_E + EXPERTS_PER_GROUP, :]
    for g in range(1, N_EXPERT_GROUPS):
        lo = ROUTE_E + g * EXPERTS_PER_GROUP
        el = jnp.where(g_sel == g, logits_t[lo:lo + EXPERTS_PER_GROUP, :], el)
    m1 = jnp.max(el, axis=0, keepdims=True)
    i1 = jnp.min(jnp.where(el == m1, row8, SUBLANES), axis=0, keepdims=True)
    el2 = jnp.where(row8 == i1, NEG_INF, el)
    m2 = jnp.max(el2, axis=0, keepdims=True)
    i2 = jnp.min(jnp.where(el2 == m2, row8, SUBLANES), axis=0, keepdims=True)
    ratio = jnp.exp(m2 - m1)
    p1 = 1.0 / (1.0 + ratio)
    eid_ref[0:1, :] = (g_sel * EXPERTS_PER_GROUP + i1).astype(I32)
    eid_ref[1:2, :] = (g_sel * EXPERTS_PER_GROUP + i2).astype(I32)
    rows = lax.broadcasted_iota(I32, (LANES, tm), 0)
    gate_rows = jnp.where(rows == 0, g_prob * p1, jnp.where(rows == 1, g_prob * p1 * ratio, 0.0))
    gcol_ref[...] = gate_rows.T


def _outproj_router(x2, ys, w_out, mod, w_norm2, w_route, b_route, seq, tm=512):
    t, d = x2.shape
    spb = seq // tm
    ytile = pl.BlockSpec((tm, GROUP_WIDTH), lambda i: (i, 0))
    return pl.pallas_call(
        _outproj_router_kernel,
        grid=(t // tm,),
        in_specs=[pl.BlockSpec((tm, d), lambda i: (i, 0)), ytile, ytile, ytile, ytile,
                  pl.BlockSpec((d, d), lambda i: (0, 0)),
                  pl.BlockSpec((1, 6, d), lambda i: (i // spb, 0, 0)),
                  pl.BlockSpec((1, d), lambda i: (0, 0)),
                  pl.BlockSpec((d, LANES), lambda i: (0, 0)),
                  pl.BlockSpec((1, LANES), lambda i: (0, 0))],
        out_specs=[pl.BlockSpec((tm, d), lambda i: (i, 0)),
                   pl.BlockSpec((tm, d // 2), lambda i: (i, 0)),
                   pl.BlockSpec((TOP_K, tm), lambda i: (0, i)),
                   pl.BlockSpec((tm, LANES), lambda i: (i, 0))],
        out_shape=[jax.ShapeDtypeStruct((t, d), F32),
                   jax.ShapeDtypeStruct((t, d // 2), jnp.uint32),
                   jax.ShapeDtypeStruct((TOP_K, t), I32),
                   jax.ShapeDtypeStruct((t, LANES), F32)],
        compiler_params=_cparams("arbitrary"),
        name="outproj_router",
    )(x2, *ys, w_out, mod, w_norm2, w_route, b_route)


def _rank_kernel(eid_ref, rank_ref, count_ref, carry):
    @pl.when(pl.program_id(0) == 0)
    def _():
        carry[...] = jnp.zeros_like(carry)

    tr = eid_ref.shape[1]
    expert = lax.broadcasted_iota(I32, (N_EXPERTS, tr), 0)
    before = (lax.broadcasted_iota(I32, (tr, tr), 0) < lax.broadcasted_iota(I32, (tr, tr), 1)).astype(BF16)
    base = carry[...]
    for k in range(TOP_K):
        onehot = (expert == eid_ref[k:k + 1, :]).astype(F32)
        prefix = _dot(onehot.astype(BF16), before)
        rank_ref[k:k + 1, :] = jnp.sum(onehot * (base + prefix), axis=0, keepdims=True).astype(I32)
        base = base + jnp.sum(onehot, axis=1, keepdims=True)
    carry[...] = base
    count_ref[...] = jnp.broadcast_to(base, count_ref.shape)


def _expert_ranks(eids, tr=512):
    t = eids.shape[1]
    return pl.pallas_call(
        _rank_kernel,
        grid=(t // tr,),
        in_specs=[pl.BlockSpec((TOP_K, tr), lambda i: (0, i))],
        out_specs=[pl.BlockSpec((TOP_K, tr), lambda i: (0, i)),
                   pl.BlockSpec((N_EXPERTS, LANES), lambda i: (0, 0))],
        out_shape=[jax.ShapeDtypeStruct((TOP_K, t), I32),
                   jax.ShapeDtypeStruct((N_EXPERTS, LANES), F32)],
        scratch_shapes=[pltpu.VMEM((N_EXPERTS, 1), F32)],
        compiler_params=_cparams("arbitrary"),
        name="expert_ranks",
    )(eids)


def _dest_kernel(pstart_ref, eid_ref, rank_ref, dest_ref):
    e = eid_ref[...]
    dest = rank_ref[...]
    for j in range(N_EXPERTS):
        dest = dest + jnp.where(e == j, pstart_ref[j], 0)
    dest_ref[...] = dest


def _dest_rows(p_starts, eids, ranks, tm=2048):
    t = eids.shape[1]
    grid_spec = pltpu.PrefetchScalarGridSpec(
        num_scalar_prefetch=1,
        grid=(t // tm,),
        in_specs=[pl.BlockSpec((TOP_K, tm), lambda i, ps: (0, i)),
                  pl.BlockSpec((TOP_K, tm), lambda i, ps: (0, i))],
        out_specs=pl.BlockSpec((TOP_K, tm), lambda i, ps: (0, i)))
    return pl.pallas_call(
        _dest_kernel,
        grid_spec=grid_spec,
        out_shape=jax.ShapeDtypeStruct((TOP_K, t), I32),
        compiler_params=_cparams("arbitrary"),
        name="dest_rows",
    )(p_starts, eids, ranks)


ROW_TILE = IDX_CHUNK // TOP_K


def _tile_indices(idx_hbm, idx_smem, idx_sems, tile, slot):
    return pltpu.make_async_copy(idx_hbm.at[pl.ds(tile * IDX_CHUNK, IDX_CHUNK)],
                                 idx_smem.at[pl.ds(slot * IDX_CHUNK, IDX_CHUNK)], idx_sems.at[slot])


def _dispatch_kernel(pend_ref, padded_ref, nu_ref, idx_hbm, h_ref, xb_hbm, idx_smem, zero_blk, idx_sems, row_sem,
                     zero_sem):
    def zero_block(start):
        return pltpu.make_async_copy(zero_blk, xb_hbm.at[pl.ds(pl.multiple_of(start, MOE_BLOCK), MOE_BLOCK)],
                                     zero_sem)

    @pl.when(pl.program_id(0) == 0)
    def _():
        zero_blk[...] = jnp.zeros_like(zero_blk)
        for e in range(N_EXPERTS):
            @pl.when(padded_ref[e] > 0)
            def _(e=e):
                zero_block(pend_ref[e] - MOE_BLOCK).start()
        for e in range(N_EXPERTS):
            @pl.when(padded_ref[e] > 0)
            def _(e=e):
                zero_block(pend_ref[e] - MOE_BLOCK).wait()

        def zero_unused(b, carry):
            copy = zero_block(b * MOE_BLOCK)
            copy.start()
            copy.wait()
            return carry

        lax.fori_loop(nu_ref[0], xb_hbm.shape[0] // MOE_BLOCK, zero_unused, 0)

    i = pl.program_id(0)

    @pl.when(i == 0)
    def _():
        _tile_indices(idx_hbm, idx_smem, idx_sems, i, 0).start()

    def row_copy(r, dst_row):
        return pltpu.make_async_copy(h_ref.at[pl.ds(r, 1)], xb_hbm.at[pl.ds(dst_row, 1)], row_sem)

    def step_for_slot(slot):
        _tile_indices(idx_hbm, idx_smem, idx_sems, i, slot).wait()

        @pl.when(i + 1 < pl.num_programs(0))
        def _():
            _tile_indices(idx_hbm, idx_smem, idx_sems, i + 1, 1 - slot).start()

        def issue(r, carry):
            for k in range(TOP_K):
                row_copy(r, idx_smem[slot * IDX_CHUNK + k * ROW_TILE + r]).start(priority=k)
            return carry

        lax.fori_loop(0, ROW_TILE, issue, 0, unroll=8)

    for parity in range(2):
        pl.when(i % 2 == parity)(functools.partial(step_for_slot, parity))

    def drain(r, carry):
        for k in range(TOP_K):
            row_copy(0, 0).wait()
        return carry

    lax.fori_loop(0, ROW_TILE, drain, 0, unroll=8)


def _dispatch(p_ends, padded, n_used, idx_tiles, h2, dst_rows):
    t, d = h2.shape
    any_spec = pl.BlockSpec(memory_space=pl.ANY)
    grid_spec = pltpu.PrefetchScalarGridSpec(
        num_scalar_prefetch=3,
        grid=(t // ROW_TILE,),
        in_specs=[any_spec, pl.BlockSpec((ROW_TILE, d), lambda i, pe, pd, nu: (i, 0))],
        out_specs=any_spec,
        scratch_shapes=[pltpu.SMEM((2 * IDX_CHUNK,), I32), pltpu.VMEM((MOE_BLOCK, d), h2.dtype),
                        pltpu.SemaphoreType.DMA((2,)), pltpu.SemaphoreType.DMA, pltpu.SemaphoreType.DMA])
    return pl.pallas_call(
        _dispatch_kernel,
        grid_spec=grid_spec,
        out_shape=jax.ShapeDtypeStruct((dst_rows, d), h2.dtype),
        compiler_params=_cparams("arbitrary"),
        name="moe_dispatch",
    )(p_ends, padded, n_used, idx_tiles, h2)


def _expert_kernel(be_ref, nu_ref, x_ref, wg_ref, wu_ref, wd_ref, y_ref, wg_b, wu_b, wd_b):
    b = pl.program_id(0)

    @pl.when(b < nu_ref[0])
    def _():
        @pl.when(jnp.logical_or(b == 0, be_ref[b] != be_ref[jnp.maximum(b - 1, 0)]))
        def _():
            wg_b[...] = wg_ref[0].astype(BF16)
            wu_b[...] = wu_ref[0].astype(BF16)
            wd_b[...] = wd_ref[0].astype(BF16)

        x = jnp.concatenate(_unpack_bf16_pairs(x_ref[...]), axis=1).astype(BF16)
        a = _silu(_dot(x, wg_b[...])) * _dot(x, wu_b[...])
        y_ref[...] = _pack_bf16_pairs(_dot(a.astype(BF16), wd_b[...]))

    @pl.when(b >= nu_ref[0])
    def _():
        y_ref[...] = jnp.zeros_like(y_ref)


def _expert_mlp(block_e, n_used, xb, w_gate, w_up, w_down):
    p, words = xb.shape
    d, de = w_gate.shape[1:]
    blk = lambda b, be, nu: (jnp.minimum(b, nu[0] - 1), 0)
    grid_spec = pltpu.PrefetchScalarGridSpec(
        num_scalar_prefetch=2,
        grid=(p // MOE_BLOCK,),
        in_specs=[pl.BlockSpec((MOE_BLOCK, words), blk),
                  pl.BlockSpec((1, d, de), lambda b, be, nu: (be[b], 0, 0)),
                  pl.BlockSpec((1, d, de), lambda b, be, nu: (be[b], 0, 0)),
                  pl.BlockSpec((1, de, d), lambda b, be, nu: (be[b], 0, 0))],
        out_specs=pl.BlockSpec((MOE_BLOCK, words), lambda b, be, nu: (b, 0)),
        scratch_shapes=[pltpu.VMEM((d, de), BF16), pltpu.VMEM((d, de), BF16), pltpu.VMEM((de, d), BF16)])
    return pl.pallas_call(
        _expert_kernel,
        grid_spec=grid_spec,
        out_shape=jax.ShapeDtypeStruct((p, words), jnp.uint32),
        compiler_params=_cparams("arbitrary"),
        name="expert_mlp",
    )(block_e, n_used, xb, w_gate, w_up, w_down)


def _combine_kernel(idx_hbm, yb_hbm, x_ref, gcol_ref, mod_ref, wnf_ref, o_ref, ybuf, idx_smem, idx_sems, row_sems,
                    *, final):
    i = pl.program_id(0)
    n = pl.num_programs(0)

    indices = functools.partial(_tile_indices, idx_hbm, idx_smem, idx_sems)

    def row_copy(src_row, slot, k, r):
        return pltpu.make_async_copy(yb_hbm.at[pl.ds(src_row, 1)], ybuf.at[slot, k, pl.ds(r, 1)],
                                     row_sems.at[slot])

    def issue_tile(slot):
        def issue(r, carry):
            for k in range(TOP_K):
                row_copy(idx_smem[slot * IDX_CHUNK + k * ROW_TILE + r], slot, k, r).start(priority=k)
            return carry

        lax.fori_loop(0, ROW_TILE, issue, 0, unroll=8)

    @pl.when(i == 0)
    def _():
        first = indices(i, 0)
        first.start()
        first.wait()
        issue_tile(0)

        @pl.when(n > 1)
        def _():
            indices(i + 1, 1).start()

    def step_for_slot(slot):
        other = 1 - slot

        @pl.when(i + 1 < n)
        def _():
            indices(i + 1, other).wait()

        @pl.when(i + 2 < n)
        def _():
            indices(i + 2, slot).start()

        @pl.when(i + 1 < n)
        def _():
            issue_tile(other)

        def drain(r, carry):
            for k in range(TOP_K):
                row_copy(0, slot, k, 0).wait()
            return carry

        lax.fori_loop(0, ROW_TILE, drain, 0, unroll=8)

        gc = gcol_ref[...]
        y0 = _unpack_bf16_pairs(ybuf[slot, 0])
        y1 = _unpack_bf16_pairs(ybuf[slot, 1])
        moe = jnp.concatenate([gc[:, 0:1] * y0[0] + gc[:, 1:2] * y1[0], gc[:, 0:1] * y0[1] + gc[:, 1:2] * y1[1]],
                              axis=1)
        x2 = x_ref[...] + mod_ref[0][5:6, :] * moe
        o_ref[...] = _rms_norm(x2, wnf_ref[...]) if final else x2

    for parity in range(2):
        pl.when(i % 2 == parity)(functools.partial(step_for_slot, parity))


def _combine(idx_tiles, yb, x2, gcol, mod, w_norm_final, seq, final):
    t, d = x2.shape
    tm = ROW_TILE
    spb = seq // tm
    any_spec = pl.BlockSpec(memory_space=pl.ANY)
    return pl.pallas_call(
        functools.partial(_combine_kernel, final=final),
        grid=(t // tm,),
        in_specs=[any_spec, any_spec,
                  pl.BlockSpec((tm, d), lambda i: (i, 0)),
                  pl.BlockSpec((tm, LANES), lambda i: (i, 0)),
                  pl.BlockSpec((1, 6, d), lambda i: (i // spb, 0, 0)),
                  pl.BlockSpec((1, d), lambda i: (0, 0))],
        out_specs=pl.BlockSpec((tm, d), lambda i: (i, 0)),
        out_shape=jax.ShapeDtypeStruct((t, d), F32),
        scratch_shapes=[pltpu.VMEM((2, TOP_K, tm, d // 2), jnp.uint32), pltpu.SMEM((2 * IDX_CHUNK,), I32),
                        pltpu.SemaphoreType.DMA((2,)), pltpu.SemaphoreType.DMA((2,))],
        compiler_params=_cparams("arbitrary"),
        name="moe_combine",
    )(idx_tiles, yb, x2, gcol, mod, w_norm_final)


def _lane_row(pieces, width=LANES):
    row = jnp.zeros((width,), F32)
    for off, vec in pieces.items():
        row = row.at[off:off + vec.shape[0]].set(vec.astype(F32))
    return row.reshape(1, width)


def _split_w_in(w_in):
    gw, nh = GROUP_WIDTH, N_HEADS
    widths = [gw, gw, gw, nh, gw, gw, gw, 3 * gw, nh, gw, gw, gw, gw, nh, nh]
    cuts, acc = [], 0
    for w in widths[:-1]:
        acc += w
        cuts.append(acc)
    (aq, ak, av, af, su, sv, mz, mxbc, mdt, lq, lk, lv, lo, li, lf) = jnp.split(w_in, cuts, axis=1)
    scale = HEAD_DIM ** -0.5
    wmain = jnp.concatenate([ak, su, sv, mz, mxbc, lq, lk * scale, lv, lo], axis=1).astype(BF16)
    wqt = (aq * scale).T.astype(BF16)
    wvt = av.T.astype(BF16)
    wg = jnp.zeros((w_in.shape[0], LANES), F32)
    for off, w in ((GATE_AF, af), (GATE_DT, mdt), (GATE_LI, li), (GATE_LF, lf)):
        wg = wg.at[:, off:off + nh].set(w)
    return wmain, wqt, wvt, wg.astype(BF16)


def _moe_layer(x1, h2, eids, gcol, mod_l, w_gate, w_up, w_down, layer, w_norm_final, seq, final):
    t, d = x1.shape
    ranks, counts = _expert_ranks(eids)
    counts = counts[:, 0].astype(I32)
    padded = ((counts + MOE_BLOCK - 1) // MOE_BLOCK) * MOE_BLOCK
    p_ends = jnp.cumsum(padded)
    p_starts = (p_ends - padded).astype(I32)
    n_blocks = (t * TOP_K) // MOE_BLOCK + N_EXPERTS
    blocks = jnp.arange(n_blocks, dtype=I32)
    block_e = jnp.sum((p_ends[None, :] <= (blocks * MOE_BLOCK)[:, None]).astype(I32), axis=1)
    block_e = jnp.minimum(block_e, N_EXPERTS - 1)
    n_used = (p_ends[-1:] // MOE_BLOCK).astype(I32)
    block_e = jnp.where(blocks < n_used, block_e, block_e[n_used[0] - 1])
    dest = _dest_rows(p_starts, eids, ranks)
    idx_tiles = dest.reshape(TOP_K, t // ROW_TILE, ROW_TILE).transpose(1, 0, 2).reshape(-1)
    xb = _dispatch(p_ends.astype(I32), padded.astype(I32), n_used, idx_tiles, h2, n_blocks * MOE_BLOCK)
    yb = _expert_mlp(block_e + layer * N_EXPERTS, n_used, xb, w_gate, w_up, w_down)
    return _combine(idx_tiles, yb, x1, gcol, mod_l, w_norm_final, seq, final)


def kernel(x, c, w_in, w_out, w_mix_norm, attn_f_bias, sg_w, sg_b, ssm_conv_w, ssm_conv_b, ssm_dt_bias,
           ssm_a_log, ssm_d, mlstm_i_bias, mlstm_f_bias, w_ada, b_ada, w_norm1, w_norm2, w_router_group,
           b_router_group, w_router_expert, b_router_expert, w_expert_gate, w_expert_up, w_expert_down,
           w_norm_final):
    bsz, seq, d = x.shape
    depth = w_in.shape[0]
    gw = GROUP_WIDTH
    mod = _ada_modulation(c, w_ada, b_ada).reshape(depth, bsz, 6, d)
    x2 = x.reshape(bsz * seq, d)
    wnf = w_norm_final.reshape(1, d)
    w_eg = w_expert_gate.reshape((depth * N_EXPERTS,) + w_expert_gate.shape[2:])
    w_eu = w_expert_up.reshape((depth * N_EXPERTS,) + w_expert_up.shape[2:])
    w_ed = w_expert_down.reshape((depth * N_EXPERTS,) + w_expert_down.shape[2:])
    for l in range(depth):
        wmain, wqt, wvt, wg = _split_w_in(w_in[l])
        gains = w_mix_norm[l].reshape(N_HEADS, 1, gw)
        proj, qt, vt, gates = _inproj(x2, mod[l], w_norm1[l].reshape(1, d), wmain, wqt, wvt, wg, bsz, seq)
        f2 = _forget_cumsum(gates, _lane_row({GATE_AF: attn_f_bias[l]}), bsz, seq)
        y_attn = _attention(proj, qt, vt, f2, gains[0], bsz, seq)
        sg_bias_full = jnp.repeat(sg_b[l].T, HEAD_DIM, axis=1)
        y_sg = _spatial_gating(proj, gains[1], sg_w[l], sg_bias_full)
        y_ssm = _ssd_mixer(proj, gates, ssm_conv_w[l], ssm_conv_b[l].reshape(1, -1),
                           _lane_row({GATE_DT: ssm_dt_bias[l]}), _lane_row({GATE_DT: ssm_a_log[l]}),
                           jnp.repeat(ssm_d[l], HEAD_DIM).reshape(1, gw), gains[2], bsz, seq)
        y_ml = _mlstm_mixer(proj, gates, _lane_row({GATE_LI: mlstm_i_bias[l], GATE_LF: mlstm_f_bias[l]}),
                            gains[3], bsz, seq)
        w_route = jnp.zeros((d, LANES), F32)
        w_route = w_route.at[:, ROUTE_G:ROUTE_G + N_EXPERT_GROUPS].set(w_router_group[l])
        w_route = w_route.at[:, ROUTE_E:ROUTE_E + N_EXPERTS].set(w_router_expert[l]).astype(BF16)
        b_route = _lane_row({ROUTE_G: b_router_group[l], ROUTE_E: b_router_expert[l]})
        x1, h2, eids, gcol = _outproj_router(x2, (y_attn, y_sg, y_ssm, y_ml), w_out[l].astype(BF16), mod[l],
                                             w_norm2[l].reshape(1, d), w_route, b_route, seq)
        x2 = _moe_layer(x1, h2, eids, gcol, mod[l], w_eg, w_eu, w_ed, l, wnf, seq, final=(l == depth - 1))
    return x2.reshape(bsz, seq, d)
```

```python
import functools

import jax
import jax.numpy as jnp
from jax import lax
from jax.experimental import pallas as pl
from jax.experimental.pallas import tpu as pltpu

F32 = jnp.float32
BF16 = jnp.bfloat16
I32 = jnp.int32

LANES = 128
SUBLANES = 8
HEAD_DIM = 64
N_HEADS = 4
GROUP_WIDTH = 256
CHUNK = 128
MIX_TILE = 4 * CHUNK
SSM_GROUPS = 2
CONV_WIDTH = 4
N_EXPERT_GROUPS = 4
EXPERTS_PER_GROUP = 8
N_EXPERTS = N_EXPERT_GROUPS * EXPERTS_PER_GROUP
TOP_K = 2
MOE_BLOCK = 512
NORM_EPS = 1e-6
MLSTM_M_INIT = -1e30
NEG_INF = float("-inf")
VMEM_LIMIT_BYTES = 48 * 1024 * 1024
IDX_CHUNK = 1024

(PB_K, PB_SU, PB_SV, PB_MZ, PB_X, PB_B, PB_C, PB_LQ, PB_LK, PB_LV, PB_LO) = range(11)
PROJ_COLS = 11 * GROUP_WIDTH
LOG2_E = 1.4426950408889634
GATE_AF, GATE_DT, GATE_LI, GATE_LF = 0, 4, 8, 12
ROUTE_G, ROUTE_E = 0, 8

NT_DIMS = (((1,), (1,)), ((), ()))


def _cparams(*sem):
    return pltpu.CompilerParams(dimension_semantics=sem, vmem_limit_bytes=VMEM_LIMIT_BYTES)


def _dot(a, b):
    return jnp.dot(a, b, preferred_element_type=F32)


def _dot_nt(a, b):
    return lax.dot_general(a, b, NT_DIMS, preferred_element_type=F32)


def _dot_exact(a, b):
    return jnp.dot(a, b, preferred_element_type=F32, precision=lax.Precision.HIGHEST)


def _head_of_lane(shape, axis=1):
    return lax.broadcasted_iota(I32, shape, axis) // HEAD_DIM


def _sigmoid(x):
    return 1.0 / (1.0 + jnp.exp(-x))


def _silu(x):
    return x * _sigmoid(x)


def _log_sigmoid(x):
    return jnp.minimum(x, 0.0) - jnp.log1p(jnp.exp(-jnp.abs(x)))


def _softplus(x):
    return jnp.maximum(x, 0.0) + jnp.log1p(jnp.exp(-jnp.abs(x)))


def _gelu_tanh(x):
    return 0.5 * x * (1.0 + jnp.tanh(0.7978845608028654 * (x + 0.044715 * (x * x * x))))


def _expand_heads(cols, width=GROUP_WIDTH):
    rows = cols[0].shape[0]
    head = _head_of_lane((rows, width))
    out = jnp.broadcast_to(cols[N_HEADS - 1], (rows, width))
    for h in range(N_HEADS - 2, -1, -1):
        out = jnp.where(head == h, jnp.broadcast_to(cols[h], (rows, width)), out)
    return out


def _mask_head(x, h):
    return jnp.where(_head_of_lane(x.shape) == h, x, jnp.zeros_like(x))


def _head_rms_norm(y, gain):
    head = _head_of_lane(y.shape)
    sq = y * y
    cols = [jnp.sum(jnp.where(head == h, sq, 0.0), axis=1, keepdims=True) * (1.0 / HEAD_DIM)
            for h in range(N_HEADS)]
    return y * lax.rsqrt(_expand_heads(cols) + NORM_EPS) * gain


def _rms_norm(x, gain):
    ms = jnp.mean(x * x, axis=1, keepdims=True)
    return x * lax.rsqrt(ms + NORM_EPS) * gain


def _pack_bf16_pairs(x):
    half = x.shape[1] // 2
    bits = lax.bitcast_convert_type(x.astype(BF16).astype(F32), jnp.uint32)
    return bits[:, :half] | lax.shift_right_logical(bits[:, half:], jnp.uint32(16))


def _unpack_bf16_pairs(words):
    first = lax.bitcast_convert_type(words & jnp.uint32(0xFFFF0000), F32)
    second = lax.bitcast_convert_type(lax.shift_left(words, jnp.uint32(16)), F32)
    return first, second


def _tril(n, strict=False):
    r = lax.broadcasted_iota(I32, (n, n), 0)
    c = lax.broadcasted_iota(I32, (n, n), 1)
    return (r > c) if strict else (r >= c)


def _ada_kernel(c_ref, w_ref, b_ref, o_ref):
    o_ref[0] = _dot_exact(_silu(c_ref[...]), w_ref[0]) + b_ref[0]


def _ada_modulation(c, w_ada, b_ada):
    depth, d, d6 = w_ada.shape
    bsz = c.shape[0]
    return pl.pallas_call(
        _ada_kernel,
        grid=(depth, d6 // d),
        in_specs=[pl.BlockSpec((bsz, d), lambda l, j: (0, 0)),
                  pl.BlockSpec((1, d, d), lambda l, j: (l, 0, j)),
                  pl.BlockSpec((1, 1, d), lambda l, j: (l, 0, j))],
        out_specs=pl.BlockSpec((1, bsz, d), lambda l, j: (l, 0, j)),
        out_shape=jax.ShapeDtypeStruct((depth, bsz, d6), F32),
        compiler_params=_cparams("arbitrary", "arbitrary"),
        name="ada_modulation",
    )(c, w_ada, b_ada.reshape(depth, 1, d6))


def _forget_bias_slab(gates, carry):
    tb = gates.shape[0]
    cum = _dot_exact(_tril(tb).astype(F32), _log_sigmoid(gates)) + carry[...]
    carry[...] = cum[tb - 1:tb, :]
    f2 = cum * LOG2_E
    hi = f2.astype(BF16)
    rest = f2 - hi.astype(F32)
    mid = rest.astype(BF16)
    lo = (rest - mid.astype(F32)).astype(BF16)
    r = lax.broadcasted_iota(I32, (LANES, GROUP_WIDTH), 0) - GATE_AF
    c = lax.broadcasted_iota(I32, (LANES, GROUP_WIDTH), 1)
    slab = jnp.zeros((tb, GROUP_WIDTH), F32)
    for j, piece in enumerate((hi, mid, lo)):
        place = (r >= 0) & (r < N_HEADS) & (c == ((r + 1) % N_HEADS) * HEAD_DIM + j)
        slab = slab + _dot(piece, jnp.where(place, -1.0, 0.0).astype(BF16))
    return slab.astype(BF16)


def _inproj_kernel(x_ref, mod_ref, wn_ref, wmain_ref, wqt_ref, wvt_ref, wg_ref, fbias_ref,
                   proj_ref, qt_ref, vt_ref, gates_ref, slab_ref, h_scr, f_carry, *, tiles_per_seq):
    @pl.when(pl.program_id(0) % tiles_per_seq == 0)
    def _():
        f_carry[...] = jnp.zeros_like(f_carry)

    mod = mod_ref[0]
    h = _rms_norm(x_ref[...], wn_ref[...]) * (1.0 + mod[1:2, :]) + mod[0:1, :]
    h_scr[...] = h.astype(BF16)
    for c0 in range(0, PROJ_COLS, GROUP_WIDTH):
        proj_ref[:, c0:c0 + GROUP_WIDTH] = _dot(h_scr[...], wmain_ref[:, c0:c0 + GROUP_WIDTH]).astype(BF16)
    qt_ref[0] = (_dot_nt(wqt_ref[...], h_scr[...]) * LOG2_E).astype(BF16)
    vt_ref[0] = _dot_nt(wvt_ref[...], h_scr[...]).astype(BF16)
    gates = _dot(h_scr[...], wg_ref[...])
    gates_ref[...] = gates
    slab_ref[...] = _forget_bias_slab(gates + fbias_ref[...], f_carry)


def _inproj(x2, mod, w_norm, wmain, wqt, wvt, wg, f_bias_row, bsz, seq, tm=512):
    t, d = x2.shape
    spb = seq // tm
    tspec = pl.BlockSpec((1, GROUP_WIDTH, tm), lambda i: (i // spb, 0, i % spb))
    return pl.pallas_call(
        functools.partial(_inproj_kernel, tiles_per_seq=spb),
        grid=(t // tm,),
        in_specs=[pl.BlockSpec((tm, d), lambda i: (i, 0)),
                  pl.BlockSpec((1, 6, d), lambda i: (i // spb, 0, 0)),
                  pl.BlockSpec((1, d), lambda i: (0, 0)),
                  pl.BlockSpec((d, PROJ_COLS), lambda i: (0, 0)),
                  pl.BlockSpec((GROUP_WIDTH, d), lambda i: (0, 0)),
                  pl.BlockSpec((GROUP_WIDTH, d), lambda i: (0, 0)),
                  pl.BlockSpec((d, LANES), lambda i: (0, 0)),
                  pl.BlockSpec((1, LANES), lambda i: (0, 0))],
        out_specs=[pl.BlockSpec((tm, PROJ_COLS), lambda i: (i, 0)), tspec, tspec,
                   pl.BlockSpec((tm, LANES), lambda i: (i, 0)),
                   pl.BlockSpec((tm, GROUP_WIDTH), lambda i: (i, 0))],
        out_shape=[jax.ShapeDtypeStruct((t, PROJ_COLS), BF16),
                   jax.ShapeDtypeStruct((bsz, GROUP_WIDTH, seq), BF16),
                   jax.ShapeDtypeStruct((bsz, GROUP_WIDTH, seq), BF16),
                   jax.ShapeDtypeStruct((t, LANES), F32),
                   jax.ShapeDtypeStruct((t, GROUP_WIDTH), BF16)],
        scratch_shapes=[pltpu.VMEM((tm, d), BF16), pltpu.VMEM((1, LANES), F32)],
        compiler_params=_cparams("arbitrary"),
        name="norm_inproj",
    )(x2, mod, w_norm, wmain, wqt, wvt, wg, f_bias_row)


ONES_ROWS = 16


def _attn_kernel(qi_ref, kj_ref, k_ref, qt_ref, vt_ref, f_ref, gain_ref, o_ref, qtm, m_s, l_s, acc):
    p = pl.program_id(1)
    qi, kj = qi_ref[p], kj_ref[p]
    tk = k_ref.shape[0]
    tq = qt_ref.shape[2]

    @pl.when(kj == 0)
    def _():
        qt = qt_ref[0]
        row = lax.broadcasted_iota(I32, qt.shape, 0)
        for h in range(N_HEADS):
            slot = ((h + 1) % N_HEADS) * HEAD_DIM
            ones_rows = jnp.where((row >= slot) & (row < slot + 3), 1.0, 0.0).astype(BF16)
            qtm[h] = jnp.where(row // HEAD_DIM == h, qt, ones_rows)
        m_s[...] = jnp.full_like(m_s, NEG_INF)
        l_s[...] = jnp.zeros_like(l_s)
        acc[...] = jnp.zeros_like(acc)

    def step(diagonal):
        k = k_ref[...]
        bias = f_ref[...]
        head = _head_of_lane(k.shape)
        ones = jnp.ones((ONES_ROWS, tk), BF16)
        if diagonal:
            visible = lax.broadcasted_iota(I32, (tk, tq), 0) <= lax.broadcasted_iota(I32, (tk, tq), 1)
        scores, m_news, alphas, probs = {}, {}, {}, {}

        def score(h):
            s = _dot(jnp.where(head == h, k, bias), qtm[h])
            if diagonal:
                s = jnp.where(visible, s, NEG_INF)
            scores[h] = s
            m_prev = m_s[h]
            m_news[h] = jnp.maximum(m_prev, jnp.max(s, axis=0, keepdims=True))
            alphas[h] = jnp.exp2(m_prev - m_news[h])
            m_s[h] = m_news[h]

        def prob(h):
            probs[h] = jnp.exp2(scores[h] - m_news[h]).astype(BF16)

        def weighted_sum(h):
            vt_ext = jnp.concatenate([vt_ref[0, h * HEAD_DIM:(h + 1) * HEAD_DIM, :], ones], axis=0)
            pv = _dot(vt_ext, probs[h])
            acc[h] = alphas[h] * acc[h] + pv[0:HEAD_DIM, :]
            l_s[h] = alphas[h] * l_s[h] + pv[HEAD_DIM:HEAD_DIM + 1, :]

        for stage in range(N_HEADS + 2):
            if stage < N_HEADS:
                score(stage)
            if 0 <= stage - 1 < N_HEADS:
                prob(stage - 1)
            if 0 <= stage - 2 < N_HEADS:
                weighted_sum(stage - 2)

    @pl.when(kj < qi)
    def _():
        step(False)

    @pl.when(kj == qi)
    def _():
        step(True)
        yt = jnp.concatenate([acc[h] / l_s[h] for h in range(N_HEADS)], axis=0)
        o_ref[...] = _head_rms_norm(yt.T, gain_ref[...]).astype(BF16)


def _attention(proj, qt, vt, f2, gain, bsz, seq, tq=1024):
    nq = seq // tq
    pairs = [(i, j) for i in range(nq) for j in range(i + 1)]
    qi = jnp.asarray([a for a, _ in pairs], I32)
    kj = jnp.asarray([b for _, b in pairs], I32)
    t = bsz * seq
    grid_spec = pltpu.PrefetchScalarGridSpec(
        num_scalar_prefetch=2,
        grid=(bsz, len(pairs)),
        in_specs=[pl.BlockSpec((tq, GROUP_WIDTH), lambda b, p, qi, kj: (b * nq + kj[p], PB_K)),
                  pl.BlockSpec((1, GROUP_WIDTH, tq), lambda b, p, qi, kj: (b, 0, qi[p])),
                  pl.BlockSpec((1, GROUP_WIDTH, tq), lambda b, p, qi, kj: (b, 0, kj[p])),
                  pl.BlockSpec((tq, GROUP_WIDTH), lambda b, p, qi, kj: (b * nq + kj[p], 0)),
                  pl.BlockSpec((1, GROUP_WIDTH), lambda b, p, qi, kj: (0, 0))],
        out_specs=pl.BlockSpec((tq, GROUP_WIDTH), lambda b, p, qi, kj: (b * nq + qi[p], 0)),
        scratch_shapes=[pltpu.VMEM((N_HEADS, GROUP_WIDTH, tq), BF16),
                        pltpu.VMEM((N_HEADS, 1, tq), F32),
                        pltpu.VMEM((N_HEADS, 1, tq), F32),
                        pltpu.VMEM((N_HEADS, HEAD_DIM, tq), F32)])
    return pl.pallas_call(
        _attn_kernel,
        grid_spec=grid_spec,
        out_shape=jax.ShapeDtypeStruct((t, GROUP_WIDTH), BF16),
        compiler_params=_cparams("arbitrary", "arbitrary"),
        name="fox_attention",
    )(qi, kj, proj, qt, vt, f2, gain)


def _sg_kernel(u_ref, v_ref, gain_ref, w_ref, b_ref, o_ref):
    tm = u_ref.shape[0]
    u = _gelu_tanh(u_ref[...].astype(F32))
    v = _head_rms_norm(_gelu_tanh(v_ref[...].astype(F32)), gain_ref[...])
    causal = _tril(CHUNK)
    ws = [jnp.where(causal, w_ref[h], 0.0).astype(BF16) for h in range(N_HEADS)]
    for c0 in range(0, tm, CHUNK):
        vc = v[c0:c0 + CHUNK, :].astype(BF16)
        mixed = b_ref[...]
        for h in range(N_HEADS):
            mixed = mixed + _dot(ws[h], _mask_head(vc, h))
        o_ref[c0:c0 + CHUNK, :] = (u[c0:c0 + CHUNK, :] * mixed).astype(BF16)


def _spatial_gating(proj, gain, sg_w, sg_bias_full, tm=512):
    t = proj.shape[0]
    return pl.pallas_call(
        _sg_kernel,
        grid=(t // tm,),
        in_specs=[pl.BlockSpec((tm, GROUP_WIDTH), lambda i: (i, PB_SU)),
                  pl.BlockSpec((tm, GROUP_WIDTH), lambda i: (i, PB_SV)),
                  pl.BlockSpec((1, GROUP_WIDTH), lambda i: (0, 0)),
                  pl.BlockSpec((N_HEADS, CHUNK, CHUNK), lambda i: (0, 0, 0)),
                  pl.BlockSpec((CHUNK, GROUP_WIDTH), lambda i: (0, 0))],
        out_specs=pl.BlockSpec((tm, GROUP_WIDTH), lambda i: (i, 0)),
        out_shape=jax.ShapeDtypeStruct((t, GROUP_WIDTH), BF16),
        compiler_params=_cparams("arbitrary"),
        name="spatial_gating",
    )(proj, proj, gain, sg_w, sg_bias_full)


def _ssd_kernel(z_ref, x_ref, b_ref, c_ref, g_ref, cw_ref, cb_ref, dtb_ref, alog_ref, dskip_ref, gain_ref,
                o_ref, conv_scr, xbc_scr, state):
    L = CHUNK
    W = GROUP_WIDTH
    tile = x_ref.shape[0]

    @pl.when(pl.program_id(1) == 0)
    def _():
        conv_scr[0:SUBLANES, :] = jnp.zeros((SUBLANES, 3 * W), F32)
        state[...] = jnp.zeros_like(state)

    conv_scr[SUBLANES:, 0:W] = x_ref[...].astype(F32)
    conv_scr[SUBLANES:, W:2 * W] = b_ref[...].astype(F32)
    conv_scr[SUBLANES:, 2 * W:] = c_ref[...].astype(F32)
    cw = cw_ref[...]
    conv = cb_ref[...] + cw[CONV_WIDTH - 1:CONV_WIDTH, :] * conv_scr[SUBLANES:, :]
    for s in range(1, CONV_WIDTH):
        conv = conv + cw[CONV_WIDTH - 1 - s:CONV_WIDTH - s, :] * conv_scr[SUBLANES - s:SUBLANES - s + tile, :]
    conv_scr[0:SUBLANES, :] = conv_scr[tile:tile + SUBLANES, :]
    xbc_scr[...] = _silu(conv)

    neg_a = -jnp.exp(alog_ref[...])
    for c0 in range(0, tile, L):
        rows = slice(c0, c0 + L)
        dt = _softplus(g_ref[rows, :] + dtb_ref[...])
        y = _ssd_chunk(xbc_scr[rows, 0:W], xbc_scr[rows, W:2 * W], xbc_scr[rows, 2 * W:], dt, dt * neg_a,
                       dskip_ref[...], state)
        y = y * _silu(z_ref[rows, :].astype(F32))
        o_ref[rows, :] = _head_rms_norm(y, gain_ref[...]).astype(BF16)


def _ssd_chunk(xs, bm, cm, dt, da, dskip, state):
    L = CHUNK
    a_cum = _dot_exact(_tril(L).astype(F32), da)
    a_row = a_cum.T
    dt_cols = [dt[:, GATE_DT + h:GATE_DT + h + 1] for h in range(N_HEADS)]
    a_cols = [a_cum[:, GATE_DT + h:GATE_DT + h + 1] for h in range(N_HEADS)]
    a_end = [a_cum[L - 1:L, GATE_DT + h:GATE_DT + h + 1] for h in range(N_HEADS)]
    xdt = xs * _expand_heads(dt_cols)
    xdt_b = xdt.astype(BF16)
    xw = (xdt * _expand_heads([jnp.exp(a_end[h] - a_cols[h]) for h in range(N_HEADS)])).astype(BF16)
    exp_a = _expand_heads([jnp.exp(a_cols[h]) for h in range(N_HEADS)])
    causal = _tril(L)
    half = lax.broadcasted_iota(I32, (1, LANES), 1) < HEAD_DIM

    y = dskip * xs
    y_off = []
    for g in range(SSM_GROUPS):
        bg = bm[:, g * LANES:(g + 1) * LANES]
        cg = cm[:, g * LANES:(g + 1) * LANES].astype(BF16)
        scores = _dot_nt(cg, bg.astype(BF16))
        for r in range(N_HEADS // SSM_GROUPS):
            h = g * (N_HEADS // SSM_GROUPS) + r
            seg = a_cols[h] - a_row[GATE_DT + h:GATE_DT + h + 1, :]
            decay = jnp.exp(jnp.where(causal, seg, NEG_INF))
            y = y + _dot((scores * decay).astype(BF16), _mask_head(xdt_b, h))
        st_in = state[g]
        y_off.append(_dot(cg, st_in.astype(BF16)))
        chunk_decay = jnp.where(half, jnp.exp(a_end[2 * g]), jnp.exp(a_end[2 * g + 1]))
        state[g] = chunk_decay * st_in + _dot(bg.T.astype(BF16), xw[:, g * LANES:(g + 1) * LANES])
    return y + jnp.concatenate(y_off, axis=1) * exp_a


def _ssd_mixer(proj, gates, conv_w, conv_b, dtb_row, alog_row, dskip_row, gain, bsz, seq, tile=MIX_TILE):
    t = proj.shape[0]
    nc = seq // tile
    row = lambda blk: pl.BlockSpec((tile, GROUP_WIDTH), lambda b, j, blk=blk: (b * nc + j, blk))
    const = lambda shape: pl.BlockSpec(shape, lambda b, j: (0,) * len(shape))
    return pl.pallas_call(
        _ssd_kernel,
        grid=(bsz, nc),
        in_specs=[row(PB_MZ), row(PB_X), row(PB_B), row(PB_C),
                  pl.BlockSpec((tile, LANES), lambda b, j: (b * nc + j, 0)),
                  const((CONV_WIDTH, 3 * GROUP_WIDTH)), const((1, 3 * GROUP_WIDTH)),
                  const((1, LANES)), const((1, LANES)), const((1, GROUP_WIDTH)), const((1, GROUP_WIDTH))],
        out_specs=pl.BlockSpec((tile, GROUP_WIDTH), lambda b, j: (b * nc + j, 0)),
        out_shape=jax.ShapeDtypeStruct((t, GROUP_WIDTH), BF16),
        scratch_shapes=[pltpu.VMEM((tile + SUBLANES, 3 * GROUP_WIDTH), F32),
                        pltpu.VMEM((tile, 3 * GROUP_WIDTH), F32),
                        pltpu.VMEM((SSM_GROUPS, LANES, LANES), F32)],
        compiler_params=_cparams("arbitrary", "arbitrary"),
        name="ssd_mixer",
    )(proj, proj, proj, proj, gates, conv_w, conv_b, dtb_row, alog_row, dskip_row, gain)


def _mlstm_kernel(q_ref, k_ref, v_ref, o_gate_ref, g_ref, bias_ref, gain_ref, o_ref, ct, nb, m_row):
    L = CHUNK
    W = GROUP_WIDTH

    @pl.when(pl.program_id(1) == 0)
    def _():
        ct[...] = jnp.zeros_like(ct)
        nb[...] = jnp.zeros_like(nb)
        m_row[...] = jnp.full_like(m_row, MLSTM_M_INIT)

    def chunk(c, carry):
        rows = pl.ds(pl.multiple_of(c * L, L), L)
        y = _mlstm_chunk(q_ref[rows, :], k_ref[rows, :], v_ref[rows, :], g_ref[rows, :] + bias_ref[...],
                         ct, nb, m_row)
        y = _sigmoid(o_gate_ref[rows, :].astype(F32)) * _head_rms_norm(y, gain_ref[...])
        o_ref[rows, :] = y.astype(BF16)
        return carry

    lax.fori_loop(0, q_ref.shape[0] // L, chunk, 0)


def _mlstm_chunk(q, k, v, gate, ct, nb, m_row):
    L = CHUNK
    W = GROUP_WIDTH
    a_full = _dot_exact(_tril(L).astype(F32), _log_sigmoid(gate))
    a_rows = a_full.T
    g_rows = gate.T
    causal = _tril(L)
    lane = lax.broadcasted_iota(I32, (1, LANES), 1)

    inter_q = _dot(q, ct[...].astype(BF16))
    n_q = _dot(q, nb[...].astype(BF16))
    m_old = m_row[...]
    num = jnp.zeros((L, W), F32)
    inter_cols, den_cols, ws_cols, scale_cols = [], [], [], []
    m_next = m_old
    for h in range(N_HEADS):
        a_col = a_full[:, GATE_LF + h:GATE_LF + h + 1]
        i_col = gate[:, GATE_LI + h:GATE_LI + h + 1]
        a_r = a_rows[GATE_LF + h:GATE_LF + h + 1, :]
        i_r = g_rows[GATE_LI + h:GATE_LI + h + 1, :]
        a_end = a_full[L - 1:L, GATE_LF + h:GATE_LF + h + 1]
        m_in = m_old[:, h:h + 1]
        log_d = jnp.where(causal, a_col - a_r + i_r, NEG_INF)
        log_inter = a_col + m_in
        m_t = jnp.maximum(jnp.max(log_d, axis=1, keepdims=True), log_inter)
        w = _dot_nt(_mask_head(q, h), k) * jnp.exp(log_d - m_t)
        inter = jnp.exp(log_inter - m_t)
        num = num + _dot(w.astype(BF16), _mask_head(v, h))
        den = jnp.sum(w, axis=1, keepdims=True) + inter * n_q[:, h:h + 1]
        inter_cols.append(inter)
        den_cols.append(jnp.maximum(jnp.abs(den), jnp.exp(-m_t)))
        g_col = a_end - a_col + i_col
        m_new = jnp.maximum(a_end + m_in, jnp.max(g_col, axis=0, keepdims=True))
        ws_cols.append(jnp.exp(g_col - m_new))
        scale_cols.append(jnp.exp(a_end + m_in - m_new))
        m_next = jnp.where(lane == h, m_new, m_next)
    hout = (num + _expand_heads(inter_cols) * inter_q) / _expand_heads(den_cols)

    kw_t = (k.astype(F32) * _expand_heads(ws_cols)).T.astype(BF16)
    scale_row = _expand_heads(scale_cols)
    same_head = _head_of_lane((W, W), 0) == _head_of_lane((W, W), 1)
    ct[...] = scale_row * ct[...] + jnp.where(same_head, _dot(kw_t, v), 0.0)
    col_is_head = _head_of_lane((W, LANES), 0) == lax.broadcasted_iota(I32, (W, LANES), 1)
    scale_n = scale_cols[N_HEADS - 1]
    for h in range(N_HEADS - 2, -1, -1):
        scale_n = jnp.where(lane == h, scale_cols[h], scale_n)
    nb[...] = scale_n * nb[...] + jnp.where(col_is_head, _dot(kw_t, jnp.ones((L, LANES), BF16)), 0.0)
    m_row[...] = m_next
    return hout


def _mlstm_mixer(proj, gates, bias_row, gain, bsz, seq, tile=MIX_TILE):
    t = proj.shape[0]
    nc = seq // tile
    row = lambda blk: pl.BlockSpec((tile, GROUP_WIDTH), lambda b, j, blk=blk: (b * nc + j, blk))
    const = lambda shape: pl.BlockSpec(shape, lambda b, j: (0,) * len(shape))
    return pl.pallas_call(
        _mlstm_kernel,
        grid=(bsz, nc),
        in_specs=[row(PB_LQ), row(PB_LK), row(PB_LV), row(PB_LO),
                  pl.BlockSpec((tile, LANES), lambda b, j: (b * nc + j, 0)),
                  const((1, LANES)), const((1, GROUP_WIDTH))],
        out_specs=pl.BlockSpec((tile, GROUP_WIDTH), lambda b, j: (b * nc + j, 0)),
        out_shape=jax.ShapeDtypeStruct((t, GROUP_WIDTH), BF16),
        scratch_shapes=[pltpu.VMEM((GROUP_WIDTH, GROUP_WIDTH), F32),
                        pltpu.VMEM((GROUP_WIDTH, LANES), F32),
                        pltpu.VMEM((1, LANES), F32)],
        compiler_params=_cparams("arbitrary", "arbitrary"),
        name="mlstm_mixer",
    )(proj, proj, proj, proj, gates, bias_row, gain)


def _outproj_router_kernel(x_ref, ya_ref, ys_ref, ym_ref, yl_ref, wo_ref, mod_ref, wn_ref, wr_ref, br_ref,
                           xo_ref, h2_ref, eid_ref, gcol_ref):
    W = GROUP_WIDTH
    tm = x_ref.shape[0]
    mod = mod_ref[0]
    out = _dot(ya_ref[...], wo_ref[0:W, :])
    out = out + _dot(ys_ref[...], wo_ref[W:2 * W, :])
    out = out + _dot(ym_ref[...], wo_ref[2 * W:3 * W, :])
    out = out + _dot(yl_ref[...], wo_ref[3 * W:4 * W, :])
    x1 = x_ref[...] + mod[2:3, :] * out
    xo_ref[...] = x1
    h2 = _rms_norm(x1, wn_ref[...]) * (1.0 + mod[4:5, :]) + mod[3:4, :]
    h2_ref[...] = _pack_bf16_pairs(h2)

    logits_t = (_dot(h2.astype(BF16), wr_ref[...]) + br_ref[...]).T
    row8 = lax.broadcasted_iota(I32, (SUBLANES, tm), 0).astype(F32)
    gl = jnp.where(row8 < N_EXPERT_GROUPS, logits_t[ROUTE_G:ROUTE_G + SUBLANES, :], NEG_INF)
    g_max = jnp.max(gl, axis=0, keepdims=True)
    g_sel = jnp.min(jnp.where(gl == g_max, row8, SUBLANES), axis=0, keepdims=True)
    g_prob = 1.0 / jnp.sum(jnp.exp(gl - g_max), axis=0, keepdims=True)
    el = logits_t[ROUTE_E:ROUTE_E + EXPERTS_PER_GROUP, :]
    for g in range(1, N_EXPERT_GROUPS):
        lo = ROUTE_E + g * EXPERTS_PER_GROUP
        el = jnp.where(g_sel == g, logits_t[lo:lo + EXPERTS_PER_GROUP, :], el)
    m1 = jnp.max(el, axis=0, keepdims=True)
    i1 = jnp.min(jnp.where(el == m1, row8, SUBLANES), axis=0, keepdims=True)
    el2 = jnp.where(row8 == i1, NEG_INF, el)
    m2 = jnp.max(el2, axis=0, keepdims=True)
    i2 = jnp.min(jnp.where(el2 == m2, row8, SUBLANES), axis=0, keepdims=True)
    ratio = jnp.exp(m2 - m1)
    p1 = 1.0 / (1.0 + ratio)
    eid_ref[0:1, :] = (g_sel * EXPERTS_PER_GROUP + i1).astype(I32)
    eid_ref[1:2, :] = (g_sel * EXPERTS_PER_GROUP + i2).astype(I32)
    rows = lax.broadcasted_iota(I32, (LANES, tm), 0)
    gate_rows = jnp.where(rows == 0, g_prob * p1, jnp.where(rows == 1, g_prob * p1 * ratio, 0.0))
    gcol_ref[...] = gate_rows.T


def _outproj_router(x2, ys, w_out, mod, w_norm2, w_route, b_route, seq, tm=512):
    t, d = x2.shape
    spb = seq // tm
    ytile = pl.BlockSpec((tm, GROUP_WIDTH), lambda i: (i, 0))
    return pl.pallas_call(
        _outproj_router_kernel,
        grid=(t // tm,),
        in_specs=[pl.BlockSpec((tm, d), lambda i: (i, 0)), ytile, ytile, ytile, ytile,
                  pl.BlockSpec((d, d), lambda i: (0, 0)),
                  pl.BlockSpec((1, 6, d), lambda i: (i // spb, 0, 0)),
                  pl.BlockSpec((1, d), lambda i: (0, 0)),
                  pl.BlockSpec((d, LANES), lambda i: (0, 0)),
                  pl.BlockSpec((1, LANES), lambda i: (0, 0))],
        out_specs=[pl.BlockSpec((tm, d), lambda i: (i, 0)),
                   pl.BlockSpec((tm, d // 2), lambda i: (i, 0)),
                   pl.BlockSpec((TOP_K, tm), lambda i: (0, i)),
                   pl.BlockSpec((tm, LANES), lambda i: (i, 0))],
        out_shape=[jax.ShapeDtypeStruct((t, d), F32),
                   jax.ShapeDtypeStruct((t, d // 2), jnp.uint32),
                   jax.ShapeDtypeStruct((TOP_K, t), I32),
                   jax.ShapeDtypeStruct((t, LANES), F32)],
        compiler_params=_cparams("arbitrary"),
        name="outproj_router",
    )(x2, *ys, w_out, mod, w_norm2, w_route, b_route)


def _rank_kernel(eid_ref, rank_ref, count_ref, carry):
    @pl.when(pl.program_id(0) == 0)
    def _():
        carry[...] = jnp.zeros_like(carry)

    tr = eid_ref.shape[1]
    expert = lax.broadcasted_iota(I32, (N_EXPERTS, tr), 0)
    before = (lax.broadcasted_iota(I32, (tr, tr), 0) < lax.broadcasted_iota(I32, (tr, tr), 1)).astype(BF16)
    base = carry[...]
    for k in range(TOP_K):
        onehot = (expert == eid_ref[k:k + 1, :]).astype(F32)
        prefix = _dot(onehot.astype(BF16), before)
        rank_ref[k:k + 1, :] = jnp.sum(onehot * (base + prefix), axis=0, keepdims=True).astype(I32)
        base = base + jnp.sum(onehot, axis=1, keepdims=True)
    carry[...] = base
    count_ref[...] = jnp.broadcast_to(base, count_ref.shape)


def _expert_ranks(eids, tr=512):
    t = eids.shape[1]
    return pl.pallas_call(
        _rank_kernel,
        grid=(t // tr,),
        in_specs=[pl.BlockSpec((TOP_K, tr), lambda i: (0, i))],
        out_specs=[pl.BlockSpec((TOP_K, tr), lambda i: (0, i)),
                   pl.BlockSpec((N_EXPERTS, LANES), lambda i: (0, 0))],
        out_shape=[jax.ShapeDtypeStruct((TOP_K, t), I32),
                   jax.ShapeDtypeStruct((N_EXPERTS, LANES), F32)],
        scratch_shapes=[pltpu.VMEM((N_EXPERTS, 1), F32)],
        compiler_params=_cparams("arbitrary"),
        name="expert_ranks",
    )(eids)


def _dest_kernel(pstart_ref, eid_ref, rank_ref, dest_ref):
    e = eid_ref[...]
    dest = rank_ref[...]
    for j in range(N_EXPERTS):
        dest = dest + jnp.where(e == j, pstart_ref[j], 0)
    dest_ref[...] = dest


def _dest_rows(p_starts, eids, ranks, tm=2048):
    t = eids.shape[1]
    grid_spec = pltpu.PrefetchScalarGridSpec(
        num_scalar_prefetch=1,
        grid=(t // tm,),
        in_specs=[pl.BlockSpec((TOP_K, tm), lambda i, ps: (0, i)),
                  pl.BlockSpec((TOP_K, tm), lambda i, ps: (0, i))],
        out_specs=pl.BlockSpec((TOP_K, tm), lambda i, ps: (0, i)))
    return pl.pallas_call(
        _dest_kernel,
        grid_spec=grid_spec,
        out_shape=jax.ShapeDtypeStruct((TOP_K, t), I32),
        compiler_params=_cparams("arbitrary"),
        name="dest_rows",
    )(p_starts, eids, ranks)


ROW_TILE = IDX_CHUNK // TOP_K


def _tile_indices(idx_hbm, idx_smem, idx_sems, tile, slot):
    return pltpu.make_async_copy(idx_hbm.at[pl.ds(tile * IDX_CHUNK, IDX_CHUNK)],
                                 idx_smem.at[pl.ds(slot * IDX_CHUNK, IDX_CHUNK)], idx_sems.at[slot])


def _dispatch_kernel(pend_ref, padded_ref, nu_ref, idx_hbm, h_ref, xb_hbm, idx_smem, zero_blk, idx_sems, row_sem,
                     zero_sem):
    def zero_block(start):
        return pltpu.make_async_copy(zero_blk, xb_hbm.at[pl.ds(pl.multiple_of(start, MOE_BLOCK), MOE_BLOCK)],
                                     zero_sem)

    @pl.when(pl.program_id(0) == 0)
    def _():
        zero_blk[...] = jnp.zeros_like(zero_blk)
        for e in range(N_EXPERTS):
            @pl.when(padded_ref[e] > 0)
            def _(e=e):
                zero_block(pend_ref[e] - MOE_BLOCK).start()
        for e in range(N_EXPERTS):
            @pl.when(padded_ref[e] > 0)
            def _(e=e):
                zero_block(pend_ref[e] - MOE_BLOCK).wait()

        def zero_unused(b, carry):
            copy = zero_block(b * MOE_BLOCK)
            copy.start()
            copy.wait()
            return carry

        lax.fori_loop(nu_ref[0], xb_hbm.shape[0] // MOE_BLOCK, zero_unused, 0)

    i = pl.program_id(0)

    @pl.when(i == 0)
    def _():
        _tile_indices(idx_hbm, idx_smem, idx_sems, i, 0).start()

    def row_copy(r, dst_row):
        return pltpu.make_async_copy(h_ref.at[pl.ds(r, 1)], xb_hbm.at[pl.ds(dst_row, 1)], row_sem)

    def step_for_slot(slot):
        _tile_indices(idx_hbm, idx_smem, idx_sems, i, slot).wait()

        @pl.when(i + 1 < pl.num_programs(0))
        def _():
            _tile_indices(idx_hbm, idx_smem, idx_sems, i + 1, 1 - slot).start()

        def issue(r, carry):
            for k in range(TOP_K):
                row_copy(r, idx_smem[slot * IDX_CHUNK + k * ROW_TILE + r]).start(priority=k)
            return carry

        lax.fori_loop(0, ROW_TILE, issue, 0, unroll=8)

    for parity in range(2):
        pl.when(i % 2 == parity)(functools.partial(step_for_slot, parity))

    def drain(r, carry):
        for k in range(TOP_K):
            row_copy(0, 0).wait()
        return carry

    lax.fori_loop(0, ROW_TILE, drain, 0, unroll=8)


def _dispatch(p_ends, padded, n_used, idx_tiles, h2, dst_rows):
    t, d = h2.shape
    any_spec = pl.BlockSpec(memory_space=pl.ANY)
    grid_spec = pltpu.PrefetchScalarGridSpec(
        num_scalar_prefetch=3,
        grid=(t // ROW_TILE,),
        in_specs=[any_spec, pl.BlockSpec((ROW_TILE, d), lambda i, pe, pd, nu: (i, 0))],
        out_specs=any_spec,
        scratch_shapes=[pltpu.SMEM((2 * IDX_CHUNK,), I32), pltpu.VMEM((MOE_BLOCK, d), h2.dtype),
                        pltpu.SemaphoreType.DMA((2,)), pltpu.SemaphoreType.DMA, pltpu.SemaphoreType.DMA])
    return pl.pallas_call(
        _dispatch_kernel,
        grid_spec=grid_spec,
        out_shape=jax.ShapeDtypeStruct((dst_rows, d), h2.dtype),
        compiler_params=_cparams("arbitrary"),
        name="moe_dispatch",
    )(p_ends, padded, n_used, idx_tiles, h2)


def _expert_kernel(be_ref, nu_ref, x_ref, wg_ref, wu_ref, wd_ref, y_ref, wg_b, wu_b, wd_b):
    b = pl.program_id(0)

    @pl.when(b < nu_ref[0])
    def _():
        @pl.when(jnp.logical_or(b == 0, be_ref[b] != be_ref[jnp.maximum(b - 1, 0)]))
        def _():
            wg_b[...] = wg_ref[0].astype(BF16)
            wu_b[...] = wu_ref[0].astype(BF16)
            wd_b[...] = wd_ref[0].astype(BF16)

        x = jnp.concatenate(_unpack_bf16_pairs(x_ref[...]), axis=1).astype(BF16)
        a = _silu(_dot(x, wg_b[...])) * _dot(x, wu_b[...])
        y_ref[...] = _pack_bf16_pairs(_dot(a.astype(BF16), wd_b[...]))

    @pl.when(b >= nu_ref[0])
    def _():
        y_ref[...] = jnp.zeros_like(y_ref)


def _expert_mlp(block_e, n_used, xb, w_gate, w_up, w_down):
    p, words = xb.shape
    d, de = w_gate.shape[1:]
    blk = lambda b, be, nu: (jnp.minimum(b, nu[0] - 1), 0)
    grid_spec = pltpu.PrefetchScalarGridSpec(
        num_scalar_prefetch=2,
        grid=(p // MOE_BLOCK,),
        in_specs=[pl.BlockSpec((MOE_BLOCK, words), blk),
                  pl.BlockSpec((1, d, de), lambda b, be, nu: (be[b], 0, 0)),
                  pl.BlockSpec((1, d, de), lambda b, be, nu: (be[b], 0, 0)),
                  pl.BlockSpec((1, de, d), lambda b, be, nu: (be[b], 0, 0))],
        out_specs=pl.BlockSpec((MOE_BLOCK, words), lambda b, be, nu: (b, 0)),
        scratch_shapes=[pltpu.VMEM((d, de), BF16), pltpu.VMEM((d, de), BF16), pltpu.VMEM((de, d), BF16)])
    return pl.pallas_call(
        _expert_kernel,
        grid_spec=grid_spec,
        out_shape=jax.ShapeDtypeStruct((p, words), jnp.uint32),
        compiler_params=_cparams("arbitrary"),
        name="expert_mlp",
    )(block_e, n_used, xb, w_gate, w_up, w_down)


def _combine_kernel(idx_hbm, yb_hbm, x_ref, gcol_ref, mod_ref, wnf_ref, o_ref, ybuf, idx_smem, idx_sems, row_sems,
                    *, final):
    i = pl.program_id(0)
    n = pl.num_programs(0)

    indices = functools.partial(_tile_indices, idx_hbm, idx_smem, idx_sems)

    def row_copy(src_row, slot, k, r):
        return pltpu.make_async_copy(yb_hbm.at[pl.ds(src_row, 1)], ybuf.at[slot, k, pl.ds(r, 1)],
                                     row_sems.at[slot])

    def issue_tile(slot):
        def issue(r, carry):
            for k in range(TOP_K):
                row_copy(idx_smem[slot * IDX_CHUNK + k * ROW_TILE + r], slot, k, r).start(priority=k)
            return carry

        lax.fori_loop(0, ROW_TILE, issue, 0, unroll=8)

    @pl.when(i == 0)
    def _():
        first = indices(i, 0)
        first.start()
        first.wait()
        issue_tile(0)

        @pl.when(n > 1)
        def _():
            indices(i + 1, 1).start()

    def step_for_slot(slot):
        other = 1 - slot

        @pl.when(i + 1 < n)
        def _():
            indices(i + 1, other).wait()

        @pl.when(i + 2 < n)
        def _():
            indices(i + 2, slot).start()

        @pl.when(i + 1 < n)
        def _():
            issue_tile(other)

        def drain(r, carry):
            for k in range(TOP_K):
                row_copy(0, slot, k, 0).wait()
            return carry

        lax.fori_loop(0, ROW_TILE, drain, 0, unroll=8)

        gc = gcol_ref[...]
        y0 = _unpack_bf16_pairs(ybuf[slot, 0])
        y1 = _unpack_bf16_pairs(ybuf[slot, 1])
        moe = jnp.concatenate([gc[:, 0:1] * y0[0] + gc[:, 1:2] * y1[0], gc[:, 0:1] * y0[1] + gc[:, 1:2] * y1[1]],
                              axis=1)
        x2 = x_ref[...] + mod_ref[0][5:6, :] * moe
        o_ref[...] = _rms_norm(x2, wnf_ref[...]) if final else x2

    for parity in range(2):
        pl.when(i % 2 == parity)(functools.partial(step_for_slot, parity))


def _combine(idx_tiles, yb, x2, gcol, mod, w_norm_final, seq, final):
    t, d = x2.shape
    tm = ROW_TILE
    spb = seq // tm
    any_spec = pl.BlockSpec(memory_space=pl.ANY)
    return pl.pallas_call(
        functools.partial(_combine_kernel, final=final),
        grid=(t // tm,),
        in_specs=[any_spec, any_spec,
                  pl.BlockSpec((tm, d), lambda i: (i, 0)),
                  pl.BlockSpec((tm, LANES), lambda i: (i, 0)),
                  pl.BlockSpec((1, 6, d), lambda i: (i // spb, 0, 0)),
                  pl.BlockSpec((1, d), lambda i: (0, 0))],
        out_specs=pl.BlockSpec((tm, d), lambda i: (i, 0)),
        out_shape=jax.ShapeDtypeStruct((t, d), F32),
        scratch_shapes=[pltpu.VMEM((2, TOP_K, tm, d // 2), jnp.uint32), pltpu.SMEM((2 * IDX_CHUNK,), I32),
                        pltpu.SemaphoreType.DMA((2,)), pltpu.SemaphoreType.DMA((2,))],
        compiler_params=_cparams("arbitrary"),
        name="moe_combine",
    )(idx_tiles, yb, x2, gcol, mod, w_norm_final)


def _lane_row(pieces, width=LANES):
    row = jnp.zeros((width,), F32)
    for off, vec in pieces.items():
        row = row.at[off:off + vec.shape[0]].set(vec.astype(F32))
    return row.reshape(1, width)


def _split_w_in(w_in):
    gw, nh = GROUP_WIDTH, N_HEADS
    widths = [gw, gw, gw, nh, gw, gw, gw, 3 * gw, nh, gw, gw, gw, gw, nh, nh]
    cuts, acc = [], 0
    for w in widths[:-1]:
        acc += w
        cuts.append(acc)
    (aq, ak, av, af, su, sv, mz, mxbc, mdt, lq, lk, lv, lo, li, lf) = jnp.split(w_in, cuts, axis=1)
    scale = HEAD_DIM ** -0.5
    wmain = jnp.concatenate([ak, su, sv, mz, mxbc, lq, lk * scale, lv, lo], axis=1).astype(BF16)
    wqt = (aq * scale).T.astype(BF16)
    wvt = av.T.astype(BF16)
    wg = jnp.zeros((w_in.shape[0], LANES), F32)
    for off, w in ((GATE_AF, af), (GATE_DT, mdt), (GATE_LI, li), (GATE_LF, lf)):
        wg = wg.at[:, off:off + nh].set(w)
    return wmain, wqt, wvt, wg.astype(BF16)


def _moe_layer(x1, h2, eids, gcol, mod_l, w_gate, w_up, w_down, layer, w_norm_final, seq, final):
    t, d = x1.shape
    ranks, counts = _expert_ranks(eids)
    counts = counts[:, 0].astype(I32)
    padded = ((counts + MOE_BLOCK - 1) // MOE_BLOCK) * MOE_BLOCK
    p_ends = jnp.cumsum(padded)
    p_starts = (p_ends - padded).astype(I32)
    n_blocks = (t * TOP_K) // MOE_BLOCK + N_EXPERTS
    blocks = jnp.arange(n_blocks, dtype=I32)
    block_e = jnp.sum((p_ends[None, :] <= (blocks * MOE_BLOCK)[:, None]).astype(I32), axis=1)
    block_e = jnp.minimum(block_e, N_EXPERTS - 1)
    n_used = (p_ends[-1:] // MOE_BLOCK).astype(I32)
    block_e = jnp.where(blocks < n_used, block_e, block_e[n_used[0] - 1])
    dest = _dest_rows(p_starts, eids, ranks)
    idx_tiles = dest.reshape(TOP_K, t // ROW_TILE, ROW_TILE).transpose(1, 0, 2).reshape(-1)
    xb = _dispatch(p_ends.astype(I32), padded.astype(I32), n_used, idx_tiles, h2, n_blocks * MOE_BLOCK)
    yb = _expert_mlp(block_e + layer * N_EXPERTS, n_used, xb, w_gate, w_up, w_down)
    return _combine(idx_tiles, yb, x1, gcol, mod_l, w_norm_final, seq, final)


def kernel(x, c, w_in, w_out, w_mix_norm, attn_f_bias, sg_w, sg_b, ssm_conv_w, ssm_conv_b, ssm_dt_bias,
           ssm_a_log, ssm_d, mlstm_i_bias, mlstm_f_bias, w_ada, b_ada, w_norm1, w_norm2, w_router_group,
           b_router_group, w_router_expert, b_router_expert, w_expert_gate, w_expert_up, w_expert_down,
           w_norm_final):
    bsz, seq, d = x.shape
    depth = w_in.shape[0]
    gw = GROUP_WIDTH
    mod = _ada_modulation(c, w_ada, b_ada).reshape(depth, bsz, 6, d)
    x2 = x.reshape(bsz * seq, d)
    wnf = w_norm_final.reshape(1, d)
    w_eg = w_expert_gate.reshape((depth * N_EXPERTS,) + w_expert_gate.shape[2:])
    w_eu = w_expert_up.reshape((depth * N_EXPERTS,) + w_expert_up.shape[2:])
    w_ed = w_expert_down.reshape((depth * N_EXPERTS,) + w_expert_down.shape[2:])
    for l in range(depth):
        wmain, wqt, wvt, wg = _split_w_in(w_in[l])
        gains = w_mix_norm[l].reshape(N_HEADS, 1, gw)
        proj, qt, vt, gates, f2 = _inproj(x2, mod[l], w_norm1[l].reshape(1, d), wmain, wqt, wvt, wg,
                                          _lane_row({GATE_AF: attn_f_bias[l]}), bsz, seq)
        y_attn = _attention(proj, qt, vt, f2, gains[0], bsz, seq)
        sg_bias_full = jnp.repeat(sg_b[l].T, HEAD_DIM, axis=1)
        y_sg = _spatial_gating(proj, gains[1], sg_w[l], sg_bias_full)
        y_ssm = _ssd_mixer(proj, gates, ssm_conv_w[l], ssm_conv_b[l].reshape(1, -1),
                           _lane_row({GATE_DT: ssm_dt_bias[l]}), _lane_row({GATE_DT: ssm_a_log[l]}),
                           jnp.repeat(ssm_d[l], HEAD_DIM).reshape(1, gw), gains[2], bsz, seq)
        y_ml = _mlstm_mixer(proj, gates, _lane_row({GATE_LI: mlstm_i_bias[l], GATE_LF: mlstm_f_bias[l]}),
                            gains[3], bsz, seq)
        w_route = jnp.zeros((d, LANES), F32)
        w_route = w_route.at[:, ROUTE_G:ROUTE_G + N_EXPERT_GROUPS].set(w_router_group[l])
        w_route = w_route.at[:, ROUTE_E:ROUTE_E + N_EXPERTS].set(w_router_expert[l]).astype(BF16)
        b_route = _lane_row({ROUTE_G: b_router_group[l], ROUTE_E: b_router_expert[l]})
        x1, h2, eids, gcol = _outproj_router(x2, (y_attn, y_sg, y_ssm, y_ml), w_out[l].astype(BF16), mod[l],
                                             w_norm2[l].reshape(1, d), w_route, b_route, seq)
        x2 = _moe_layer(x1, h2, eids, gcol, mod[l], w_eg, w_eu, w_ed, l, wnf, seq, final=(l == depth - 1))
    return x2.reshape(bsz, seq, d)
```

```python
import functools

import jax
import jax.numpy as jnp
from jax import lax
from jax.experimental import pallas as pl
from jax.experimental.pallas import tpu as pltpu

F32 = jnp.float32
BF16 = jnp.bfloat16
I32 = jnp.int32

LANES = 128
SUBLANES = 8
HEAD_DIM = 64
N_HEADS = 4
GROUP_WIDTH = 256
CHUNK = 128
MIX_TILE = 4 * CHUNK
SSM_GROUPS = 2
CONV_WIDTH = 4
N_EXPERT_GROUPS = 4
EXPERTS_PER_GROUP = 8
N_EXPERTS = N_EXPERT_GROUPS * EXPERTS_PER_GROUP
TOP_K = 2
MOE_BLOCK = 512
NORM_EPS = 1e-6
MLSTM_M_INIT = -1e30
NEG_INF = float("-inf")
VMEM_LIMIT_BYTES = 48 * 1024 * 1024
IDX_CHUNK = 1024

(PB_K, PB_SU, PB_SV, PB_MZ, PB_X, PB_B, PB_C, PB_LQ, PB_LK, PB_LV, PB_LO) = range(11)
PROJ_COLS = 11 * GROUP_WIDTH
LOG2_E = 1.4426950408889634
GATE_AF, GATE_DT, GATE_LI, GATE_LF = 0, 4, 8, 12
ROUTE_G, ROUTE_E = 0, 8

NT_DIMS = (((1,), (1,)), ((), ()))


def _cparams(*sem):
    return pltpu.CompilerParams(dimension_semantics=sem, vmem_limit_bytes=VMEM_LIMIT_BYTES)


def _dot(a, b):
    return jnp.dot(a, b, preferred_element_type=F32)


def _dot_nt(a, b):
    return lax.dot_general(a, b, NT_DIMS, preferred_element_type=F32)


def _dot_exact(a, b):
    return jnp.dot(a, b, preferred_element_type=F32, precision=lax.Precision.HIGHEST)


def _head_of_lane(shape, axis=1):
    return lax.broadcasted_iota(I32, shape, axis) // HEAD_DIM


def _sigmoid(x):
    return 1.0 / (1.0 + jnp.exp(-x))


def _silu(x):
    return x * _sigmoid(x)


def _log_sigmoid(x):
    return jnp.minimum(x, 0.0) - jnp.log1p(jnp.exp(-jnp.abs(x)))


def _softplus(x):
    return jnp.maximum(x, 0.0) + jnp.log1p(jnp.exp(-jnp.abs(x)))


def _gelu_tanh(x):
    return 0.5 * x * (1.0 + jnp.tanh(0.7978845608028654 * (x + 0.044715 * (x * x * x))))


def _expand_heads(cols, width=GROUP_WIDTH):
    rows = cols[0].shape[0]
    head = _head_of_lane((rows, width))
    out = jnp.broadcast_to(cols[N_HEADS - 1], (rows, width))
    for h in range(N_HEADS - 2, -1, -1):
        out = jnp.where(head == h, jnp.broadcast_to(cols[h], (rows, width)), out)
    return out


def _mask_head(x, h):
    return jnp.where(_head_of_lane(x.shape) == h, x, jnp.zeros_like(x))


def _head_rms_norm(y, gain):
    head = _head_of_lane(y.shape)
    sq = y * y
    cols = [jnp.sum(jnp.where(head == h, sq, 0.0), axis=1, keepdims=True) * (1.0 / HEAD_DIM)
            for h in range(N_HEADS)]
    return y * lax.rsqrt(_expand_heads(cols) + NORM_EPS) * gain


def _rms_norm(x, gain):
    ms = jnp.mean(x * x, axis=1, keepdims=True)
    return x * lax.rsqrt(ms + NORM_EPS) * gain


def _pack_bf16_pairs(x):
    half = x.shape[1] // 2
    bits = lax.bitcast_convert_type(x.astype(BF16).astype(F32), jnp.uint32)
    return bits[:, :half] | lax.shift_right_logical(bits[:, half:], jnp.uint32(16))


def _unpack_bf16_pairs(words):
    first = lax.bitcast_convert_type(words & jnp.uint32(0xFFFF0000), F32)
    second = lax.bitcast_convert_type(lax.shift_left(words, jnp.uint32(16)), F32)
    return first, second


def _cumsum_rows(x):
    tri = jnp.where(_tril(x.shape[0]), 1.0, 0.0).astype(BF16)
    hi = x.astype(BF16)
    rest = x - hi.astype(F32)
    mid = rest.astype(BF16)
    lo = (rest - mid.astype(F32)).astype(BF16)
    return _dot(tri, hi) + _dot(tri, mid) + _dot(tri, lo)


def _tril(n, strict=False):
    r = lax.broadcasted_iota(I32, (n, n), 0)
    c = lax.broadcasted_iota(I32, (n, n), 1)
    return (r > c) if strict else (r >= c)


def _ada_kernel(c_ref, w_ref, b_ref, o_ref):
    o_ref[0] = _dot_exact(_silu(c_ref[...]), w_ref[0]) + b_ref[0]


def _ada_modulation(c, w_ada, b_ada):
    depth, d, d6 = w_ada.shape
    bsz = c.shape[0]
    return pl.pallas_call(
        _ada_kernel,
        grid=(depth, d6 // d),
        in_specs=[pl.BlockSpec((bsz, d), lambda l, j: (0, 0)),
                  pl.BlockSpec((1, d, d), lambda l, j: (l, 0, j)),
                  pl.BlockSpec((1, 1, d), lambda l, j: (l, 0, j))],
        out_specs=pl.BlockSpec((1, bsz, d), lambda l, j: (l, 0, j)),
        out_shape=jax.ShapeDtypeStruct((depth, bsz, d6), F32),
        compiler_params=_cparams("arbitrary", "arbitrary"),
        name="ada_modulation",
    )(c, w_ada, b_ada.reshape(depth, 1, d6))


def _forget_bias_slab(gates, carry):
    tb = gates.shape[0]
    cum = _cumsum_rows(_log_sigmoid(gates)) + carry[...]
    carry[...] = cum[tb - 1:tb, :]
    f2 = cum * LOG2_E
    hi = f2.astype(BF16)
    rest = f2 - hi.astype(F32)
    mid = rest.astype(BF16)
    lo = (rest - mid.astype(F32)).astype(BF16)
    r = lax.broadcasted_iota(I32, (LANES, GROUP_WIDTH), 0) - GATE_AF
    c = lax.broadcasted_iota(I32, (LANES, GROUP_WIDTH), 1)
    slab = jnp.zeros((tb, GROUP_WIDTH), F32)
    for j, piece in enumerate((hi, mid, lo)):
        place = (r >= 0) & (r < N_HEADS) & (c == ((r + 1) % N_HEADS) * HEAD_DIM + j)
        slab = slab + _dot(piece, jnp.where(place, -1.0, 0.0).astype(BF16))
    return slab.astype(BF16)


def _inproj_kernel(x_ref, mod_ref, wn_ref, wmain_ref, wqt_ref, wvt_ref, wg_ref, fbias_ref,
                   proj_ref, qt_ref, vt_ref, gates_ref, slab_ref, h_scr, f_carry, *, tiles_per_seq):
    @pl.when(pl.program_id(0) % tiles_per_seq == 0)
    def _():
        f_carry[...] = jnp.zeros_like(f_carry)

    mod = mod_ref[0]
    h = _rms_norm(x_ref[...], wn_ref[...]) * (1.0 + mod[1:2, :]) + mod[0:1, :]
    h_scr[...] = h.astype(BF16)
    for c0 in range(0, PROJ_COLS, GROUP_WIDTH):
        proj_ref[:, c0:c0 + GROUP_WIDTH] = _dot(h_scr[...], wmain_ref[:, c0:c0 + GROUP_WIDTH]).astype(BF16)
    qt_ref[0] = (_dot_nt(wqt_ref[...], h_scr[...]) * LOG2_E).astype(BF16)
    vt_ref[0] = _dot_nt(wvt_ref[...], h_scr[...]).astype(BF16)
    gates = _dot(h_scr[...], wg_ref[...])
    gates_ref[...] = gates
    slab_ref[...] = _forget_bias_slab(gates + fbias_ref[...], f_carry)


def _inproj(x2, mod, w_norm, wmain, wqt, wvt, wg, f_bias_row, bsz, seq, tm=512):
    t, d = x2.shape
    spb = seq // tm
    tspec = pl.BlockSpec((1, GROUP_WIDTH, tm), lambda i: (i // spb, 0, i % spb))
    return pl.pallas_call(
        functools.partial(_inproj_kernel, tiles_per_seq=spb),
        grid=(t // tm,),
        in_specs=[pl.BlockSpec((tm, d), lambda i: (i, 0)),
                  pl.BlockSpec((1, 6, d), lambda i: (i // spb, 0, 0)),
                  pl.BlockSpec((1, d), lambda i: (0, 0)),
                  pl.BlockSpec((d, PROJ_COLS), lambda i: (0, 0)),
                  pl.BlockSpec((GROUP_WIDTH, d), lambda i: (0, 0)),
                  pl.BlockSpec((GROUP_WIDTH, d), lambda i: (0, 0)),
                  pl.BlockSpec((d, LANES), lambda i: (0, 0)),
                  pl.BlockSpec((1, LANES), lambda i: (0, 0))],
        out_specs=[pl.BlockSpec((tm, PROJ_COLS), lambda i: (i, 0)), tspec, tspec,
                   pl.BlockSpec((tm, LANES), lambda i: (i, 0)),
                   pl.BlockSpec((tm, GROUP_WIDTH), lambda i: (i, 0))],
        out_shape=[jax.ShapeDtypeStruct((t, PROJ_COLS), BF16),
                   jax.ShapeDtypeStruct((bsz, GROUP_WIDTH, seq), BF16),
                   jax.ShapeDtypeStruct((bsz, GROUP_WIDTH, seq), BF16),
                   jax.ShapeDtypeStruct((t, LANES), F32),
                   jax.ShapeDtypeStruct((t, GROUP_WIDTH), BF16)],
        scratch_shapes=[pltpu.VMEM((tm, d), BF16), pltpu.VMEM((1, LANES), F32)],
        compiler_params=_cparams("arbitrary"),
        name="norm_inproj",
    )(x2, mod, w_norm, wmain, wqt, wvt, wg, f_bias_row)


ONES_ROWS = 16


def _attn_kernel(qi_ref, kj_ref, k_ref, qt_ref, vt_ref, f_ref, gain_ref, o_ref, qtm, m_s, l_s, acc):
    p = pl.program_id(1)
    qi, kj = qi_ref[p], kj_ref[p]
    tk = k_ref.shape[0]
    tq = qt_ref.shape[2]

    @pl.when(kj == 0)
    def _():
        qt = qt_ref[0]
        row = lax.broadcasted_iota(I32, qt.shape, 0)
        for h in range(N_HEADS):
            slot = ((h + 1) % N_HEADS) * HEAD_DIM
            ones_rows = jnp.where((row >= slot) & (row < slot + 3), 1.0, 0.0).astype(BF16)
            qtm[h] = jnp.where(row // HEAD_DIM == h, qt, ones_rows)
        m_s[...] = jnp.full_like(m_s, NEG_INF)
        l_s[...] = jnp.zeros_like(l_s)
        acc[...] = jnp.zeros_like(acc)

    def step(diagonal):
        k = k_ref[...]
        bias = f_ref[...]
        head = _head_of_lane(k.shape)
        ones = jnp.ones((ONES_ROWS, tk), BF16)
        if diagonal:
            visible = lax.broadcasted_iota(I32, (tk, tq), 0) <= lax.broadcasted_iota(I32, (tk, tq), 1)
        scores, m_news, alphas, probs = {}, {}, {}, {}

        def score(h):
            s = _dot(jnp.where(head == h, k, bias), qtm[h])
            if diagonal:
                s = jnp.where(visible, s, NEG_INF)
            scores[h] = s
            m_prev = m_s[h]
            m_news[h] = jnp.maximum(m_prev, jnp.max(s, axis=0, keepdims=True))
            alphas[h] = jnp.exp2(m_prev - m_news[h])
            m_s[h] = m_news[h]

        def prob(h):
            probs[h] = jnp.exp2(scores[h] - m_news[h]).astype(BF16)

        def weighted_sum(h):
            vt_ext = jnp.concatenate([vt_ref[0, h * HEAD_DIM:(h + 1) * HEAD_DIM, :], ones], axis=0)
            pv = _dot(vt_ext, probs[h])
            acc[h] = alphas[h] * acc[h] + pv[0:HEAD_DIM, :]
            l_s[h] = alphas[h] * l_s[h] + pv[HEAD_DIM:HEAD_DIM + 1, :]

        for stage in range(N_HEADS + 2):
            if stage < N_HEADS:
                score(stage)
            if 0 <= stage - 1 < N_HEADS:
                prob(stage - 1)
            if 0 <= stage - 2 < N_HEADS:
                weighted_sum(stage - 2)

    @pl.when(kj < qi)
    def _():
        step(False)

    @pl.when(kj == qi)
    def _():
        step(True)
        yt = jnp.concatenate([acc[h] / l_s[h] for h in range(N_HEADS)], axis=0)
        o_ref[...] = _head_rms_norm(yt.T, gain_ref[...]).astype(BF16)


def _attention(proj, qt, vt, f2, gain, bsz, seq, tq=1024):
    nq = seq // tq
    pairs = [(i, j) for i in range(nq) for j in range(i + 1)]
    qi = jnp.asarray([a for a, _ in pairs], I32)
    kj = jnp.asarray([b for _, b in pairs], I32)
    t = bsz * seq
    grid_spec = pltpu.PrefetchScalarGridSpec(
        num_scalar_prefetch=2,
        grid=(bsz, len(pairs)),
        in_specs=[pl.BlockSpec((tq, GROUP_WIDTH), lambda b, p, qi, kj: (b * nq + kj[p], PB_K)),
                  pl.BlockSpec((1, GROUP_WIDTH, tq), lambda b, p, qi, kj: (b, 0, qi[p])),
                  pl.BlockSpec((1, GROUP_WIDTH, tq), lambda b, p, qi, kj: (b, 0, kj[p])),
                  pl.BlockSpec((tq, GROUP_WIDTH), lambda b, p, qi, kj: (b * nq + kj[p], 0)),
                  pl.BlockSpec((1, GROUP_WIDTH), lambda b, p, qi, kj: (0, 0))],
        out_specs=pl.BlockSpec((tq, GROUP_WIDTH), lambda b, p, qi, kj: (b * nq + qi[p], 0)),
        scratch_shapes=[pltpu.VMEM((N_HEADS, GROUP_WIDTH, tq), BF16),
                        pltpu.VMEM((N_HEADS, 1, tq), F32),
                        pltpu.VMEM((N_HEADS, 1, tq), F32),
                        pltpu.VMEM((N_HEADS, HEAD_DIM, tq), F32)])
    return pl.pallas_call(
        _attn_kernel,
        grid_spec=grid_spec,
        out_shape=jax.ShapeDtypeStruct((t, GROUP_WIDTH), BF16),
        compiler_params=_cparams("arbitrary", "arbitrary"),
        name="fox_attention",
    )(qi, kj, proj, qt, vt, f2, gain)


def _sg_kernel(u_ref, v_ref, gain_ref, w_ref, b_ref, o_ref):
    tm = u_ref.shape[0]
    u = _gelu_tanh(u_ref[...].astype(F32))
    v = _head_rms_norm(_gelu_tanh(v_ref[...].astype(F32)), gain_ref[...])
    causal = _tril(CHUNK)
    ws = [jnp.where(causal, w_ref[h], 0.0).astype(BF16) for h in range(N_HEADS)]
    for c0 in range(0, tm, CHUNK):
        vc = v[c0:c0 + CHUNK, :].astype(BF16)
        mixed = b_ref[...]
        for h in range(N_HEADS):
            mixed = mixed + _dot(ws[h], _mask_head(vc, h))
        o_ref[c0:c0 + CHUNK, :] = (u[c0:c0 + CHUNK, :] * mixed).astype(BF16)


def _spatial_gating(proj, gain, sg_w, sg_bias_full, tm=512):
    t = proj.shape[0]
    return pl.pallas_call(
        _sg_kernel,
        grid=(t // tm,),
        in_specs=[pl.BlockSpec((tm, GROUP_WIDTH), lambda i: (i, PB_SU)),
                  pl.BlockSpec((tm, GROUP_WIDTH), lambda i: (i, PB_SV)),
                  pl.BlockSpec((1, GROUP_WIDTH), lambda i: (0, 0)),
                  pl.BlockSpec((N_HEADS, CHUNK, CHUNK), lambda i: (0, 0, 0)),
                  pl.BlockSpec((CHUNK, GROUP_WIDTH), lambda i: (0, 0))],
        out_specs=pl.BlockSpec((tm, GROUP_WIDTH), lambda i: (i, 0)),
        out_shape=jax.ShapeDtypeStruct((t, GROUP_WIDTH), BF16),
        compiler_params=_cparams("arbitrary"),
        name="spatial_gating",
    )(proj, proj, gain, sg_w, sg_bias_full)


def _ssd_kernel(z_ref, x_ref, b_ref, c_ref, g_ref, cw_ref, cb_ref, dtb_ref, alog_ref, dskip_ref, gain_ref,
                o_ref, conv_scr, xbc_scr, state):
    L = CHUNK
    W = GROUP_WIDTH
    tile = x_ref.shape[0]

    @pl.when(pl.program_id(1) == 0)
    def _():
        conv_scr[0:SUBLANES, :] = jnp.zeros((SUBLANES, 3 * W), F32)
        state[...] = jnp.zeros_like(state)

    conv_scr[SUBLANES:, 0:W] = x_ref[...].astype(F32)
    conv_scr[SUBLANES:, W:2 * W] = b_ref[...].astype(F32)
    conv_scr[SUBLANES:, 2 * W:] = c_ref[...].astype(F32)
    cw = cw_ref[...]
    conv = cb_ref[...] + cw[CONV_WIDTH - 1:CONV_WIDTH, :] * conv_scr[SUBLANES:, :]
    for s in range(1, CONV_WIDTH):
        conv = conv + cw[CONV_WIDTH - 1 - s:CONV_WIDTH - s, :] * conv_scr[SUBLANES - s:SUBLANES - s + tile, :]
    conv_scr[0:SUBLANES, :] = conv_scr[tile:tile + SUBLANES, :]
    xbc_scr[...] = _silu(conv)

    neg_a = -jnp.exp(alog_ref[...])
    for c0 in range(0, tile, L):
        rows = slice(c0, c0 + L)
        dt = _softplus(g_ref[rows, :] + dtb_ref[...])
        y = _ssd_chunk(xbc_scr[rows, 0:W], xbc_scr[rows, W:2 * W], xbc_scr[rows, 2 * W:], dt, dt * neg_a,
                       dskip_ref[...], state)
        y = y * _silu(z_ref[rows, :].astype(F32))
        o_ref[rows, :] = _head_rms_norm(y, gain_ref[...]).astype(BF16)


def _ssd_chunk(xs, bm, cm, dt, da, dskip, state):
    L = CHUNK
    a_cum = _cumsum_rows(da)
    a_row = a_cum.T
    dt_cols = [dt[:, GATE_DT + h:GATE_DT + h + 1] for h in range(N_HEADS)]
    a_cols = [a_cum[:, GATE_DT + h:GATE_DT + h + 1] for h in range(N_HEADS)]
    a_end = [a_cum[L - 1:L, GATE_DT + h:GATE_DT + h + 1] for h in range(N_HEADS)]
    xdt = xs * _expand_heads(dt_cols)
    xdt_b = xdt.astype(BF16)
    xw = (xdt * _expand_heads([jnp.exp(a_end[h] - a_cols[h]) for h in range(N_HEADS)])).astype(BF16)
    exp_a = _expand_heads([jnp.exp(a_cols[h]) for h in range(N_HEADS)])
    causal = _tril(L)
    half = lax.broadcasted_iota(I32, (1, LANES), 1) < HEAD_DIM

    y = dskip * xs
    y_off = []
    for g in range(SSM_GROUPS):
        bg = bm[:, g * LANES:(g + 1) * LANES]
        cg = cm[:, g * LANES:(g + 1) * LANES].astype(BF16)
        scores = _dot_nt(cg, bg.astype(BF16))
        for r in range(N_HEADS // SSM_GROUPS):
            h = g * (N_HEADS // SSM_GROUPS) + r
            seg = a_cols[h] - a_row[GATE_DT + h:GATE_DT + h + 1, :]
            decay = jnp.exp(jnp.where(causal, seg, NEG_INF))
            y = y + _dot((scores * decay).astype(BF16), _mask_head(xdt_b, h))
        st_in = state[g]
        y_off.append(_dot(cg, st_in.astype(BF16)))
        chunk_decay = jnp.where(half, jnp.exp(a_end[2 * g]), jnp.exp(a_end[2 * g + 1]))
        state[g] = chunk_decay * st_in + _dot(bg.T.astype(BF16), xw[:, g * LANES:(g + 1) * LANES])
    return y + jnp.concatenate(y_off, axis=1) * exp_a


def _ssd_mixer(proj, gates, conv_w, conv_b, dtb_row, alog_row, dskip_row, gain, bsz, seq, tile=MIX_TILE):
    t = proj.shape[0]
    nc = seq // tile
    row = lambda blk: pl.BlockSpec((tile, GROUP_WIDTH), lambda b, j, blk=blk: (b * nc + j, blk))
    const = lambda shape: pl.BlockSpec(shape, lambda b, j: (0,) * len(shape))
    return pl.pallas_call(
        _ssd_kernel,
        grid=(bsz, nc),
        in_specs=[row(PB_MZ), row(PB_X), row(PB_B), row(PB_C),
                  pl.BlockSpec((tile, LANES), lambda b, j: (b * nc + j, 0)),
                  const((CONV_WIDTH, 3 * GROUP_WIDTH)), const((1, 3 * GROUP_WIDTH)),
                  const((1, LANES)), const((1, LANES)), const((1, GROUP_WIDTH)), const((1, GROUP_WIDTH))],
        out_specs=pl.BlockSpec((tile, GROUP_WIDTH), lambda b, j: (b * nc + j, 0)),
        out_shape=jax.ShapeDtypeStruct((t, GROUP_WIDTH), BF16),
        scratch_shapes=[pltpu.VMEM((tile + SUBLANES, 3 * GROUP_WIDTH), F32),
                        pltpu.VMEM((tile, 3 * GROUP_WIDTH), F32),
                        pltpu.VMEM((SSM_GROUPS, LANES, LANES), F32)],
        compiler_params=_cparams("arbitrary", "arbitrary"),
        name="ssd_mixer",
    )(proj, proj, proj, proj, gates, conv_w, conv_b, dtb_row, alog_row, dskip_row, gain)


def _mlstm_kernel(q_ref, k_ref, v_ref, o_gate_ref, g_ref, bias_ref, gain_ref, o_ref, ct, nb, m_row):
    L = CHUNK
    W = GROUP_WIDTH

    @pl.when(pl.program_id(1) == 0)
    def _():
        ct[...] = jnp.zeros_like(ct)
        nb[...] = jnp.zeros_like(nb)
        m_row[...] = jnp.full_like(m_row, MLSTM_M_INIT)

    def chunk(c, carry):
        rows = pl.ds(pl.multiple_of(c * L, L), L)
        y = _mlstm_chunk(q_ref[rows, :], k_ref[rows, :], v_ref[rows, :], g_ref[rows, :] + bias_ref[...],
                         ct, nb, m_row)
        y = _sigmoid(o_gate_ref[rows, :].astype(F32)) * _head_rms_norm(y, gain_ref[...])
        o_ref[rows, :] = y.astype(BF16)
        return carry

    lax.fori_loop(0, q_ref.shape[0] // L, chunk, 0)


def _mlstm_chunk(q, k, v, gate, ct, nb, m_row):
    L = CHUNK
    W = GROUP_WIDTH
    a_full = _cumsum_rows(_log_sigmoid(gate))
    a_rows = a_full.T
    g_rows = gate.T
    causal = _tril(L)
    lane = lax.broadcasted_iota(I32, (1, LANES), 1)

    inter_q = _dot(q, ct[...].astype(BF16))
    n_q = _dot(q, nb[...].astype(BF16))
    m_old = m_row[...]
    num = jnp.zeros((L, W), F32)
    inter_cols, den_cols, ws_cols, scale_cols = [], [], [], []
    m_next = m_old
    for h in range(N_HEADS):
        a_col = a_full[:, GATE_LF + h:GATE_LF + h + 1]
        i_col = gate[:, GATE_LI + h:GATE_LI + h + 1]
        a_r = a_rows[GATE_LF + h:GATE_LF + h + 1, :]
        i_r = g_rows[GATE_LI + h:GATE_LI + h + 1, :]
        a_end = a_full[L - 1:L, GATE_LF + h:GATE_LF + h + 1]
        m_in = m_old[:, h:h + 1]
        log_d = jnp.where(causal, a_col - a_r + i_r, NEG_INF)
        log_inter = a_col + m_in
        m_t = jnp.maximum(jnp.max(log_d, axis=1, keepdims=True), log_inter)
        w = _dot_nt(_mask_head(q, h), k) * jnp.exp(log_d - m_t)
        inter = jnp.exp(log_inter - m_t)
        num = num + _dot(w.astype(BF16), _mask_head(v, h))
        den = jnp.sum(w, axis=1, keepdims=True) + inter * n_q[:, h:h + 1]
        inter_cols.append(inter)
        den_cols.append(jnp.maximum(jnp.abs(den), jnp.exp(-m_t)))
        g_col = a_end - a_col + i_col
        m_new = jnp.maximum(a_end + m_in, jnp.max(g_col, axis=0, keepdims=True))
        ws_cols.append(jnp.exp(g_col - m_new))
        scale_cols.append(jnp.exp(a_end + m_in - m_new))
        m_next = jnp.where(lane == h, m_new, m_next)
    hout = (num + _expand_heads(inter_cols) * inter_q) / _expand_heads(den_cols)

    kw_t = (k.astype(F32) * _expand_heads(ws_cols)).T.astype(BF16)
    scale_row = _expand_heads(scale_cols)
    same_head = _head_of_lane((W, W), 0) == _head_of_lane((W, W), 1)
    ct[...] = scale_row * ct[...] + jnp.where(same_head, _dot(kw_t, v), 0.0)
    col_is_head = _head_of_lane((W, LANES), 0) == lax.broadcasted_iota(I32, (W, LANES), 1)
    scale_n = scale_cols[N_HEADS - 1]
    for h in range(N_HEADS - 2, -1, -1):
        scale_n = jnp.where(lane == h, scale_cols[h], scale_n)
    nb[...] = scale_n * nb[...] + jnp.where(col_is_head, _dot(kw_t, jnp.ones((L, LANES), BF16)), 0.0)
    m_row[...] = m_next
    return hout


def _mlstm_mixer(proj, gates, bias_row, gain, bsz, seq, tile=MIX_TILE):
    t = proj.shape[0]
    nc = seq // tile
    row = lambda blk: pl.BlockSpec((tile, GROUP_WIDTH), lambda b, j, blk=blk: (b * nc + j, blk))
    const = lambda shape: pl.BlockSpec(shape, lambda b, j: (0,) * len(shape))
    return pl.pallas_call(
        _mlstm_kernel,
        grid=(bsz, nc),
        in_specs=[row(PB_LQ), row(PB_LK), row(PB_LV), row(PB_LO),
                  pl.BlockSpec((tile, LANES), lambda b, j: (b * nc + j, 0)),
                  const((1, LANES)), const((1, GROUP_WIDTH))],
        out_specs=pl.BlockSpec((tile, GROUP_WIDTH), lambda b, j: (b * nc + j, 0)),
        out_shape=jax.ShapeDtypeStruct((t, GROUP_WIDTH), BF16),
        scratch_shapes=[pltpu.VMEM((GROUP_WIDTH, GROUP_WIDTH), F32),
                        pltpu.VMEM((GROUP_WIDTH, LANES), F32),
                        pltpu.VMEM((1, LANES), F32)],
        compiler_params=_cparams("arbitrary", "arbitrary"),
        name="mlstm_mixer",
    )(proj, proj, proj, proj, gates, bias_row, gain)


def _outproj_router_kernel(x_ref, ya_ref, ys_ref, ym_ref, yl_ref, wo_ref, mod_ref, wn_ref, wr_ref, br_ref,
                           xo_ref, h2_ref, eid_ref, gcol_ref):
    W = GROUP_WIDTH
    tm = x_ref.shape[0]
    mod = mod_ref[0]
    out = _dot(ya_ref[...], wo_ref[0:W, :])
    out = out + _dot(ys_ref[...], wo_ref[W:2 * W, :])
    out = out + _dot(ym_ref[...], wo_ref[2 * W:3 * W, :])
    out = out + _dot(yl_ref[...], wo_ref[3 * W:4 * W, :])
    x1 = x_ref[...] + mod[2:3, :] * out
    xo_ref[...] = x1
    h2 = _rms_norm(x1, wn_ref[...]) * (1.0 + mod[4:5, :]) + mod[3:4, :]
    h2_ref[...] = _pack_bf16_pairs(h2)

    logits_t = (_dot(h2.astype(BF16), wr_ref[...]) + br_ref[...]).T
    row8 = lax.broadcasted_iota(I32, (SUBLANES, tm), 0).astype(F32)
    gl = jnp.where(row8 < N_EXPERT_GROUPS, logits_t[ROUTE_G:ROUTE_G + SUBLANES, :], NEG_INF)
    g_max = jnp.max(gl, axis=0, keepdims=True)
    g_sel = jnp.min(jnp.where(gl == g_max, row8, SUBLANES), axis=0, keepdims=True)
    g_prob = 1.0 / jnp.sum(jnp.exp(gl - g_max), axis=0, keepdims=True)
    el = logits_t[ROUTE_E:ROUTE_E + EXPERTS_PER_GROUP, :]
    for g in range(1, N_EXPERT_GROUPS):
        lo = ROUTE_E + g * EXPERTS_PER_GROUP
        el = jnp.where(g_sel == g, logits_t[lo:lo + EXPERTS_PER_GROUP, :], el)
    m1 = jnp.max(el, axis=0, keepdims=True)
    i1 = jnp.min(jnp.where(el == m1, row8, SUBLANES), axis=0, keepdims=True)
    el2 = jnp.where(row8 == i1, NEG_INF, el)
    m2 = jnp.max(el2, axis=0, keepdims=True)
    i2 = jnp.min(jnp.where(el2 == m2, row8, SUBLANES), axis=0, keepdims=True)
    ratio = jnp.exp(m2 - m1)
    p1 = 1.0 / (1.0 + ratio)
    eid_ref[0:1, :] = (g_sel * EXPERTS_PER_GROUP + i1).astype(I32)
    eid_ref[1:2, :] = (g_sel * EXPERTS_PER_GROUP + i2).astype(I32)
    rows = lax.broadcasted_iota(I32, (LANES, tm), 0)
    gate_rows = jnp.where(rows == 0, g_prob * p1, jnp.where(rows == 1, g_prob * p1 * ratio, 0.0))
    gcol_ref[...] = gate_rows.T


def _outproj_router(x2, ys, w_out, mod, w_norm2, w_route, b_route, seq, tm=512):
    t, d = x2.shape
    spb = seq // tm
    ytile = pl.BlockSpec((tm, GROUP_WIDTH), lambda i: (i, 0))
    return pl.pallas_call(
        _outproj_router_kernel,
        grid=(t // tm,),
        in_specs=[pl.BlockSpec((tm, d), lambda i: (i, 0)), ytile, ytile, ytile, ytile,
                  pl.BlockSpec((d, d), lambda i: (0, 0)),
                  pl.BlockSpec((1, 6, d), lambda i: (i // spb, 0, 0)),
                  pl.BlockSpec((1, d), lambda i: (0, 0)),
                  pl.BlockSpec((d, LANES), lambda i: (0, 0)),
                  pl.BlockSpec((1, LANES), lambda i: (0, 0))],
        out_specs=[pl.BlockSpec((tm, d), lambda i: (i, 0)),
                   pl.BlockSpec((tm, d // 2), lambda i: (i, 0)),
                   pl.BlockSpec((TOP_K, tm), lambda i: (0, i)),
                   pl.BlockSpec((tm, LANES), lambda i: (i, 0))],
        out_shape=[jax.ShapeDtypeStruct((t, d), F32),
                   jax.ShapeDtypeStruct((t, d // 2), jnp.uint32),
                   jax.ShapeDtypeStruct((TOP_K, t), I32),
                   jax.ShapeDtypeStruct((t, LANES), F32)],
        compiler_params=_cparams("arbitrary"),
        name="outproj_router",
    )(x2, *ys, w_out, mod, w_norm2, w_route, b_route)


def _rank_kernel(eid_ref, rank_ref, count_ref, carry):
    @pl.when(pl.program_id(0) == 0)
    def _():
        carry[...] = jnp.zeros_like(carry)

    tr = eid_ref.shape[1]
    expert = lax.broadcasted_iota(I32, (N_EXPERTS, tr), 0)
    before = (lax.broadcasted_iota(I32, (tr, tr), 0) < lax.broadcasted_iota(I32, (tr, tr), 1)).astype(BF16)
    base = carry[...]
    for k in range(TOP_K):
        onehot = (expert == eid_ref[k:k + 1, :]).astype(F32)
        prefix = _dot(onehot.astype(BF16), before)
        rank_ref[k:k + 1, :] = jnp.sum(onehot * (base + prefix), axis=0, keepdims=True).astype(I32)
        base = base + jnp.sum(onehot, axis=1, keepdims=True)
    carry[...] = base
    count_ref[...] = jnp.broadcast_to(base, count_ref.shape)


def _expert_ranks(eids, tr=512):
    t = eids.shape[1]
    return pl.pallas_call(
        _rank_kernel,
        grid=(t // tr,),
        in_specs=[pl.BlockSpec((TOP_K, tr), lambda i: (0, i))],
        out_specs=[pl.BlockSpec((TOP_K, tr), lambda i: (0, i)),
                   pl.BlockSpec((N_EXPERTS, LANES), lambda i: (0, 0))],
        out_shape=[jax.ShapeDtypeStruct((TOP_K, t), I32),
                   jax.ShapeDtypeStruct((N_EXPERTS, LANES), F32)],
        scratch_shapes=[pltpu.VMEM((N_EXPERTS, 1), F32)],
        compiler_params=_cparams("arbitrary"),
        name="expert_ranks",
    )(eids)


def _dest_kernel(pstart_ref, eid_ref, rank_ref, dest_ref):
    e = eid_ref[...]
    dest = rank_ref[...]
    for j in range(N_EXPERTS):
        dest = dest + jnp.where(e == j, pstart_ref[j], 0)
    dest_ref[...] = dest


def _dest_rows(p_starts, eids, ranks, tm=2048):
    t = eids.shape[1]
    grid_spec = pltpu.PrefetchScalarGridSpec(
        num_scalar_prefetch=1,
        grid=(t // tm,),
        in_specs=[pl.BlockSpec((TOP_K, tm), lambda i, ps: (0, i)),
                  pl.BlockSpec((TOP_K, tm), lambda i, ps: (0, i))],
        out_specs=pl.BlockSpec((TOP_K, tm), lambda i, ps: (0, i)))
    return pl.pallas_call(
        _dest_kernel,
        grid_spec=grid_spec,
        out_shape=jax.ShapeDtypeStruct((TOP_K, t), I32),
        compiler_params=_cparams("arbitrary"),
        name="dest_rows",
    )(p_starts, eids, ranks)


ROW_TILE = IDX_CHUNK // TOP_K


def _tile_indices(idx_hbm, idx_smem, idx_sems, tile, slot):
    return pltpu.make_async_copy(idx_hbm.at[pl.ds(tile * IDX_CHUNK, IDX_CHUNK)],
                                 idx_smem.at[pl.ds(slot * IDX_CHUNK, IDX_CHUNK)], idx_sems.at[slot])


def _dispatch_kernel(pend_ref, padded_ref, nu_ref, idx_hbm, h_ref, xb_hbm, idx_smem, zero_blk, idx_sems, row_sem,
                     zero_sem):
    def zero_block(start):
        return pltpu.make_async_copy(zero_blk, xb_hbm.at[pl.ds(pl.multiple_of(start, MOE_BLOCK), MOE_BLOCK)],
                                     zero_sem)

    @pl.when(pl.program_id(0) == 0)
    def _():
        zero_blk[...] = jnp.zeros_like(zero_blk)
        for e in range(N_EXPERTS):
            @pl.when(padded_ref[e] > 0)
            def _(e=e):
                zero_block(pend_ref[e] - MOE_BLOCK).start()
        for e in range(N_EXPERTS):
            @pl.when(padded_ref[e] > 0)
            def _(e=e):
                zero_block(pend_ref[e] - MOE_BLOCK).wait()

        def zero_unused(b, carry):
            copy = zero_block(b * MOE_BLOCK)
            copy.start()
            copy.wait()
            return carry

        lax.fori_loop(nu_ref[0], xb_hbm.shape[0] // MOE_BLOCK, zero_unused, 0)

    i = pl.program_id(0)

    @pl.when(i == 0)
    def _():
        _tile_indices(idx_hbm, idx_smem, idx_sems, i, 0).start()

    def row_copy(r, dst_row):
        return pltpu.make_async_copy(h_ref.at[pl.ds(r, 1)], xb_hbm.at[pl.ds(dst_row, 1)], row_sem)

    def step_for_slot(slot):
        _tile_indices(idx_hbm, idx_smem, idx_sems, i, slot).wait()

        @pl.when(i + 1 < pl.num_programs(0))
        def _():
            _tile_indices(idx_hbm, idx_smem, idx_sems, i + 1, 1 - slot).start()

        def issue(r, carry):
            for k in range(TOP_K):
                row_copy(r, idx_smem[slot * IDX_CHUNK + k * ROW_TILE + r]).start(priority=k)
            return carry

        lax.fori_loop(0, ROW_TILE, issue, 0, unroll=8)

    for parity in range(2):
        pl.when(i % 2 == parity)(functools.partial(step_for_slot, parity))

    def drain(r, carry):
        for k in range(TOP_K):
            row_copy(0, 0).wait()
        return carry

    lax.fori_loop(0, ROW_TILE, drain, 0, unroll=8)


def _dispatch(p_ends, padded, n_used, idx_tiles, h2, dst_rows):
    t, d = h2.shape
    any_spec = pl.BlockSpec(memory_space=pl.ANY)
    grid_spec = pltpu.PrefetchScalarGridSpec(
        num_scalar_prefetch=3,
        grid=(t // ROW_TILE,),
        in_specs=[any_spec, pl.BlockSpec((ROW_TILE, d), lambda i, pe, pd, nu: (i, 0))],
        out_specs=any_spec,
        scratch_shapes=[pltpu.SMEM((2 * IDX_CHUNK,), I32), pltpu.VMEM((MOE_BLOCK, d), h2.dtype),
                        pltpu.SemaphoreType.DMA((2,)), pltpu.SemaphoreType.DMA, pltpu.SemaphoreType.DMA])
    return pl.pallas_call(
        _dispatch_kernel,
        grid_spec=grid_spec,
        out_shape=jax.ShapeDtypeStruct((dst_rows, d), h2.dtype),
        compiler_params=_cparams("arbitrary"),
        name="moe_dispatch",
    )(p_ends, padded, n_used, idx_tiles, h2)


def _expert_kernel(be_ref, nu_ref, x_ref, wg_ref, wu_ref, wd_ref, y_ref, wg_b, wu_b, wd_b):
    b = pl.program_id(0)

    @pl.when(b < nu_ref[0])
    def _():
        @pl.when(jnp.logical_or(b == 0, be_ref[b] != be_ref[jnp.maximum(b - 1, 0)]))
        def _():
            wg_b[...] = wg_ref[0].astype(BF16)
            wu_b[...] = wu_ref[0].astype(BF16)
            wd_b[...] = wd_ref[0].astype(BF16)

        x = jnp.concatenate(_unpack_bf16_pairs(x_ref[...]), axis=1).astype(BF16)
        a = _silu(_dot(x, wg_b[...])) * _dot(x, wu_b[...])
        y_ref[...] = _pack_bf16_pairs(_dot(a.astype(BF16), wd_b[...]))

    @pl.when(b >= nu_ref[0])
    def _():
        y_ref[...] = jnp.zeros_like(y_ref)


def _expert_mlp(block_e, n_used, xb, w_gate, w_up, w_down):
    p, words = xb.shape
    d, de = w_gate.shape[1:]
    blk = lambda b, be, nu: (jnp.minimum(b, nu[0] - 1), 0)
    grid_spec = pltpu.PrefetchScalarGridSpec(
        num_scalar_prefetch=2,
        grid=(p // MOE_BLOCK,),
        in_specs=[pl.BlockSpec((MOE_BLOCK, words), blk),
                  pl.BlockSpec((1, d, de), lambda b, be, nu: (be[b], 0, 0)),
                  pl.BlockSpec((1, d, de), lambda b, be, nu: (be[b], 0, 0)),
                  pl.BlockSpec((1, de, d), lambda b, be, nu: (be[b], 0, 0))],
        out_specs=pl.BlockSpec((MOE_BLOCK, words), lambda b, be, nu: (b, 0)),
        scratch_shapes=[pltpu.VMEM((d, de), BF16), pltpu.VMEM((d, de), BF16), pltpu.VMEM((de, d), BF16)])
    return pl.pallas_call(
        _expert_kernel,
        grid_spec=grid_spec,
        out_shape=jax.ShapeDtypeStruct((p, words), jnp.uint32),
        compiler_params=_cparams("arbitrary"),
        name="expert_mlp",
    )(block_e, n_used, xb, w_gate, w_up, w_down)


def _combine_kernel(idx_hbm, yb_hbm, x_ref, gcol_ref, mod_ref, wnf_ref, o_ref, ybuf, idx_smem, idx_sems, row_sems,
                    *, final):
    i = pl.program_id(0)
    n = pl.num_programs(0)

    indices = functools.partial(_tile_indices, idx_hbm, idx_smem, idx_sems)

    def row_copy(src_row, slot, k, r):
        return pltpu.make_async_copy(yb_hbm.at[pl.ds(src_row, 1)], ybuf.at[slot, k, pl.ds(r, 1)],
                                     row_sems.at[slot])

    def issue_tile(slot):
        def issue(r, carry):
            for k in range(TOP_K):
                row_copy(idx_smem[slot * IDX_CHUNK + k * ROW_TILE + r], slot, k, r).start(priority=k)
            return carry

        lax.fori_loop(0, ROW_TILE, issue, 0, unroll=8)

    @pl.when(i == 0)
    def _():
        first = indices(i, 0)
        first.start()
        first.wait()
        issue_tile(0)

        @pl.when(n > 1)
        def _():
            indices(i + 1, 1).start()

    def step_for_slot(slot):
        other = 1 - slot

        @pl.when(i + 1 < n)
        def _():
            indices(i + 1, other).wait()

        @pl.when(i + 2 < n)
        def _():
            indices(i + 2, slot).start()

        @pl.when(i + 1 < n)
        def _():
            issue_tile(other)

        def drain(r, carry):
            for k in range(TOP_K):
                row_copy(0, slot, k, 0).wait()
            return carry

        lax.fori_loop(0, ROW_TILE, drain, 0, unroll=8)

        gc = gcol_ref[...]
        y0 = _unpack_bf16_pairs(ybuf[slot, 0])
        y1 = _unpack_bf16_pairs(ybuf[slot, 1])
        moe = jnp.concatenate([gc[:, 0:1] * y0[0] + gc[:, 1:2] * y1[0], gc[:, 0:1] * y0[1] + gc[:, 1:2] * y1[1]],
                              axis=1)
        x2 = x_ref[...] + mod_ref[0][5:6, :] * moe
        o_ref[...] = _rms_norm(x2, wnf_ref[...]) if final else x2

    for parity in range(2):
        pl.when(i % 2 == parity)(functools.partial(step_for_slot, parity))


def _combine(idx_tiles, yb, x2, gcol, mod, w_norm_final, seq, final):
    t, d = x2.shape
    tm = ROW_TILE
    spb = seq // tm
    any_spec = pl.BlockSpec(memory_space=pl.ANY)
    return pl.pallas_call(
        functools.partial(_combine_kernel, final=final),
        grid=(t // tm,),
        in_specs=[any_spec, any_spec,
                  pl.BlockSpec((tm, d), lambda i: (i, 0)),
                  pl.BlockSpec((tm, LANES), lambda i: (i, 0)),
                  pl.BlockSpec((1, 6, d), lambda i: (i // spb, 0, 0)),
                  pl.BlockSpec((1, d), lambda i: (0, 0))],
        out_specs=pl.BlockSpec((tm, d), lambda i: (i, 0)),
        out_shape=jax.ShapeDtypeStruct((t, d), F32),
        scratch_shapes=[pltpu.VMEM((2, TOP_K, tm, d // 2), jnp.uint32), pltpu.SMEM((2 * IDX_CHUNK,), I32),
                        pltpu.SemaphoreType.DMA((2,)), pltpu.SemaphoreType.DMA((2,))],
        compiler_params=_cparams("arbitrary"),
        name="moe_combine",
    )(idx_tiles, yb, x2, gcol, mod, w_norm_final)


def _lane_row(pieces, width=LANES):
    row = jnp.zeros((width,), F32)
    for off, vec in pieces.items():
        row = row.at[off:off + vec.shape[0]].set(vec.astype(F32))
    return row.reshape(1, width)


def _split_w_in(w_in):
    gw, nh = GROUP_WIDTH, N_HEADS
    widths = [gw, gw, gw, nh, gw, gw, gw, 3 * gw, nh, gw, gw, gw, gw, nh, nh]
    cuts, acc = [], 0
    for w in widths[:-1]:
        acc += w
        cuts.append(acc)
    (aq, ak, av, af, su, sv, mz, mxbc, mdt, lq, lk, lv, lo, li, lf) = jnp.split(w_in, cuts, axis=1)
    scale = HEAD_DIM ** -0.5
    wmain = jnp.concatenate([ak, su, sv, mz, mxbc, lq, lk * scale, lv, lo], axis=1).astype(BF16)
    wqt = (aq * scale).T.astype(BF16)
    wvt = av.T.astype(BF16)
    wg = jnp.zeros((w_in.shape[0], LANES), F32)
    for off, w in ((GATE_AF, af), (GATE_DT, mdt), (GATE_LI, li), (GATE_LF, lf)):
        wg = wg.at[:, off:off + nh].set(w)
    return wmain, wqt, wvt, wg.astype(BF16)


def _moe_layer(x1, h2, eids, gcol, mod_l, w_gate, w_up, w_down, layer, w_norm_final, seq, final):
    t, d = x1.shape
    ranks, counts = _expert_ranks(eids)
    counts = counts[:, 0].astype(I32)
    padded = ((counts + MOE_BLOCK - 1) // MOE_BLOCK) * MOE_BLOCK
    p_ends = jnp.cumsum(padded)
    p_starts = (p_ends - padded).astype(I32)
    n_blocks = (t * TOP_K) // MOE_BLOCK + N_EXPERTS
    blocks = jnp.arange(n_blocks, dtype=I32)
    block_e = jnp.sum((p_ends[None, :] <= (blocks * MOE_BLOCK)[:, None]).astype(I32), axis=1)
    block_e = jnp.minimum(block_e, N_EXPERTS - 1)
    n_used = (p_ends[-1:] // MOE_BLOCK).astype(I32)
    block_e = jnp.where(blocks < n_used, block_e, block_e[n_used[0] - 1])
    dest = _dest_rows(p_starts, eids, ranks)
    idx_tiles = dest.reshape(TOP_K, t // ROW_TILE, ROW_TILE).transpose(1, 0, 2).reshape(-1)
    xb = _dispatch(p_ends.astype(I32), padded.astype(I32), n_used, idx_tiles, h2, n_blocks * MOE_BLOCK)
    yb = _expert_mlp(block_e + layer * N_EXPERTS, n_used, xb, w_gate, w_up, w_down)
    return _combine(idx_tiles, yb, x1, gcol, mod_l, w_norm_final, seq, final)


def kernel(x, c, w_in, w_out, w_mix_norm, attn_f_bias, sg_w, sg_b, ssm_conv_w, ssm_conv_b, ssm_dt_bias,
           ssm_a_log, ssm_d, mlstm_i_bias, mlstm_f_bias, w_ada, b_ada, w_norm1, w_norm2, w_router_group,
           b_router_group, w_router_expert, b_router_expert, w_expert_gate, w_expert_up, w_expert_down,
           w_norm_final):
    bsz, seq, d = x.shape
    depth = w_in.shape[0]
    gw = GROUP_WIDTH
    mod = _ada_modulation(c, w_ada, b_ada).reshape(depth, bsz, 6, d)
    x2 = x.reshape(bsz * seq, d)
    wnf = w_norm_final.reshape(1, d)
    w_eg = w_expert_gate.reshape((depth * N_EXPERTS,) + w_expert_gate.shape[2:])
    w_eu = w_expert_up.reshape((depth * N_EXPERTS,) + w_expert_up.shape[2:])
    w_ed = w_expert_down.reshape((depth * N_EXPERTS,) + w_expert_down.shape[2:])
    for l in range(depth):
        wmain, wqt, wvt, wg = _split_w_in(w_in[l])
        gains = w_mix_norm[l].reshape(N_HEADS, 1, gw)
        proj, qt, vt, gates, f2 = _inproj(x2, mod[l], w_norm1[l].reshape(1, d), wmain, wqt, wvt, wg,
                                          _lane_row({GATE_AF: attn_f_bias[l]}), bsz, seq)
        y_attn = _attention(proj, qt, vt, f2, gains[0], bsz, seq)
        sg_bias_full = jnp.repeat(sg_b[l].T, HEAD_DIM, axis=1)
        y_sg = _spatial_gating(proj, gains[1], sg_w[l], sg_bias_full)
        y_ssm = _ssd_mixer(proj, gates, ssm_conv_w[l], ssm_conv_b[l].reshape(1, -1),
                           _lane_row({GATE_DT: ssm_dt_bias[l]}), _lane_row({GATE_DT: ssm_a_log[l]}),
                           jnp.repeat(ssm_d[l], HEAD_DIM).reshape(1, gw), gains[2], bsz, seq)
        y_ml = _mlstm_mixer(proj, gates, _lane_row({GATE_LI: mlstm_i_bias[l], GATE_LF: mlstm_f_bias[l]}),
                            gains[3], bsz, seq)
        w_route = jnp.zeros((d, LANES), F32)
        w_route = w_route.at[:, ROUTE_G:ROUTE_G + N_EXPERT_GROUPS].set(w_router_group[l])
        w_route = w_route.at[:, ROUTE_E:ROUTE_E + N_EXPERTS].set(w_router_expert[l]).astype(BF16)
        b_route = _lane_row({ROUTE_G: b_router_group[l], ROUTE_E: b_router_expert[l]})
        x1, h2, eids, gcol = _outproj_router(x2, (y_attn, y_sg, y_ssm, y_ml), w_out[l].astype(BF16), mod[l],
                                             w_norm2[l].reshape(1, d), w_route, b_route, seq)
        x2 = _moe_layer(x1, h2, eids, gcol, mod[l], w_eg, w_eu, w_ed, l, wnf, seq, final=(l == depth - 1))
    return x2.reshape(bsz, seq, d)
```

```python
import functools

import jax
import jax.numpy as jnp
from jax import lax
from jax.experimental import pallas as pl
from jax.experimental.pallas import tpu as pltpu

F32 = jnp.float32
BF16 = jnp.bfloat16
I32 = jnp.int32

LANES = 128
SUBLANES = 8
HEAD_DIM = 64
N_HEADS = 4
GROUP_WIDTH = 256
CHUNK = 128
MIX_TILE = 4 * CHUNK
SSM_GROUPS = 2
CONV_WIDTH = 4
N_EXPERT_GROUPS = 4
EXPERTS_PER_GROUP = 8
N_EXPERTS = N_EXPERT_GROUPS * EXPERTS_PER_GROUP
TOP_K = 2
MOE_BLOCK = 512
NORM_EPS = 1e-6
MLSTM_M_INIT = -1e30
NEG_INF = float("-inf")
VMEM_LIMIT_BYTES = 48 * 1024 * 1024
IDX_CHUNK = 1024

(PB_K, PB_SU, PB_SV, PB_MZ, PB_X, PB_B, PB_C, PB_LQ, PB_LK, PB_LV, PB_LO) = range(11)
PROJ_COLS = 11 * GROUP_WIDTH
LOG2_E = 1.4426950408889634
GATE_AF, GATE_DT, GATE_LI, GATE_LF = 0, 4, 8, 12
ROUTE_G, ROUTE_E = 0, 8

NT_DIMS = (((1,), (1,)), ((), ()))


def _cparams(*sem):
    return pltpu.CompilerParams(dimension_semantics=sem, vmem_limit_bytes=VMEM_LIMIT_BYTES)


def _dot(a, b):
    return jnp.dot(a, b, preferred_element_type=F32)


def _dot_nt(a, b):
    return lax.dot_general(a, b, NT_DIMS, preferred_element_type=F32)


def _dot_exact(a, b):
    return jnp.dot(a, b, preferred_element_type=F32, precision=lax.Precision.HIGHEST)


def _head_of_lane(shape, axis=1):
    return lax.broadcasted_iota(I32, shape, axis) // HEAD_DIM


def _sigmoid(x):
    return 1.0 / (1.0 + jnp.exp(-x))


def _silu(x):
    return x * _sigmoid(x)


def _log_sigmoid(x):
    return jnp.minimum(x, 0.0) - jnp.log1p(jnp.exp(-jnp.abs(x)))


def _softplus(x):
    return jnp.maximum(x, 0.0) + jnp.log1p(jnp.exp(-jnp.abs(x)))


def _gelu_tanh(x):
    return 0.5 * x * (1.0 + jnp.tanh(0.7978845608028654 * (x + 0.044715 * (x * x * x))))


def _expand_heads(cols, width=GROUP_WIDTH):
    rows = cols[0].shape[0]
    head = _head_of_lane((rows, width))
    out = jnp.broadcast_to(cols[N_HEADS - 1], (rows, width))
    for h in range(N_HEADS - 2, -1, -1):
        out = jnp.where(head == h, jnp.broadcast_to(cols[h], (rows, width)), out)
    return out


def _mask_head(x, h):
    return jnp.where(_head_of_lane(x.shape) == h, x, jnp.zeros_like(x))


def _head_rms_norm(y, gain):
    head = _head_of_lane(y.shape)
    sq = y * y
    cols = [jnp.sum(jnp.where(head == h, sq, 0.0), axis=1, keepdims=True) * (1.0 / HEAD_DIM)
            for h in range(N_HEADS)]
    return y * lax.rsqrt(_expand_heads(cols) + NORM_EPS) * gain


def _rms_norm(x, gain):
    ms = jnp.mean(x * x, axis=1, keepdims=True)
    return x * lax.rsqrt(ms + NORM_EPS) * gain


def _pack_bf16_pairs(x):
    half = x.shape[1] // 2
    bits = lax.bitcast_convert_type(x.astype(BF16).astype(F32), jnp.uint32)
    return bits[:, :half] | lax.shift_right_logical(bits[:, half:], jnp.uint32(16))


def _unpack_bf16_pairs(words):
    first = lax.bitcast_convert_type(words & jnp.uint32(0xFFFF0000), F32)
    second = lax.bitcast_convert_type(lax.shift_left(words, jnp.uint32(16)), F32)
    return first, second


def _cumsum_rows(x):
    tri = jnp.where(_tril(x.shape[0]), 1.0, 0.0).astype(BF16)
    hi = x.astype(BF16)
    rest = x - hi.astype(F32)
    mid = rest.astype(BF16)
    lo = (rest - mid.astype(F32)).astype(BF16)
    return _dot(tri, hi) + _dot(tri, mid) + _dot(tri, lo)


def _tril(n, strict=False):
    r = lax.broadcasted_iota(I32, (n, n), 0)
    c = lax.broadcasted_iota(I32, (n, n), 1)
    return (r > c) if strict else (r >= c)


def _ada_kernel(c_ref, w_ref, b_ref, o_ref):
    o_ref[0] = _dot_exact(_silu(c_ref[...]), w_ref[0]) + b_ref[0]


def _ada_modulation(c, w_ada, b_ada):
    depth, d, d6 = w_ada.shape
    bsz = c.shape[0]
    return pl.pallas_call(
        _ada_kernel,
        grid=(depth, d6 // d),
        in_specs=[pl.BlockSpec((bsz, d), lambda l, j: (0, 0)),
                  pl.BlockSpec((1, d, d), lambda l, j: (l, 0, j)),
                  pl.BlockSpec((1, 1, d), lambda l, j: (l, 0, j))],
        out_specs=pl.BlockSpec((1, bsz, d), lambda l, j: (l, 0, j)),
        out_shape=jax.ShapeDtypeStruct((depth, bsz, d6), F32),
        compiler_params=_cparams("arbitrary", "arbitrary"),
        name="ada_modulation",
    )(c, w_ada, b_ada.reshape(depth, 1, d6))


def _forget_bias_slab(gates, carry):
    tb = gates.shape[0]
    cum = _cumsum_rows(_log_sigmoid(gates)) + carry[...]
    carry[...] = cum[tb - 1:tb, :]
    f2 = cum * LOG2_E
    hi = f2.astype(BF16)
    rest = f2 - hi.astype(F32)
    mid = rest.astype(BF16)
    lo = (rest - mid.astype(F32)).astype(BF16)
    r = lax.broadcasted_iota(I32, (LANES, GROUP_WIDTH), 0) - GATE_AF
    c = lax.broadcasted_iota(I32, (LANES, GROUP_WIDTH), 1)
    slab = jnp.zeros((tb, GROUP_WIDTH), F32)
    for j, piece in enumerate((hi, mid, lo)):
        place = (r >= 0) & (r < N_HEADS) & (c == ((r + 1) % N_HEADS) * HEAD_DIM + j)
        slab = slab + _dot(piece, jnp.where(place, -1.0, 0.0).astype(BF16))
    return slab.astype(BF16)


def _inproj_kernel(x_ref, mod_ref, wn_ref, wmain_ref, wqt_ref, wvt_ref, wg_ref, fbias_ref,
                   proj_ref, qt_ref, vt_ref, gates_ref, slab_ref, h_scr, f_carry, *, tiles_per_seq):
    @pl.when(pl.program_id(0) % tiles_per_seq == 0)
    def _():
        f_carry[...] = jnp.zeros_like(f_carry)

    mod = mod_ref[0]
    h = _rms_norm(x_ref[...], wn_ref[...]) * (1.0 + mod[1:2, :]) + mod[0:1, :]
    h_scr[...] = h.astype(BF16)
    for c0 in range(0, PROJ_COLS, GROUP_WIDTH):
        proj_ref[:, c0:c0 + GROUP_WIDTH] = _dot(h_scr[...], wmain_ref[:, c0:c0 + GROUP_WIDTH]).astype(BF16)
    qt_ref[0] = (_dot_nt(wqt_ref[...], h_scr[...]) * LOG2_E).astype(BF16)
    vt_ref[0] = _dot_nt(wvt_ref[...], h_scr[...]).astype(BF16)
    gates = _dot(h_scr[...], wg_ref[...])
    gates_ref[...] = gates
    slab_ref[...] = _forget_bias_slab(gates + fbias_ref[...], f_carry)


def _inproj(x2, mod, w_norm, wmain, wqt, wvt, wg, f_bias_row, bsz, seq, tm=512):
    t, d = x2.shape
    spb = seq // tm
    tspec = pl.BlockSpec((1, GROUP_WIDTH, tm), lambda i: (i // spb, 0, i % spb))
    return pl.pallas_call(
        functools.partial(_inproj_kernel, tiles_per_seq=spb),
        grid=(t // tm,),
        in_specs=[pl.BlockSpec((tm, d), lambda i: (i, 0)),
                  pl.BlockSpec((1, 6, d), lambda i: (i // spb, 0, 0)),
                  pl.BlockSpec((1, d), lambda i: (0, 0)),
                  pl.BlockSpec((d, PROJ_COLS), lambda i: (0, 0)),
                  pl.BlockSpec((GROUP_WIDTH, d), lambda i: (0, 0)),
                  pl.BlockSpec((GROUP_WIDTH, d), lambda i: (0, 0)),
                  pl.BlockSpec((d, LANES), lambda i: (0, 0)),
                  pl.BlockSpec((1, LANES), lambda i: (0, 0))],
        out_specs=[pl.BlockSpec((tm, PROJ_COLS), lambda i: (i, 0)), tspec, tspec,
                   pl.BlockSpec((tm, LANES), lambda i: (i, 0)),
                   pl.BlockSpec((tm, GROUP_WIDTH), lambda i: (i, 0))],
        out_shape=[jax.ShapeDtypeStruct((t, PROJ_COLS), BF16),
                   jax.ShapeDtypeStruct((bsz, GROUP_WIDTH, seq), BF16),
                   jax.ShapeDtypeStruct((bsz, GROUP_WIDTH, seq), BF16),
                   jax.ShapeDtypeStruct((t, LANES), F32),
                   jax.ShapeDtypeStruct((t, GROUP_WIDTH), BF16)],
        scratch_shapes=[pltpu.VMEM((tm, d), BF16), pltpu.VMEM((1, LANES), F32)],
        compiler_params=_cparams("arbitrary"),
        name="norm_inproj",
    )(x2, mod, w_norm, wmain, wqt, wvt, wg, f_bias_row)


ONES_ROWS = 16
QUERY_SPLITS = 2


def _attn_kernel(qi_ref, kj_ref, k_ref, qt_ref, vt_ref, f_ref, gain_ref, o_ref, qtm, m_s, l_s, acc):
    p = pl.program_id(1)
    qi, kj = qi_ref[p], kj_ref[p]
    tk = k_ref.shape[0]
    tq = qt_ref.shape[2]

    @pl.when(kj == 0)
    def _():
        qt = qt_ref[0]
        row = lax.broadcasted_iota(I32, qt.shape, 0)
        for h in range(N_HEADS):
            slot = ((h + 1) % N_HEADS) * HEAD_DIM
            ones_rows = jnp.where((row >= slot) & (row < slot + 3), 1.0, 0.0).astype(BF16)
            qtm[h] = jnp.where(row // HEAD_DIM == h, qt, ones_rows)
        m_s[...] = jnp.full_like(m_s, NEG_INF)
        l_s[...] = jnp.zeros_like(l_s)
        acc[...] = jnp.zeros_like(acc)

    def step(diagonal):
        k = k_ref[...]
        bias = f_ref[...]
        head = _head_of_lane(k.shape)
        ones = jnp.ones((ONES_ROWS, tk), BF16)
        if diagonal:
            visible = lax.broadcasted_iota(I32, (tk, tq), 0) <= lax.broadcasted_iota(I32, (tk, tq), 1)
        scores, m_news, alphas, probs = {}, {}, {}, {}
        k_aug = [jnp.where(head == h, k, bias) for h in range(N_HEADS)]
        splits = 1 if diagonal else QUERY_SPLITS
        half = tq // splits
        units = [(h, c) for h in range(N_HEADS) for c in range(splits)]

        def score(u):
            h, c = units[u]
            cols = slice(c * half, (c + 1) * half)
            s = _dot(k_aug[h], qtm[h, :, cols])
            if diagonal:
                s = jnp.where(visible[:, cols], s, NEG_INF)
            scores[u] = s
            m_prev = m_s[h, :, cols]
            m_news[u] = jnp.maximum(m_prev, jnp.max(s, axis=0, keepdims=True))
            alphas[u] = jnp.exp2(m_prev - m_news[u])
            m_s[h, :, cols] = m_news[u]

        def prob(u):
            probs[u] = jnp.exp2(scores[u] - m_news[u]).astype(BF16)

        def weighted_sum(u):
            h, c = units[u]
            cols = slice(c * half, (c + 1) * half)
            vt_ext = jnp.concatenate([vt_ref[0, h * HEAD_DIM:(h + 1) * HEAD_DIM, :], ones], axis=0)
            pv = _dot(vt_ext, probs[u])
            acc[h, :, cols] = alphas[u] * acc[h, :, cols] + pv[0:HEAD_DIM, :]
            l_s[h, :, cols] = alphas[u] * l_s[h, :, cols] + pv[HEAD_DIM:HEAD_DIM + 1, :]

        for stage in range(len(units) + 2):
            if stage < len(units):
                score(stage)
            if 0 <= stage - 1 < len(units):
                prob(stage - 1)
            if 0 <= stage - 2 < len(units):
                weighted_sum(stage - 2)

    @pl.when(kj < qi)
    def _():
        step(False)

    @pl.when(kj == qi)
    def _():
        step(True)
        yt = jnp.concatenate([acc[h] / l_s[h] for h in range(N_HEADS)], axis=0)
        o_ref[...] = _head_rms_norm(yt.T, gain_ref[...]).astype(BF16)


def _attention(proj, qt, vt, f2, gain, bsz, seq, tq=1024):
    nq = seq // tq
    pairs = [(i, j) for i in range(nq) for j in range(i + 1)]
    qi = jnp.asarray([a for a, _ in pairs], I32)
    kj = jnp.asarray([b for _, b in pairs], I32)
    t = bsz * seq
    grid_spec = pltpu.PrefetchScalarGridSpec(
        num_scalar_prefetch=2,
        grid=(bsz, len(pairs)),
        in_specs=[pl.BlockSpec((tq, GROUP_WIDTH), lambda b, p, qi, kj: (b * nq + kj[p], PB_K)),
                  pl.BlockSpec((1, GROUP_WIDTH, tq), lambda b, p, qi, kj: (b, 0, qi[p])),
                  pl.BlockSpec((1, GROUP_WIDTH, tq), lambda b, p, qi, kj: (b, 0, kj[p])),
                  pl.BlockSpec((tq, GROUP_WIDTH), lambda b, p, qi, kj: (b * nq + kj[p], 0)),
                  pl.BlockSpec((1, GROUP_WIDTH), lambda b, p, qi, kj: (0, 0))],
        out_specs=pl.BlockSpec((tq, GROUP_WIDTH), lambda b, p, qi, kj: (b * nq + qi[p], 0)),
        scratch_shapes=[pltpu.VMEM((N_HEADS, GROUP_WIDTH, tq), BF16),
                        pltpu.VMEM((N_HEADS, 1, tq), F32),
                        pltpu.VMEM((N_HEADS, 1, tq), F32),
                        pltpu.VMEM((N_HEADS, HEAD_DIM, tq), F32)])
    return pl.pallas_call(
        _attn_kernel,
        grid_spec=grid_spec,
        out_shape=jax.ShapeDtypeStruct((t, GROUP_WIDTH), BF16),
        compiler_params=_cparams("arbitrary", "arbitrary"),
        name="fox_attention",
    )(qi, kj, proj, qt, vt, f2, gain)


def _sg_kernel(u_ref, v_ref, gain_ref, w_ref, b_ref, o_ref):
    tm = u_ref.shape[0]
    u = _gelu_tanh(u_ref[...].astype(F32))
    v = _head_rms_norm(_gelu_tanh(v_ref[...].astype(F32)), gain_ref[...])
    causal = _tril(CHUNK)
    ws = [jnp.where(causal, w_ref[h], 0.0).astype(BF16) for h in range(N_HEADS)]
    for c0 in range(0, tm, CHUNK):
        vc = v[c0:c0 + CHUNK, :].astype(BF16)
        mixed = b_ref[...]
        for h in range(N_HEADS):
            mixed = mixed + _dot(ws[h], _mask_head(vc, h))
        o_ref[c0:c0 + CHUNK, :] = (u[c0:c0 + CHUNK, :] * mixed).astype(BF16)


def _spatial_gating(proj, gain, sg_w, sg_bias_full, tm=512):
    t = proj.shape[0]
    return pl.pallas_call(
        _sg_kernel,
        grid=(t // tm,),
        in_specs=[pl.BlockSpec((tm, GROUP_WIDTH), lambda i: (i, PB_SU)),
                  pl.BlockSpec((tm, GROUP_WIDTH), lambda i: (i, PB_SV)),
                  pl.BlockSpec((1, GROUP_WIDTH), lambda i: (0, 0)),
                  pl.BlockSpec((N_HEADS, CHUNK, CHUNK), lambda i: (0, 0, 0)),
                  pl.BlockSpec((CHUNK, GROUP_WIDTH), lambda i: (0, 0))],
        out_specs=pl.BlockSpec((tm, GROUP_WIDTH), lambda i: (i, 0)),
        out_shape=jax.ShapeDtypeStruct((t, GROUP_WIDTH), BF16),
        compiler_params=_cparams("arbitrary"),
        name="spatial_gating",
    )(proj, proj, gain, sg_w, sg_bias_full)


def _ssd_kernel(z_ref, x_ref, b_ref, c_ref, g_ref, cw_ref, cb_ref, dtb_ref, alog_ref, dskip_ref, gain_ref,
                o_ref, conv_scr, xbc_scr, state):
    L = CHUNK
    W = GROUP_WIDTH
    tile = x_ref.shape[0]

    @pl.when(pl.program_id(1) == 0)
    def _():
        conv_scr[0:SUBLANES, :] = jnp.zeros((SUBLANES, 3 * W), F32)
        state[...] = jnp.zeros_like(state)

    conv_scr[SUBLANES:, 0:W] = x_ref[...].astype(F32)
    conv_scr[SUBLANES:, W:2 * W] = b_ref[...].astype(F32)
    conv_scr[SUBLANES:, 2 * W:] = c_ref[...].astype(F32)
    cw = cw_ref[...]
    conv = cb_ref[...] + cw[CONV_WIDTH - 1:CONV_WIDTH, :] * conv_scr[SUBLANES:, :]
    for s in range(1, CONV_WIDTH):
        conv = conv + cw[CONV_WIDTH - 1 - s:CONV_WIDTH - s, :] * conv_scr[SUBLANES - s:SUBLANES - s + tile, :]
    conv_scr[0:SUBLANES, :] = conv_scr[tile:tile + SUBLANES, :]
    xbc_scr[...] = _silu(conv)

    neg_a = -jnp.exp(alog_ref[...])
    for c0 in range(0, tile, L):
        rows = slice(c0, c0 + L)
        dt = _softplus(g_ref[rows, :] + dtb_ref[...])
        y = _ssd_chunk(xbc_scr[rows, 0:W], xbc_scr[rows, W:2 * W], xbc_scr[rows, 2 * W:], dt, dt * neg_a,
                       dskip_ref[...], state)
        y = y * _silu(z_ref[rows, :].astype(F32))
        o_ref[rows, :] = _head_rms_norm(y, gain_ref[...]).astype(BF16)


def _ssd_chunk(xs, bm, cm, dt, da, dskip, state):
    L = CHUNK
    a_cum = _cumsum_rows(da)
    a_row = a_cum.T
    dt_cols = [dt[:, GATE_DT + h:GATE_DT + h + 1] for h in range(N_HEADS)]
    a_cols = [a_cum[:, GATE_DT + h:GATE_DT + h + 1] for h in range(N_HEADS)]
    a_end = [a_cum[L - 1:L, GATE_DT + h:GATE_DT + h + 1] for h in range(N_HEADS)]
    xdt = xs * _expand_heads(dt_cols)
    xdt_b = xdt.astype(BF16)
    xw = (xdt * _expand_heads([jnp.exp(a_end[h] - a_cols[h]) for h in range(N_HEADS)])).astype(BF16)
    exp_a = _expand_heads([jnp.exp(a_cols[h]) for h in range(N_HEADS)])
    causal = _tril(L)
    half = lax.broadcasted_iota(I32, (1, LANES), 1) < HEAD_DIM

    y = dskip * xs
    y_off = []
    for g in range(SSM_GROUPS):
        bg = bm[:, g * LANES:(g + 1) * LANES]
        cg = cm[:, g * LANES:(g + 1) * LANES].astype(BF16)
        scores = _dot_nt(cg, bg.astype(BF16))
        for r in range(N_HEADS // SSM_GROUPS):
            h = g * (N_HEADS // SSM_GROUPS) + r
            seg = a_cols[h] - a_row[GATE_DT + h:GATE_DT + h + 1, :]
            decay = jnp.exp(jnp.where(causal, seg, NEG_INF))
            y = y + _dot((scores * decay).astype(BF16), _mask_head(xdt_b, h))
        st_in = state[g]
        y_off.append(_dot(cg, st_in.astype(BF16)))
        chunk_decay = jnp.where(half, jnp.exp(a_end[2 * g]), jnp.exp(a_end[2 * g + 1]))
        state[g] = chunk_decay * st_in + _dot(bg.T.astype(BF16), xw[:, g * LANES:(g + 1) * LANES])
    return y + jnp.concatenate(y_off, axis=1) * exp_a


def _ssd_mixer(proj, gates, conv_w, conv_b, dtb_row, alog_row, dskip_row, gain, bsz, seq, tile=MIX_TILE):
    t = proj.shape[0]
    nc = seq // tile
    row = lambda blk: pl.BlockSpec((tile, GROUP_WIDTH), lambda b, j, blk=blk: (b * nc + j, blk))
    const = lambda shape: pl.BlockSpec(shape, lambda b, j: (0,) * len(shape))
    return pl.pallas_call(
        _ssd_kernel,
        grid=(bsz, nc),
        in_specs=[row(PB_MZ), row(PB_X), row(PB_B), row(PB_C),
                  pl.BlockSpec((tile, LANES), lambda b, j: (b * nc + j, 0)),
                  const((CONV_WIDTH, 3 * GROUP_WIDTH)), const((1, 3 * GROUP_WIDTH)),
                  const((1, LANES)), const((1, LANES)), const((1, GROUP_WIDTH)), const((1, GROUP_WIDTH))],
        out_specs=pl.BlockSpec((tile, GROUP_WIDTH), lambda b, j: (b * nc + j, 0)),
        out_shape=jax.ShapeDtypeStruct((t, GROUP_WIDTH), BF16),
        scratch_shapes=[pltpu.VMEM((tile + SUBLANES, 3 * GROUP_WIDTH), F32),
                        pltpu.VMEM((tile, 3 * GROUP_WIDTH), F32),
                        pltpu.VMEM((SSM_GROUPS, LANES, LANES), F32)],
        compiler_params=_cparams("arbitrary", "arbitrary"),
        name="ssd_mixer",
    )(proj, proj, proj, proj, gates, conv_w, conv_b, dtb_row, alog_row, dskip_row, gain)


def _mlstm_kernel(q_ref, k_ref, v_ref, o_gate_ref, g_ref, bias_ref, gain_ref, o_ref, ct, nb, m_row):
    L = CHUNK
    W = GROUP_WIDTH

    @pl.when(pl.program_id(1) == 0)
    def _():
        ct[...] = jnp.zeros_like(ct)
        nb[...] = jnp.zeros_like(nb)
        m_row[...] = jnp.full_like(m_row, MLSTM_M_INIT)

    def chunk(c, carry):
        rows = pl.ds(pl.multiple_of(c * L, L), L)
        y = _mlstm_chunk(q_ref[rows, :], k_ref[rows, :], v_ref[rows, :], g_ref[rows, :] + bias_ref[...],
                         gain_ref[...], ct, nb, m_row)
        o_ref[rows, :] = (_sigmoid(o_gate_ref[rows, :].astype(F32)) * y).astype(BF16)
        return carry

    lax.fori_loop(0, q_ref.shape[0] // L, chunk, 0)


def _mlstm_chunk(q, k, v, gate, gain, ct, nb, m_row):
    L = CHUNK
    W = GROUP_WIDTH
    a_full = _cumsum_rows(_log_sigmoid(gate))
    a_rows = a_full.T
    g_rows = gate.T
    causal = _tril(L)
    lane = lax.broadcasted_iota(I32, (1, LANES), 1)

    inter_q = _dot(q, ct[...].astype(BF16))
    n_q = _dot(q, nb[...].astype(BF16))
    m_old = m_row[...]
    num = jnp.zeros((L, W), F32)
    inter_cols, den_cols, ws_cols, scale_cols = [], [], [], []
    m_next = m_old
    for h in range(N_HEADS):
        a_col = a_full[:, GATE_LF + h:GATE_LF + h + 1]
        i_col = gate[:, GATE_LI + h:GATE_LI + h + 1]
        a_r = a_rows[GATE_LF + h:GATE_LF + h + 1, :]
        i_r = g_rows[GATE_LI + h:GATE_LI + h + 1, :]
        a_end = a_full[L - 1:L, GATE_LF + h:GATE_LF + h + 1]
        m_in = m_old[:, h:h + 1]
        src = jnp.where(causal, jnp.broadcast_to(i_r - a_r, (L, L)), NEG_INF)
        log_inter = a_col + m_in
        m_t = jnp.maximum(a_col + jnp.max(src, axis=1, keepdims=True), log_inter)
        w = _dot_nt(_mask_head(q, h), k) * jnp.exp(src + (a_col - m_t))
        inter = jnp.exp(log_inter - m_t)
        num = num + _dot(w.astype(BF16), _mask_head(v, h))
        den = jnp.sum(w, axis=1, keepdims=True) + inter * n_q[:, h:h + 1]
        inter_cols.append(inter)
        den_cols.append(jnp.maximum(jnp.abs(den), jnp.exp(-m_t)))
        g_col = a_end - a_col + i_col
        m_new = jnp.maximum(a_end + m_in, jnp.max(g_col, axis=0, keepdims=True))
        ws_cols.append(jnp.exp(g_col - m_new))
        scale_cols.append(jnp.exp(a_end + m_in - m_new))
        m_next = jnp.where(lane == h, m_new, m_next)
    n = num + _expand_heads(inter_cols) * inter_q
    head = _head_of_lane(n.shape)
    sq = n * n
    norm_cols = [lax.rsqrt(jnp.sum(jnp.where(head == h, sq, 0.0), axis=1, keepdims=True) * (1.0 / HEAD_DIM)
                           + NORM_EPS * den_cols[h] * den_cols[h]) for h in range(N_HEADS)]
    hout = n * _expand_heads(norm_cols) * gain

    kw_t = (k.astype(F32) * _expand_heads(ws_cols)).T.astype(BF16)
    scale_row = _expand_heads(scale_cols)
    same_head = _head_of_lane((W, W), 0) == _head_of_lane((W, W), 1)
    ct[...] = scale_row * ct[...] + jnp.where(same_head, _dot(kw_t, v), 0.0)
    col_is_head = _head_of_lane((W, LANES), 0) == lax.broadcasted_iota(I32, (W, LANES), 1)
    scale_n = scale_cols[N_HEADS - 1]
    for h in range(N_HEADS - 2, -1, -1):
        scale_n = jnp.where(lane == h, scale_cols[h], scale_n)
    nb[...] = scale_n * nb[...] + jnp.where(col_is_head, _dot(kw_t, jnp.ones((L, LANES), BF16)), 0.0)
    m_row[...] = m_next
    return hout


def _mlstm_mixer(proj, gates, bias_row, gain, bsz, seq, tile=MIX_TILE):
    t = proj.shape[0]
    nc = seq // tile
    row = lambda blk: pl.BlockSpec((tile, GROUP_WIDTH), lambda b, j, blk=blk: (b * nc + j, blk))
    const = lambda shape: pl.BlockSpec(shape, lambda b, j: (0,) * len(shape))
    return pl.pallas_call(
        _mlstm_kernel,
        grid=(bsz, nc),
        in_specs=[row(PB_LQ), row(PB_LK), row(PB_LV), row(PB_LO),
                  pl.BlockSpec((tile, LANES), lambda b, j: (b * nc + j, 0)),
                  const((1, LANES)), const((1, GROUP_WIDTH))],
        out_specs=pl.BlockSpec((tile, GROUP_WIDTH), lambda b, j: (b * nc + j, 0)),
        out_shape=jax.ShapeDtypeStruct((t, GROUP_WIDTH), BF16),
        scratch_shapes=[pltpu.VMEM((GROUP_WIDTH, GROUP_WIDTH), F32),
                        pltpu.VMEM((GROUP_WIDTH, LANES), F32),
                        pltpu.VMEM((1, LANES), F32)],
        compiler_params=_cparams("arbitrary", "arbitrary"),
        name="mlstm_mixer",
    )(proj, proj, proj, proj, gates, bias_row, gain)


def _outproj_router_kernel(x_ref, ya_ref, ys_ref, ym_ref, yl_ref, wo_ref, mod_ref, wn_ref, wr_ref, br_ref,
                           xo_ref, h2_ref, eid_ref, gcol_ref):
    W = GROUP_WIDTH
    tm = x_ref.shape[0]
    mod = mod_ref[0]
    out = _dot(ya_ref[...], wo_ref[0:W, :])
    out = out + _dot(ys_ref[...], wo_ref[W:2 * W, :])
    out = out + _dot(ym_ref[...], wo_ref[2 * W:3 * W, :])
    out = out + _dot(yl_ref[...], wo_ref[3 * W:4 * W, :])
    x1 = x_ref[...] + mod[2:3, :] * out
    xo_ref[...] = x1
    h2 = _rms_norm(x1, wn_ref[...]) * (1.0 + mod[4:5, :]) + mod[3:4, :]
    h2_ref[...] = _pack_bf16_pairs(h2)

    logits_t = (_dot(h2.astype(BF16), wr_ref[...]) + br_ref[...]).T
    row8 = lax.broadcasted_iota(I32, (SUBLANES, tm), 0).astype(F32)
    gl = jnp.where(row8 < N_EXPERT_GROUPS, logits_t[ROUTE_G:ROUTE_G + SUBLANES, :], NEG_INF)
    g_max = jnp.max(gl, axis=0, keepdims=True)
    g_sel = jnp.min(jnp.where(gl == g_max, row8, SUBLANES), axis=0, keepdims=True)
    g_prob = 1.0 / jnp.sum(jnp.exp(gl - g_max), axis=0, keepdims=True)
    el = logits_t[ROUTE_E:ROUTE_E + EXPERTS_PER_GROUP, :]
    for g in range(1, N_EXPERT_GROUPS):
        lo = ROUTE_E + g * EXPERTS_PER_GROUP
        el = jnp.where(g_sel == g, logits_t[lo:lo + EXPERTS_PER_GROUP, :], el)
    m1 = jnp.max(el, axis=0, keepdims=True)
    i1 = jnp.min(jnp.where(el == m1, row8, SUBLANES), axis=0, keepdims=True)
    el2 = jnp.where(row8 == i1, NEG_INF, el)
    m2 = jnp.max(el2, axis=0, keepdims=True)
    i2 = jnp.min(jnp.where(el2 == m2, row8, SUBLANES), axis=0, keepdims=True)
    ratio = jnp.exp(m2 - m1)
    p1 = 1.0 / (1.0 + ratio)
    eid_ref[0:1, :] = (g_sel * EXPERTS_PER_GROUP + i1).astype(I32)
    eid_ref[1:2, :] = (g_sel * EXPERTS_PER_GROUP + i2).astype(I32)
    rows = lax.broadcasted_iota(I32, (LANES, tm), 0)
    gate_rows = jnp.where(rows == 0, g_prob * p1, jnp.where(rows == 1, g_prob * p1 * ratio, 0.0))
    gcol_ref[...] = gate_rows.T


def _outproj_router(x2, ys, w_out, mod, w_norm2, w_route, b_route, seq, tm=512):
    t, d = x2.shape
    spb = seq // tm
    ytile = pl.BlockSpec((tm, GROUP_WIDTH), lambda i: (i, 0))
    return pl.pallas_call(
        _outproj_router_kernel,
        grid=(t // tm,),
        in_specs=[pl.BlockSpec((tm, d), lambda i: (i, 0)), ytile, ytile, ytile, ytile,
                  pl.BlockSpec((d, d), lambda i: (0, 0)),
                  pl.BlockSpec((1, 6, d), lambda i: (i // spb, 0, 0)),
                  pl.BlockSpec((1, d), lambda i: (0, 0)),
                  pl.BlockSpec((d, LANES), lambda i: (0, 0)),
                  pl.BlockSpec((1, LANES), lambda i: (0, 0))],
        out_specs=[pl.BlockSpec((tm, d), lambda i: (i, 0)),
                   pl.BlockSpec((tm, d // 2), lambda i: (i, 0)),
                   pl.BlockSpec((TOP_K, tm), lambda i: (0, i)),
                   pl.BlockSpec((tm, LANES), lambda i: (i, 0))],
        out_shape=[jax.ShapeDtypeStruct((t, d), F32),
                   jax.ShapeDtypeStruct((t, d // 2), jnp.uint32),
                   jax.ShapeDtypeStruct((TOP_K, t), I32),
                   jax.ShapeDtypeStruct((t, LANES), F32)],
        compiler_params=_cparams("arbitrary"),
        name="outproj_router",
    )(x2, *ys, w_out, mod, w_norm2, w_route, b_route)


def _rank_kernel(eid_ref, rank_ref, count_ref, carry):
    @pl.when(pl.program_id(0) == 0)
    def _():
        carry[...] = jnp.zeros_like(carry)

    tr = eid_ref.shape[1]
    expert = lax.broadcasted_iota(I32, (N_EXPERTS, tr), 0)
    before = (lax.broadcasted_iota(I32, (tr, tr), 0) < lax.broadcasted_iota(I32, (tr, tr), 1)).astype(BF16)
    base = carry[...]
    for k in range(TOP_K):
        onehot = (expert == eid_ref[k:k + 1, :]).astype(F32)
        prefix = _dot(onehot.astype(BF16), before)
        rank_ref[k:k + 1, :] = jnp.sum(onehot * (base + prefix), axis=0, keepdims=True).astype(I32)
        base = base + jnp.sum(onehot, axis=1, keepdims=True)
    carry[...] = base
    count_ref[...] = jnp.broadcast_to(base, count_ref.shape)


def _expert_ranks(eids, tr=512):
    t = eids.shape[1]
    return pl.pallas_call(
        _rank_kernel,
        grid=(t // tr,),
        in_specs=[pl.BlockSpec((TOP_K, tr), lambda i: (0, i))],
        out_specs=[pl.BlockSpec((TOP_K, tr), lambda i: (0, i)),
                   pl.BlockSpec((N_EXPERTS, LANES), lambda i: (0, 0))],
        out_shape=[jax.ShapeDtypeStruct((TOP_K, t), I32),
                   jax.ShapeDtypeStruct((N_EXPERTS, LANES), F32)],
        scratch_shapes=[pltpu.VMEM((N_EXPERTS, 1), F32)],
        compiler_params=_cparams("arbitrary"),
        name="expert_ranks",
    )(eids)


def _dest_kernel(pstart_ref, eid_ref, rank_ref, dest_ref):
    e = eid_ref[...]
    dest = rank_ref[...]
    for j in range(N_EXPERTS):
        dest = dest + jnp.where(e == j, pstart_ref[j], 0)
    dest_ref[...] = dest


def _dest_rows(p_starts, eids, ranks, tm=2048):
    t = eids.shape[1]
    grid_spec = pltpu.PrefetchScalarGridSpec(
        num_scalar_prefetch=1,
        grid=(t // tm,),
        in_specs=[pl.BlockSpec((TOP_K, tm), lambda i, ps: (0, i)),
                  pl.BlockSpec((TOP_K, tm), lambda i, ps: (0, i))],
        out_specs=pl.BlockSpec((TOP_K, tm), lambda i, ps: (0, i)))
    return pl.pallas_call(
        _dest_kernel,
        grid_spec=grid_spec,
        out_shape=jax.ShapeDtypeStruct((TOP_K, t), I32),
        compiler_params=_cparams("arbitrary"),
        name="dest_rows",
    )(p_starts, eids, ranks)


ROW_TILE = IDX_CHUNK // TOP_K


def _tile_indices(idx_hbm, idx_smem, idx_sems, tile, slot):
    return pltpu.make_async_copy(idx_hbm.at[pl.ds(tile * IDX_CHUNK, IDX_CHUNK)],
                                 idx_smem.at[pl.ds(slot * IDX_CHUNK, IDX_CHUNK)], idx_sems.at[slot])


def _dispatch_kernel(pend_ref, padded_ref, nu_ref, idx_hbm, h_ref, xb_hbm, idx_smem, zero_blk, idx_sems, row_sem,
                     zero_sem):
    def zero_block(start):
        return pltpu.make_async_copy(zero_blk, xb_hbm.at[pl.ds(pl.multiple_of(start, MOE_BLOCK), MOE_BLOCK)],
                                     zero_sem)

    @pl.when(pl.program_id(0) == 0)
    def _():
        zero_blk[...] = jnp.zeros_like(zero_blk)
        for e in range(N_EXPERTS):
            @pl.when(padded_ref[e] > 0)
            def _(e=e):
                zero_block(pend_ref[e] - MOE_BLOCK).start()
        for e in range(N_EXPERTS):
            @pl.when(padded_ref[e] > 0)
            def _(e=e):
                zero_block(pend_ref[e] - MOE_BLOCK).wait()

        def zero_unused(b, carry):
            copy = zero_block(b * MOE_BLOCK)
            copy.start()
            copy.wait()
            return carry

        lax.fori_loop(nu_ref[0], xb_hbm.shape[0] // MOE_BLOCK, zero_unused, 0)

    i = pl.program_id(0)

    @pl.when(i == 0)
    def _():
        _tile_indices(idx_hbm, idx_smem, idx_sems, i, 0).start()

    def row_copy(r, dst_row):
        return pltpu.make_async_copy(h_ref.at[pl.ds(r, 1)], xb_hbm.at[pl.ds(dst_row, 1)], row_sem)

    def step_for_slot(slot):
        _tile_indices(idx_hbm, idx_smem, idx_sems, i, slot).wait()

        @pl.when(i + 1 < pl.num_programs(0))
        def _():
            _tile_indices(idx_hbm, idx_smem, idx_sems, i + 1, 1 - slot).start()

        def issue(r, carry):
            for k in range(TOP_K):
                row_copy(r, idx_smem[slot * IDX_CHUNK + k * ROW_TILE + r]).start(priority=k)
            return carry

        lax.fori_loop(0, ROW_TILE, issue, 0, unroll=8)

    for parity in range(2):
        pl.when(i % 2 == parity)(functools.partial(step_for_slot, parity))

    def drain(r, carry):
        for k in range(TOP_K):
            row_copy(0, 0).wait()
        return carry

    lax.fori_loop(0, ROW_TILE, drain, 0, unroll=8)


def _dispatch(p_ends, padded, n_used, idx_tiles, h2, dst_rows):
    t, d = h2.shape
    any_spec = pl.BlockSpec(memory_space=pl.ANY)
    grid_spec = pltpu.PrefetchScalarGridSpec(
        num_scalar_prefetch=3,
        grid=(t // ROW_TILE,),
        in_specs=[any_spec, pl.BlockSpec((ROW_TILE, d), lambda i, pe, pd, nu: (i, 0))],
        out_specs=any_spec,
        scratch_shapes=[pltpu.SMEM((2 * IDX_CHUNK,), I32), pltpu.VMEM((MOE_BLOCK, d), h2.dtype),
                        pltpu.SemaphoreType.DMA((2,)), pltpu.SemaphoreType.DMA, pltpu.SemaphoreType.DMA])
    return pl.pallas_call(
        _dispatch_kernel,
        grid_spec=grid_spec,
        out_shape=jax.ShapeDtypeStruct((dst_rows, d), h2.dtype),
        compiler_params=_cparams("arbitrary"),
        name="moe_dispatch",
    )(p_ends, padded, n_used, idx_tiles, h2)


def _expert_kernel(be_ref, nu_ref, x_ref, wg_ref, wu_ref, wd_ref, y_ref, wg_b, wu_b, wd_b):
    b = pl.program_id(0)

    @pl.when(b < nu_ref[0])
    def _():
        @pl.when(jnp.logical_or(b == 0, be_ref[b] != be_ref[jnp.maximum(b - 1, 0)]))
        def _():
            wg_b[...] = wg_ref[0].astype(BF16)
            wu_b[...] = wu_ref[0].astype(BF16)
            wd_b[...] = wd_ref[0].astype(BF16)

        x = jnp.concatenate(_unpack_bf16_pairs(x_ref[...]), axis=1).astype(BF16)
        a = _silu(_dot(x, wg_b[...])) * _dot(x, wu_b[...])
        y_ref[...] = _pack_bf16_pairs(_dot(a.astype(BF16), wd_b[...]))

    @pl.when(b >= nu_ref[0])
    def _():
        y_ref[...] = jnp.zeros_like(y_ref)


def _expert_mlp(block_e, n_used, xb, w_gate, w_up, w_down):
    p, words = xb.shape
    d, de = w_gate.shape[1:]
    blk = lambda b, be, nu: (jnp.minimum(b, nu[0] - 1), 0)
    grid_spec = pltpu.PrefetchScalarGridSpec(
        num_scalar_prefetch=2,
        grid=(p // MOE_BLOCK,),
        in_specs=[pl.BlockSpec((MOE_BLOCK, words), blk),
                  pl.BlockSpec((1, d, de), lambda b, be, nu: (be[b], 0, 0)),
                  pl.BlockSpec((1, d, de), lambda b, be, nu: (be[b], 0, 0)),
                  pl.BlockSpec((1, de, d), lambda b, be, nu: (be[b], 0, 0))],
        out_specs=pl.BlockSpec((MOE_BLOCK, words), lambda b, be, nu: (b, 0)),
        scratch_shapes=[pltpu.VMEM((d, de), BF16), pltpu.VMEM((d, de), BF16), pltpu.VMEM((de, d), BF16)])
    return pl.pallas_call(
        _expert_kernel,
        grid_spec=grid_spec,
        out_shape=jax.ShapeDtypeStruct((p, words), jnp.uint32),
        compiler_params=_cparams("arbitrary"),
        name="expert_mlp",
    )(block_e, n_used, xb, w_gate, w_up, w_down)


def _combine_kernel(idx_hbm, yb_hbm, x_ref, gcol_ref, mod_ref, wnf_ref, o_ref, ybuf, idx_smem, idx_sems, row_sems,
                    *, final):
    i = pl.program_id(0)
    n = pl.num_programs(0)

    indices = functools.partial(_tile_indices, idx_hbm, idx_smem, idx_sems)

    def row_copy(src_row, slot, k, r):
        return pltpu.make_async_copy(yb_hbm.at[pl.ds(src_row, 1)], ybuf.at[slot, k, pl.ds(r, 1)],
                                     row_sems.at[slot])

    def issue_tile(slot):
        def issue(r, carry):
            for k in range(TOP_K):
                row_copy(idx_smem[slot * IDX_CHUNK + k * ROW_TILE + r], slot, k, r).start(priority=k)
            return carry

        lax.fori_loop(0, ROW_TILE, issue, 0, unroll=8)

    @pl.when(i == 0)
    def _():
        first = indices(i, 0)
        first.start()
        first.wait()
        issue_tile(0)

        @pl.when(n > 1)
        def _():
            indices(i + 1, 1).start()

    def step_for_slot(slot):
        other = 1 - slot

        @pl.when(i + 1 < n)
        def _():
            indices(i + 1, other).wait()

        @pl.when(i + 2 < n)
        def _():
            indices(i + 2, slot).start()

        @pl.when(i + 1 < n)
        def _():
            issue_tile(other)

        def drain(r, carry):
            for k in range(TOP_K):
                row_copy(0, slot, k, 0).wait()
            return carry

        lax.fori_loop(0, ROW_TILE, drain, 0, unroll=8)

        gc = gcol_ref[...]
        y0 = _unpack_bf16_pairs(ybuf[slot, 0])
        y1 = _unpack_bf16_pairs(ybuf[slot, 1])
        moe = jnp.concatenate([gc[:, 0:1] * y0[0] + gc[:, 1:2] * y1[0], gc[:, 0:1] * y0[1] + gc[:, 1:2] * y1[1]],
                              axis=1)
        x2 = x_ref[...] + mod_ref[0][5:6, :] * moe
        o_ref[...] = _rms_norm(x2, wnf_ref[...]) if final else x2

    for parity in range(2):
        pl.when(i % 2 == parity)(functools.partial(step_for_slot, parity))


def _combine(idx_tiles, yb, x2, gcol, mod, w_norm_final, seq, final):
    t, d = x2.shape
    tm = ROW_TILE
    spb = seq // tm
    any_spec = pl.BlockSpec(memory_space=pl.ANY)
    return pl.pallas_call(
        functools.partial(_combine_kernel, final=final),
        grid=(t // tm,),
        in_specs=[any_spec, any_spec,
                  pl.BlockSpec((tm, d), lambda i: (i, 0)),
                  pl.BlockSpec((tm, LANES), lambda i: (i, 0)),
                  pl.BlockSpec((1, 6, d), lambda i: (i // spb, 0, 0)),
                  pl.BlockSpec((1, d), lambda i: (0, 0))],
        out_specs=pl.BlockSpec((tm, d), lambda i: (i, 0)),
        out_shape=jax.ShapeDtypeStruct((t, d), F32),
        scratch_shapes=[pltpu.VMEM((2, TOP_K, tm, d // 2), jnp.uint32), pltpu.SMEM((2 * IDX_CHUNK,), I32),
                        pltpu.SemaphoreType.DMA((2,)), pltpu.SemaphoreType.DMA((2,))],
        compiler_params=_cparams("arbitrary"),
        name="moe_combine",
    )(idx_tiles, yb, x2, gcol, mod, w_norm_final)


def _lane_row(pieces, width=LANES):
    row = jnp.zeros((width,), F32)
    for off, vec in pieces.items():
        row = row.at[off:off + vec.shape[0]].set(vec.astype(F32))
    return row.reshape(1, width)


def _split_w_in(w_in):
    gw, nh = GROUP_WIDTH, N_HEADS
    widths = [gw, gw, gw, nh, gw, gw, gw, 3 * gw, nh, gw, gw, gw, gw, nh, nh]
    cuts, acc = [], 0
    for w in widths[:-1]:
        acc += w
        cuts.append(acc)
    (aq, ak, av, af, su, sv, mz, mxbc, mdt, lq, lk, lv, lo, li, lf) = jnp.split(w_in, cuts, axis=1)
    scale = HEAD_DIM ** -0.5
    wmain = jnp.concatenate([ak, su, sv, mz, mxbc, lq, lk * scale, lv, lo], axis=1).astype(BF16)
    wqt = (aq * scale).T.astype(BF16)
    wvt = av.T.astype(BF16)
    wg = jnp.zeros((w_in.shape[0], LANES), F32)
    for off, w in ((GATE_AF, af), (GATE_DT, mdt), (GATE_LI, li), (GATE_LF, lf)):
        wg = wg.at[:, off:off + nh].set(w)
    return wmain, wqt, wvt, wg.astype(BF16)


def _moe_layer(x1, h2, eids, gcol, mod_l, w_gate, w_up, w_down, layer, w_norm_final, seq, final):
    t, d = x1.shape
    ranks, counts = _expert_ranks(eids)
    counts = counts[:, 0].astype(I32)
    padded = ((counts + MOE_BLOCK - 1) // MOE_BLOCK) * MOE_BLOCK
    p_ends = jnp.cumsum(padded)
    p_starts = (p_ends - padded).astype(I32)
    n_blocks = (t * TOP_K) // MOE_BLOCK + N_EXPERTS
    blocks = jnp.arange(n_blocks, dtype=I32)
    block_e = jnp.sum((p_ends[None, :] <= (blocks * MOE_BLOCK)[:, None]).astype(I32), axis=1)
    block_e = jnp.minimum(block_e, N_EXPERTS - 1)
    n_used = (p_ends[-1:] // MOE_BLOCK).astype(I32)
    block_e = jnp.where(blocks < n_used, block_e, block_e[n_used[0] - 1])
    dest = _dest_rows(p_starts, eids, ranks)
    idx_tiles = dest.reshape(TOP_K, t // ROW_TILE, ROW_TILE).transpose(1, 0, 2).reshape(-1)
    xb = _dispatch(p_ends.astype(I32), padded.astype(I32), n_used, idx_tiles, h2, n_blocks * MOE_BLOCK)
    yb = _expert_mlp(block_e + layer * N_EXPERTS, n_used, xb, w_gate, w_up, w_down)
    return _combine(idx_tiles, yb, x1, gcol, mod_l, w_norm_final, seq, final)


def kernel(x, c, w_in, w_out, w_mix_norm, attn_f_bias, sg_w, sg_b, ssm_conv_w, ssm_conv_b, ssm_dt_bias,
           ssm_a_log, ssm_d, mlstm_i_bias, mlstm_f_bias, w_ada, b_ada, w_norm1, w_norm2, w_router_group,
           b_router_group, w_router_expert, b_router_expert, w_expert_gate, w_expert_up, w_expert_down,
           w_norm_final):
    bsz, seq, d = x.shape
    depth = w_in.shape[0]
    gw = GROUP_WIDTH
    mod = _ada_modulation(c, w_ada, b_ada).reshape(depth, bsz, 6, d)
    x2 = x.reshape(bsz * seq, d)
    wnf = w_norm_final.reshape(1, d)
    w_eg = w_expert_gate.reshape((depth * N_EXPERTS,) + w_expert_gate.shape[2:])
    w_eu = w_expert_up.reshape((depth * N_EXPERTS,) + w_expert_up.shape[2:])
    w_ed = w_expert_down.reshape((depth * N_EXPERTS,) + w_expert_down.shape[2:])
    for l in range(depth):
        wmain, wqt, wvt, wg = _split_w_in(w_in[l])
        gains = w_mix_norm[l].reshape(N_HEADS, 1, gw)
        proj, qt, vt, gates, f2 = _inproj(x2, mod[l], w_norm1[l].reshape(1, d), wmain, wqt, wvt, wg,
                                          _lane_row({GATE_AF: attn_f_bias[l]}), bsz, seq)
        y_attn = _attention(proj, qt, vt, f2, gains[0], bsz, seq)
        sg_bias_full = jnp.repeat(sg_b[l].T, HEAD_DIM, axis=1)
        y_sg = _spatial_gating(proj, gains[1], sg_w[l], sg_bias_full)
        y_ssm = _ssd_mixer(proj, gates, ssm_conv_w[l], ssm_conv_b[l].reshape(1, -1),
                           _lane_row({GATE_DT: ssm_dt_bias[l]}), _lane_row({GATE_DT: ssm_a_log[l]}),
                           jnp.repeat(ssm_d[l], HEAD_DIM).reshape(1, gw), gains[2], bsz, seq)
        y_ml = _mlstm_mixer(proj, gates, _lane_row({GATE_LI: mlstm_i_bias[l], GATE_LF: mlstm_f_bias[l]}),
                            gains[3], bsz, seq)
        w_route = jnp.zeros((d, LANES), F32)
        w_route = w_route.at[:, ROUTE_G:ROUTE_G + N_EXPERT_GROUPS].set(w_router_group[l])
        w_route = w_route.at[:, ROUTE_E:ROUTE_E + N_EXPERTS].set(w_router_expert[l]).astype(BF16)
        b_route = _lane_row({ROUTE_G: b_router_group[l], ROUTE_E: b_router_expert[l]})
        x1, h2, eids, gcol = _outproj_router(x2, (y_attn, y_sg, y_ssm, y_ml), w_out[l].astype(BF16), mod[l],
                                             w_norm2[l].reshape(1, d), w_route, b_route, seq)
        x2 = _moe_layer(x1, h2, eids, gcol, mod[l], w_eg, w_eu, w_ed, l, wnf, seq, final=(l == depth - 1))
    return x2.reshape(bsz, seq, d)
```

```python
import functools

import jax
import jax.numpy as jnp
from jax import lax
from jax.experimental import pallas as pl
from jax.experimental.pallas import tpu as pltpu

F32 = jnp.float32
BF16 = jnp.bfloat16
I32 = jnp.int32

LANES = 128
SUBLANES = 8
HEAD_DIM = 64
N_HEADS = 4
GROUP_WIDTH = 256
CHUNK = 128
MIX_TILE = 4 * CHUNK
SSM_GROUPS = 2
CONV_WIDTH = 4
N_EXPERT_GROUPS = 4
EXPERTS_PER_GROUP = 8
N_EXPERTS = N_EXPERT_GROUPS * EXPERTS_PER_GROUP
TOP_K = 2
MOE_BLOCK = 512
NORM_EPS = 1e-6
MLSTM_M_INIT = -1e30
NEG_INF = float("-inf")
VMEM_LIMIT_BYTES = 48 * 1024 * 1024
IDX_CHUNK = 1024

(PB_K, PB_SU, PB_SV, PB_MZ, PB_X, PB_B, PB_C, PB_LQ, PB_LK, PB_LV, PB_LO) = range(11)
PROJ_COLS = 11 * GROUP_WIDTH
LOG2_E = 1.4426950408889634
GATE_AF, GATE_DT, GATE_LI, GATE_LF = 0, 4, 8, 12
ROUTE_G, ROUTE_E = 0, 8

NT_DIMS = (((1,), (1,)), ((), ()))


def _cparams(*sem):
    return pltpu.CompilerParams(dimension_semantics=sem, vmem_limit_bytes=VMEM_LIMIT_BYTES)


def _dot(a, b):
    return jnp.dot(a, b, preferred_element_type=F32)


def _dot_nt(a, b):
    return lax.dot_general(a, b, NT_DIMS, preferred_element_type=F32)


def _dot_exact(a, b):
    return jnp.dot(a, b, preferred_element_type=F32, precision=lax.Precision.HIGHEST)


def _head_of_lane(shape, axis=1):
    return lax.broadcasted_iota(I32, shape, axis) // HEAD_DIM


def _sigmoid(x):
    return 1.0 / (1.0 + jnp.exp(-x))


def _silu(x):
    return x * _sigmoid(x)


def _log_sigmoid(x):
    return jnp.minimum(x, 0.0) - jnp.log1p(jnp.exp(-jnp.abs(x)))


def _softplus(x):
    return jnp.maximum(x, 0.0) + jnp.log1p(jnp.exp(-jnp.abs(x)))


def _gelu_tanh(x):
    return 0.5 * x * (1.0 + jnp.tanh(0.7978845608028654 * (x + 0.044715 * (x * x * x))))


def _expand_heads(cols, width=GROUP_WIDTH):
    rows = cols[0].shape[0]
    head = _head_of_lane((rows, width))
    out = jnp.broadcast_to(cols[N_HEADS - 1], (rows, width))
    for h in range(N_HEADS - 2, -1, -1):
        out = jnp.where(head == h, jnp.broadcast_to(cols[h], (rows, width)), out)
    return out


def _mask_head(x, h):
    return jnp.where(_head_of_lane(x.shape) == h, x, jnp.zeros_like(x))


def _head_rms_norm(y, gain):
    head = _head_of_lane(y.shape)
    sq = y * y
    cols = [jnp.sum(jnp.where(head == h, sq, 0.0), axis=1, keepdims=True) * (1.0 / HEAD_DIM)
            for h in range(N_HEADS)]
    return y * lax.rsqrt(_expand_heads(cols) + NORM_EPS) * gain


def _rms_norm(x, gain):
    ms = jnp.mean(x * x, axis=1, keepdims=True)
    return x * lax.rsqrt(ms + NORM_EPS) * gain


def _pack_bf16_pairs(x):
    half = x.shape[1] // 2
    bits = lax.bitcast_convert_type(x.astype(BF16).astype(F32), jnp.uint32)
    return bits[:, :half] | lax.shift_right_logical(bits[:, half:], jnp.uint32(16))


def _unpack_bf16_pairs(words):
    first = lax.bitcast_convert_type(words & jnp.uint32(0xFFFF0000), F32)
    second = lax.bitcast_convert_type(lax.shift_left(words, jnp.uint32(16)), F32)
    return first, second


def _cumsum_rows(x):
    tri = jnp.where(_tril(x.shape[0]), 1.0, 0.0).astype(BF16)
    hi = x.astype(BF16)
    rest = x - hi.astype(F32)
    mid = rest.astype(BF16)
    lo = (rest - mid.astype(F32)).astype(BF16)
    return _dot(tri, hi) + _dot(tri, mid) + _dot(tri, lo)


def _tril(n, strict=False):
    r = lax.broadcasted_iota(I32, (n, n), 0)
    c = lax.broadcasted_iota(I32, (n, n), 1)
    return (r > c) if strict else (r >= c)


def _ada_kernel(c_ref, w_ref, b_ref, o_ref):
    o_ref[0] = _dot_exact(_silu(c_ref[...]), w_ref[0]) + b_ref[0]


def _ada_modulation(c, w_ada, b_ada):
    depth, d, d6 = w_ada.shape
    bsz = c.shape[0]
    return pl.pallas_call(
        _ada_kernel,
        grid=(depth, d6 // d),
        in_specs=[pl.BlockSpec((bsz, d), lambda l, j: (0, 0)),
                  pl.BlockSpec((1, d, d), lambda l, j: (l, 0, j)),
                  pl.BlockSpec((1, 1, d), lambda l, j: (l, 0, j))],
        out_specs=pl.BlockSpec((1, bsz, d), lambda l, j: (l, 0, j)),
        out_shape=jax.ShapeDtypeStruct((depth, bsz, d6), F32),
        compiler_params=_cparams("arbitrary", "arbitrary"),
        name="ada_modulation",
    )(c, w_ada, b_ada.reshape(depth, 1, d6))


def _forget_bias_slab(gates, carry):
    tb = gates.shape[0]
    cum = _cumsum_rows(_log_sigmoid(gates)) + carry[...]
    carry[...] = cum[tb - 1:tb, :]
    f2 = cum * LOG2_E
    hi = f2.astype(BF16)
    rest = f2 - hi.astype(F32)
    mid = rest.astype(BF16)
    lo = (rest - mid.astype(F32)).astype(BF16)
    r = lax.broadcasted_iota(I32, (LANES, GROUP_WIDTH), 0) - GATE_AF
    c = lax.broadcasted_iota(I32, (LANES, GROUP_WIDTH), 1)
    slab = jnp.zeros((tb, GROUP_WIDTH), F32)
    for j, piece in enumerate((hi, mid, lo)):
        place = (r >= 0) & (r < N_HEADS) & (c == ((r + 1) % N_HEADS) * HEAD_DIM + j)
        slab = slab + _dot(piece, jnp.where(place, -1.0, 0.0).astype(BF16))
    return slab.astype(BF16)


def _inproj_kernel(x_ref, mod_ref, wn_ref, wmain_ref, wqt_ref, wvt_ref, wg_ref, fbias_ref,
                   proj_ref, qt_ref, vt_ref, gates_ref, slab_ref, h_scr, f_carry, *, tiles_per_seq):
    @pl.when(pl.program_id(0) % tiles_per_seq == 0)
    def _():
        f_carry[...] = jnp.zeros_like(f_carry)

    mod = mod_ref[0]
    h = _rms_norm(x_ref[...], wn_ref[...]) * (1.0 + mod[1:2, :]) + mod[0:1, :]
    h_scr[...] = h.astype(BF16)
    for c0 in range(0, PROJ_COLS, GROUP_WIDTH):
        proj_ref[:, c0:c0 + GROUP_WIDTH] = _dot(h_scr[...], wmain_ref[:, c0:c0 + GROUP_WIDTH]).astype(BF16)
    qt_ref[0] = (_dot_nt(wqt_ref[...], h_scr[...]) * LOG2_E).astype(BF16)
    vt_ref[0] = _dot_nt(wvt_ref[...], h_scr[...]).astype(BF16)
    gates = _dot(h_scr[...], wg_ref[...])
    gates_ref[...] = gates
    slab_ref[...] = _forget_bias_slab(gates + fbias_ref[...], f_carry)


def _inproj(x2, mod, w_norm, wmain, wqt, wvt, wg, f_bias_row, bsz, seq, tm=512):
    t, d = x2.shape
    spb = seq // tm
    tspec = pl.BlockSpec((1, GROUP_WIDTH, tm), lambda i: (i // spb, 0, i % spb))
    return pl.pallas_call(
        functools.partial(_inproj_kernel, tiles_per_seq=spb),
        grid=(t // tm,),
        in_specs=[pl.BlockSpec((tm, d), lambda i: (i, 0)),
                  pl.BlockSpec((1, 6, d), lambda i: (i // spb, 0, 0)),
                  pl.BlockSpec((1, d), lambda i: (0, 0)),
                  pl.BlockSpec((d, PROJ_COLS), lambda i: (0, 0)),
                  pl.BlockSpec((GROUP_WIDTH, d), lambda i: (0, 0)),
                  pl.BlockSpec((GROUP_WIDTH, d), lambda i: (0, 0)),
                  pl.BlockSpec((d, LANES), lambda i: (0, 0)),
                  pl.BlockSpec((1, LANES), lambda i: (0, 0))],
        out_specs=[pl.BlockSpec((tm, PROJ_COLS), lambda i: (i, 0)), tspec, tspec,
                   pl.BlockSpec((tm, LANES), lambda i: (i, 0)),
                   pl.BlockSpec((tm, GROUP_WIDTH), lambda i: (i, 0))],
        out_shape=[jax.ShapeDtypeStruct((t, PROJ_COLS), BF16),
                   jax.ShapeDtypeStruct((bsz, GROUP_WIDTH, seq), BF16),
                   jax.ShapeDtypeStruct((bsz, GROUP_WIDTH, seq), BF16),
                   jax.ShapeDtypeStruct((t, LANES), F32),
                   jax.ShapeDtypeStruct((t, GROUP_WIDTH), BF16)],
        scratch_shapes=[pltpu.VMEM((tm, d), BF16), pltpu.VMEM((1, LANES), F32)],
        compiler_params=_cparams("arbitrary"),
        name="norm_inproj",
    )(x2, mod, w_norm, wmain, wqt, wvt, wg, f_bias_row)


ONES_ROWS = 16
QUERY_SPLITS = 2


def _attn_kernel(qi_ref, kj_ref, k_ref, qt_ref, vt_ref, f_ref, gain_ref, o_ref, qtm, m_s, l_s, acc):
    p = pl.program_id(1)
    qi, kj = qi_ref[p], kj_ref[p]
    tk = k_ref.shape[0]
    tq = qt_ref.shape[2]

    @pl.when(kj == 0)
    def _():
        qt = qt_ref[0]
        row = lax.broadcasted_iota(I32, qt.shape, 0)
        for h in range(N_HEADS):
            slot = ((h + 1) % N_HEADS) * HEAD_DIM
            ones_rows = jnp.where((row >= slot) & (row < slot + 3), 1.0, 0.0).astype(BF16)
            qtm[h] = jnp.where(row // HEAD_DIM == h, qt, ones_rows)
        m_s[...] = jnp.full_like(m_s, NEG_INF)
        l_s[...] = jnp.zeros_like(l_s)
        acc[...] = jnp.zeros_like(acc)

    def step(diagonal):
        k = k_ref[...]
        bias = f_ref[...]
        head = _head_of_lane(k.shape)
        ones = jnp.ones((ONES_ROWS, tk), BF16)
        if diagonal:
            visible = lax.broadcasted_iota(I32, (tk, tq), 0) <= lax.broadcasted_iota(I32, (tk, tq), 1)
        scores, m_news, alphas, probs = {}, {}, {}, {}
        k_aug = [jnp.where(head == h, k, bias) for h in range(N_HEADS)]
        splits = 1 if diagonal else QUERY_SPLITS
        half = tq // splits
        units = [(h, c) for h in range(N_HEADS) for c in range(splits)]

        def score(u):
            h, c = units[u]
            cols = slice(c * half, (c + 1) * half)
            s = _dot(k_aug[h], qtm[h, :, cols])
            if diagonal:
                s = jnp.where(visible[:, cols], s, NEG_INF)
            scores[u] = s
            m_prev = m_s[h, :, cols]
            m_news[u] = jnp.maximum(m_prev, jnp.max(s, axis=0, keepdims=True))
            alphas[u] = jnp.exp2(m_prev - m_news[u])
            m_s[h, :, cols] = m_news[u]

        def prob(u):
            probs[u] = jnp.exp2(scores[u] - m_news[u]).astype(BF16)

        def weighted_sum(u):
            h, c = units[u]
            cols = slice(c * half, (c + 1) * half)
            vt_ext = jnp.concatenate([vt_ref[0, h * HEAD_DIM:(h + 1) * HEAD_DIM, :], ones], axis=0)
            pv = _dot(vt_ext, probs[u])
            acc[h, :, cols] = alphas[u] * acc[h, :, cols] + pv[0:HEAD_DIM, :]
            l_s[h, :, cols] = alphas[u] * l_s[h, :, cols] + pv[HEAD_DIM:HEAD_DIM + 1, :]

        for stage in range(len(units) + 2):
            if stage < len(units):
                score(stage)
            if 0 <= stage - 1 < len(units):
                prob(stage - 1)
            if 0 <= stage - 2 < len(units):
                weighted_sum(stage - 2)

    @pl.when(kj < qi)
    def _():
        step(False)

    @pl.when(kj == qi)
    def _():
        step(True)
        yt = jnp.concatenate([acc[h] / l_s[h] for h in range(N_HEADS)], axis=0)
        o_ref[...] = _head_rms_norm(yt.T, gain_ref[...]).astype(BF16)


def _attention(proj, qt, vt, f2, gain, bsz, seq, tq=1024):
    nq = seq // tq
    pairs = [(i, j) for i in range(nq) for j in range(i + 1)]
    qi = jnp.asarray([a for a, _ in pairs], I32)
    kj = jnp.asarray([b for _, b in pairs], I32)
    t = bsz * seq
    grid_spec = pltpu.PrefetchScalarGridSpec(
        num_scalar_prefetch=2,
        grid=(bsz, len(pairs)),
        in_specs=[pl.BlockSpec((tq, GROUP_WIDTH), lambda b, p, qi, kj: (b * nq + kj[p], PB_K)),
                  pl.BlockSpec((1, GROUP_WIDTH, tq), lambda b, p, qi, kj: (b, 0, qi[p])),
                  pl.BlockSpec((1, GROUP_WIDTH, tq), lambda b, p, qi, kj: (b, 0, kj[p])),
                  pl.BlockSpec((tq, GROUP_WIDTH), lambda b, p, qi, kj: (b * nq + kj[p], 0)),
                  pl.BlockSpec((1, GROUP_WIDTH), lambda b, p, qi, kj: (0, 0))],
        out_specs=pl.BlockSpec((tq, GROUP_WIDTH), lambda b, p, qi, kj: (b * nq + qi[p], 0)),
        scratch_shapes=[pltpu.VMEM((N_HEADS, GROUP_WIDTH, tq), BF16),
                        pltpu.VMEM((N_HEADS, 1, tq), F32),
                        pltpu.VMEM((N_HEADS, 1, tq), F32),
                        pltpu.VMEM((N_HEADS, HEAD_DIM, tq), F32)])
    return pl.pallas_call(
        _attn_kernel,
        grid_spec=grid_spec,
        out_shape=jax.ShapeDtypeStruct((t, GROUP_WIDTH), BF16),
        compiler_params=_cparams("arbitrary", "arbitrary"),
        name="fox_attention",
    )(qi, kj, proj, qt, vt, f2, gain)


def _sg_kernel(u_ref, v_ref, gain_ref, w_ref, b_ref, o_ref):
    tm = u_ref.shape[0]
    u = _gelu_tanh(u_ref[...].astype(F32))
    v = _head_rms_norm(_gelu_tanh(v_ref[...].astype(F32)), gain_ref[...])
    causal = _tril(CHUNK)
    ws = [jnp.where(causal, w_ref[h], 0.0).astype(BF16) for h in range(N_HEADS)]
    for c0 in range(0, tm, CHUNK):
        vc = v[c0:c0 + CHUNK, :].astype(BF16)
        mixed = b_ref[...]
        for h in range(N_HEADS):
            mixed = mixed + _dot(ws[h], _mask_head(vc, h))
        o_ref[c0:c0 + CHUNK, :] = (u[c0:c0 + CHUNK, :] * mixed).astype(BF16)


def _spatial_gating(proj, gain, sg_w, sg_bias_full, tm=512):
    t = proj.shape[0]
    return pl.pallas_call(
        _sg_kernel,
        grid=(t // tm,),
        in_specs=[pl.BlockSpec((tm, GROUP_WIDTH), lambda i: (i, PB_SU)),
                  pl.BlockSpec((tm, GROUP_WIDTH), lambda i: (i, PB_SV)),
                  pl.BlockSpec((1, GROUP_WIDTH), lambda i: (0, 0)),
                  pl.BlockSpec((N_HEADS, CHUNK, CHUNK), lambda i: (0, 0, 0)),
                  pl.BlockSpec((CHUNK, GROUP_WIDTH), lambda i: (0, 0))],
        out_specs=pl.BlockSpec((tm, GROUP_WIDTH), lambda i: (i, 0)),
        out_shape=jax.ShapeDtypeStruct((t, GROUP_WIDTH), BF16),
        compiler_params=_cparams("arbitrary"),
        name="spatial_gating",
    )(proj, proj, gain, sg_w, sg_bias_full)


def _ssd_kernel(z_ref, x_ref, b_ref, c_ref, g_ref, cw_ref, cb_ref, dtb_ref, alog_ref, dskip_ref, gain_ref,
                o_ref, conv_scr, xbc_scr, state):
    L = CHUNK
    W = GROUP_WIDTH
    tile = x_ref.shape[0]

    @pl.when(pl.program_id(1) == 0)
    def _():
        conv_scr[0:SUBLANES, :] = jnp.zeros((SUBLANES, 3 * W), F32)
        state[...] = jnp.zeros_like(state)

    conv_scr[SUBLANES:, 0:W] = x_ref[...].astype(F32)
    conv_scr[SUBLANES:, W:2 * W] = b_ref[...].astype(F32)
    conv_scr[SUBLANES:, 2 * W:] = c_ref[...].astype(F32)
    cw = cw_ref[...]
    conv = cb_ref[...] + cw[CONV_WIDTH - 1:CONV_WIDTH, :] * conv_scr[SUBLANES:, :]
    for s in range(1, CONV_WIDTH):
        conv = conv + cw[CONV_WIDTH - 1 - s:CONV_WIDTH - s, :] * conv_scr[SUBLANES - s:SUBLANES - s + tile, :]
    conv_scr[0:SUBLANES, :] = conv_scr[tile:tile + SUBLANES, :]
    xbc_scr[...] = _silu(conv)

    neg_a = -jnp.exp(alog_ref[...])
    for c0 in range(0, tile, L):
        rows = slice(c0, c0 + L)
        dt = _softplus(g_ref[rows, :] + dtb_ref[...])
        y = _ssd_chunk(xbc_scr[rows, 0:W], xbc_scr[rows, W:2 * W], xbc_scr[rows, 2 * W:], dt, dt * neg_a,
                       dskip_ref[...], state)
        y = y * _silu(z_ref[rows, :].astype(F32))
        o_ref[rows, :] = _head_rms_norm(y, gain_ref[...]).astype(BF16)


def _ssd_chunk(xs, bm, cm, dt, da, dskip, state):
    L = CHUNK
    a_cum = _cumsum_rows(da)
    a_row = a_cum.T
    dt_cols = [dt[:, GATE_DT + h:GATE_DT + h + 1] for h in range(N_HEADS)]
    a_cols = [a_cum[:, GATE_DT + h:GATE_DT + h + 1] for h in range(N_HEADS)]
    a_end = [a_cum[L - 1:L, GATE_DT + h:GATE_DT + h + 1] for h in range(N_HEADS)]
    xdt = xs * _expand_heads(dt_cols)
    xdt_b = xdt.astype(BF16)
    xw = (xdt * _expand_heads([jnp.exp(a_end[h] - a_cols[h]) for h in range(N_HEADS)])).astype(BF16)
    exp_a = _expand_heads([jnp.exp(a_cols[h]) for h in range(N_HEADS)])
    causal = _tril(L)
    half = lax.broadcasted_iota(I32, (1, LANES), 1) < HEAD_DIM

    heads_per_group = N_HEADS // SSM_GROUPS
    bgs = [bm[:, g * LANES:(g + 1) * LANES] for g in range(SSM_GROUPS)]
    cgs = [cm[:, g * LANES:(g + 1) * LANES].astype(BF16) for g in range(SSM_GROUPS)]
    scores = [_dot_nt(cgs[g], bgs[g].astype(BF16)) for g in range(SSM_GROUPS)]
    y_off = [_dot(cgs[g], state[g].astype(BF16)) for g in range(SSM_GROUPS)]
    for g in range(SSM_GROUPS):
        chunk_decay = jnp.where(half, jnp.exp(a_end[2 * g]), jnp.exp(a_end[2 * g + 1]))
        state[g] = chunk_decay * state[g] + _dot(bgs[g].T.astype(BF16), xw[:, g * LANES:(g + 1) * LANES])
    decays = [jnp.exp(jnp.where(causal, a_cols[h] - a_row[GATE_DT + h:GATE_DT + h + 1, :], NEG_INF))
              for h in range(N_HEADS)]
    y = dskip * xs
    for h in range(N_HEADS):
        y = y + _dot((scores[h // heads_per_group] * decays[h]).astype(BF16), _mask_head(xdt_b, h))
    return y + jnp.concatenate(y_off, axis=1) * exp_a


def _ssd_mixer(proj, gates, conv_w, conv_b, dtb_row, alog_row, dskip_row, gain, bsz, seq, tile=MIX_TILE):
    t = proj.shape[0]
    nc = seq // tile
    row = lambda blk: pl.BlockSpec((tile, GROUP_WIDTH), lambda b, j, blk=blk: (b * nc + j, blk))
    const = lambda shape: pl.BlockSpec(shape, lambda b, j: (0,) * len(shape))
    return pl.pallas_call(
        _ssd_kernel,
        grid=(bsz, nc),
        in_specs=[row(PB_MZ), row(PB_X), row(PB_B), row(PB_C),
                  pl.BlockSpec((tile, LANES), lambda b, j: (b * nc + j, 0)),
                  const((CONV_WIDTH, 3 * GROUP_WIDTH)), const((1, 3 * GROUP_WIDTH)),
                  const((1, LANES)), const((1, LANES)), const((1, GROUP_WIDTH)), const((1, GROUP_WIDTH))],
        out_specs=pl.BlockSpec((tile, GROUP_WIDTH), lambda b, j: (b * nc + j, 0)),
        out_shape=jax.ShapeDtypeStruct((t, GROUP_WIDTH), BF16),
        scratch_shapes=[pltpu.VMEM((tile + SUBLANES, 3 * GROUP_WIDTH), F32),
                        pltpu.VMEM((tile, 3 * GROUP_WIDTH), F32),
                        pltpu.VMEM((SSM_GROUPS, LANES, LANES), F32)],
        compiler_params=_cparams("arbitrary", "arbitrary"),
        name="ssd_mixer",
    )(proj, proj, proj, proj, gates, conv_w, conv_b, dtb_row, alog_row, dskip_row, gain)


def _mlstm_kernel(q_ref, k_ref, v_ref, o_gate_ref, g_ref, bias_ref, gain_ref, o_ref, ct, nb, m_row):
    L = CHUNK
    W = GROUP_WIDTH

    @pl.when(pl.program_id(1) == 0)
    def _():
        ct[...] = jnp.zeros_like(ct)
        nb[...] = jnp.zeros_like(nb)
        m_row[...] = jnp.full_like(m_row, MLSTM_M_INIT)

    def chunk(c, carry):
        rows = pl.ds(pl.multiple_of(c * L, L), L)
        y = _mlstm_chunk(q_ref[rows, :], k_ref[rows, :], v_ref[rows, :], g_ref[rows, :] + bias_ref[...],
                         gain_ref[...], ct, nb, m_row)
        o_ref[rows, :] = (_sigmoid(o_gate_ref[rows, :].astype(F32)) * y).astype(BF16)
        return carry

    lax.fori_loop(0, q_ref.shape[0] // L, chunk, 0)


def _mlstm_chunk(q, k, v, gate, gain, ct, nb, m_row):
    L = CHUNK
    W = GROUP_WIDTH
    a_full = _cumsum_rows(_log_sigmoid(gate))
    a_rows = a_full.T
    g_rows = gate.T
    causal = _tril(L)
    lane = lax.broadcasted_iota(I32, (1, LANES), 1)

    inter_q = _dot(q, ct[...].astype(BF16))
    n_q = _dot(q, nb[...].astype(BF16))
    m_old = m_row[...]
    num = jnp.zeros((L, W), F32)
    inter_cols, den_cols, ws_cols, scale_cols = [], [], [], []
    m_next = m_old
    qk = [_dot_nt(_mask_head(q, h), k) for h in range(N_HEADS)]
    decays, m_ts = [], []
    for h in range(N_HEADS):
        a_col = a_full[:, GATE_LF + h:GATE_LF + h + 1]
        i_col = gate[:, GATE_LI + h:GATE_LI + h + 1]
        a_r = a_rows[GATE_LF + h:GATE_LF + h + 1, :]
        i_r = g_rows[GATE_LI + h:GATE_LI + h + 1, :]
        a_end = a_full[L - 1:L, GATE_LF + h:GATE_LF + h + 1]
        m_in = m_old[:, h:h + 1]
        src = jnp.where(causal, jnp.broadcast_to(i_r - a_r, (L, L)), NEG_INF)
        log_inter = a_col + m_in
        m_t = jnp.maximum(a_col + jnp.max(src, axis=1, keepdims=True), log_inter)
        decays.append(jnp.exp(src + (a_col - m_t)))
        m_ts.append(m_t)
        inter_cols.append(jnp.exp(log_inter - m_t))
        g_col = a_end - a_col + i_col
        m_new = jnp.maximum(a_end + m_in, jnp.max(g_col, axis=0, keepdims=True))
        ws_cols.append(jnp.exp(g_col - m_new))
        scale_cols.append(jnp.exp(a_end + m_in - m_new))
        m_next = jnp.where(lane == h, m_new, m_next)
    for h in range(N_HEADS):
        w = qk[h] * decays[h]
        num = num + _dot(w.astype(BF16), _mask_head(v, h))
        den = jnp.sum(w, axis=1, keepdims=True) + inter_cols[h] * n_q[:, h:h + 1]
        den_cols.append(jnp.maximum(jnp.abs(den), jnp.exp(-m_ts[h])))
    n = num + _expand_heads(inter_cols) * inter_q
    head = _head_of_lane(n.shape)
    sq = n * n
    norm_cols = [lax.rsqrt(jnp.sum(jnp.where(head == h, sq, 0.0), axis=1, keepdims=True) * (1.0 / HEAD_DIM)
                           + NORM_EPS * den_cols[h] * den_cols[h]) for h in range(N_HEADS)]
    hout = n * _expand_heads(norm_cols) * gain

    kw_t = (k.astype(F32) * _expand_heads(ws_cols)).T.astype(BF16)
    scale_row = _expand_heads(scale_cols)
    same_head = _head_of_lane((W, W), 0) == _head_of_lane((W, W), 1)
    ct[...] = scale_row * ct[...] + jnp.where(same_head, _dot(kw_t, v), 0.0)
    col_is_head = _head_of_lane((W, LANES), 0) == lax.broadcasted_iota(I32, (W, LANES), 1)
    scale_n = scale_cols[N_HEADS - 1]
    for h in range(N_HEADS - 2, -1, -1):
        scale_n = jnp.where(lane == h, scale_cols[h], scale_n)
    nb[...] = scale_n * nb[...] + jnp.where(col_is_head, _dot(kw_t, jnp.ones((L, LANES), BF16)), 0.0)
    m_row[...] = m_next
    return hout


def _mlstm_mixer(proj, gates, bias_row, gain, bsz, seq, tile=MIX_TILE):
    t = proj.shape[0]
    nc = seq // tile
    row = lambda blk: pl.BlockSpec((tile, GROUP_WIDTH), lambda b, j, blk=blk: (b * nc + j, blk))
    const = lambda shape: pl.BlockSpec(shape, lambda b, j: (0,) * len(shape))
    return pl.pallas_call(
        _mlstm_kernel,
        grid=(bsz, nc),
        in_specs=[row(PB_LQ), row(PB_LK), row(PB_LV), row(PB_LO),
                  pl.BlockSpec((tile, LANES), lambda b, j: (b * nc + j, 0)),
                  const((1, LANES)), const((1, GROUP_WIDTH))],
        out_specs=pl.BlockSpec((tile, GROUP_WIDTH), lambda b, j: (b * nc + j, 0)),
        out_shape=jax.ShapeDtypeStruct((t, GROUP_WIDTH), BF16),
        scratch_shapes=[pltpu.VMEM((GROUP_WIDTH, GROUP_WIDTH), F32),
                        pltpu.VMEM((GROUP_WIDTH, LANES), F32),
                        pltpu.VMEM((1, LANES), F32)],
        compiler_params=_cparams("arbitrary", "arbitrary"),
        name="mlstm_mixer",
    )(proj, proj, proj, proj, gates, bias_row, gain)


def _outproj_router_kernel(x_ref, ya_ref, ys_ref, ym_ref, yl_ref, wo_ref, mod_ref, wn_ref, wr_ref, br_ref,
                           xo_ref, h2_ref, eid_ref, gcol_ref):
    W = GROUP_WIDTH
    tm = x_ref.shape[0]
    mod = mod_ref[0]
    out = _dot(ya_ref[...], wo_ref[0:W, :])
    out = out + _dot(ys_ref[...], wo_ref[W:2 * W, :])
    out = out + _dot(ym_ref[...], wo_ref[2 * W:3 * W, :])
    out = out + _dot(yl_ref[...], wo_ref[3 * W:4 * W, :])
    x1 = x_ref[...] + mod[2:3, :] * out
    xo_ref[...] = x1
    h2 = _rms_norm(x1, wn_ref[...]) * (1.0 + mod[4:5, :]) + mod[3:4, :]
    h2_ref[...] = _pack_bf16_pairs(h2)

    logits_t = (_dot(h2.astype(BF16), wr_ref[...]) + br_ref[...]).T
    row8 = lax.broadcasted_iota(I32, (SUBLANES, tm), 0).astype(F32)
    gl = jnp.where(row8 < N_EXPERT_GROUPS, logits_t[ROUTE_G:ROUTE_G + SUBLANES, :], NEG_INF)
    g_max = jnp.max(gl, axis=0, keepdims=True)
    g_sel = jnp.min(jnp.where(gl == g_max, row8, SUBLANES), axis=0, keepdims=True)
    g_prob = 1.0 / jnp.sum(jnp.exp(gl - g_max), axis=0, keepdims=True)
    el = logits_t[ROUTE_E:ROUTE_E + EXPERTS_PER_GROUP, :]
    for g in range(1, N_EXPERT_GROUPS):
        lo = ROUTE_E + g * EXPERTS_PER_GROUP
        el = jnp.where(g_sel == g, logits_t[lo:lo + EXPERTS_PER_GROUP, :], el)
    m1 = jnp.max(el, axis=0, keepdims=True)
    i1 = jnp.min(jnp.where(el == m1, row8, SUBLANES), axis=0, keepdims=True)
    el2 = jnp.where(row8 == i1, NEG_INF, el)
    m2 = jnp.max(el2, axis=0, keepdims=True)
    i2 = jnp.min(jnp.where(el2 == m2, row8, SUBLANES), axis=0, keepdims=True)
    ratio = jnp.exp(m2 - m1)
    p1 = 1.0 / (1.0 + ratio)
    eid_ref[0:1, :] = (g_sel * EXPERTS_PER_GROUP + i1).astype(I32)
    eid_ref[1:2, :] = (g_sel * EXPERTS_PER_GROUP + i2).astype(I32)
    rows = lax.broadcasted_iota(I32, (LANES, tm), 0)
    gate_rows = jnp.where(rows == 0, g_prob * p1, jnp.where(rows == 1, g_prob * p1 * ratio, 0.0))
    gcol_ref[...] = gate_rows.T


def _outproj_router(x2, ys, w_out, mod, w_norm2, w_route, b_route, seq, tm=512):
    t, d = x2.shape
    spb = seq // tm
    ytile = pl.BlockSpec((tm, GROUP_WIDTH), lambda i: (i, 0))
    return pl.pallas_call(
        _outproj_router_kernel,
        grid=(t // tm,),
        in_specs=[pl.BlockSpec((tm, d), lambda i: (i, 0)), ytile, ytile, ytile, ytile,
                  pl.BlockSpec((d, d), lambda i: (0, 0)),
                  pl.BlockSpec((1, 6, d), lambda i: (i // spb, 0, 0)),
                  pl.BlockSpec((1, d), lambda i: (0, 0)),
                  pl.BlockSpec((d, LANES), lambda i: (0, 0)),
                  pl.BlockSpec((1, LANES), lambda i: (0, 0))],
        out_specs=[pl.BlockSpec((tm, d), lambda i: (i, 0)),
                   pl.BlockSpec((tm, d // 2), lambda i: (i, 0)),
                   pl.BlockSpec((TOP_K, tm), lambda i: (0, i)),
                   pl.BlockSpec((tm, LANES), lambda i: (i, 0))],
        out_shape=[jax.ShapeDtypeStruct((t, d), F32),
                   jax.ShapeDtypeStruct((t, d // 2), jnp.uint32),
                   jax.ShapeDtypeStruct((TOP_K, t), I32),
                   jax.ShapeDtypeStruct((t, LANES), F32)],
        compiler_params=_cparams("arbitrary"),
        name="outproj_router",
    )(x2, *ys, w_out, mod, w_norm2, w_route, b_route)


def _rank_kernel(eid_ref, rank_ref, count_ref, carry):
    @pl.when(pl.program_id(0) == 0)
    def _():
        carry[...] = jnp.zeros_like(carry)

    tr = eid_ref.shape[1]
    expert = lax.broadcasted_iota(I32, (N_EXPERTS, tr), 0)
    before = (lax.broadcasted_iota(I32, (tr, tr), 0) < lax.broadcasted_iota(I32, (tr, tr), 1)).astype(BF16)
    base = carry[...]
    for k in range(TOP_K):
        onehot = (expert == eid_ref[k:k + 1, :]).astype(F32)
        prefix = _dot(onehot.astype(BF16), before)
        rank_ref[k:k + 1, :] = jnp.sum(onehot * (base + prefix), axis=0, keepdims=True).astype(I32)
        base = base + jnp.sum(onehot, axis=1, keepdims=True)
    carry[...] = base
    count_ref[...] = jnp.broadcast_to(base, count_ref.shape)


def _expert_ranks(eids, tr=512):
    t = eids.shape[1]
    return pl.pallas_call(
        _rank_kernel,
        grid=(t // tr,),
        in_specs=[pl.BlockSpec((TOP_K, tr), lambda i: (0, i))],
        out_specs=[pl.BlockSpec((TOP_K, tr), lambda i: (0, i)),
                   pl.BlockSpec((N_EXPERTS, LANES), lambda i: (0, 0))],
        out_shape=[jax.ShapeDtypeStruct((TOP_K, t), I32),
                   jax.ShapeDtypeStruct((N_EXPERTS, LANES), F32)],
        scratch_shapes=[pltpu.VMEM((N_EXPERTS, 1), F32)],
        compiler_params=_cparams("arbitrary"),
        name="expert_ranks",
    )(eids)


def _dest_kernel(pstart_ref, eid_ref, rank_ref, dest_ref):
    e = eid_ref[...]
    dest = rank_ref[...]
    for j in range(N_EXPERTS):
        dest = dest + jnp.where(e == j, pstart_ref[j], 0)
    dest_ref[...] = dest


def _dest_rows(p_starts, eids, ranks, tm=2048):
    t = eids.shape[1]
    grid_spec = pltpu.PrefetchScalarGridSpec(
        num_scalar_prefetch=1,
        grid=(t // tm,),
        in_specs=[pl.BlockSpec((TOP_K, tm), lambda i, ps: (0, i)),
                  pl.BlockSpec((TOP_K, tm), lambda i, ps: (0, i))],
        out_specs=pl.BlockSpec((TOP_K, tm), lambda i, ps: (0, i)))
    return pl.pallas_call(
        _dest_kernel,
        grid_spec=grid_spec,
        out_shape=jax.ShapeDtypeStruct((TOP_K, t), I32),
        compiler_params=_cparams("arbitrary"),
        name="dest_rows",
    )(p_starts, eids, ranks)


ROW_TILE = IDX_CHUNK // TOP_K


def _tile_indices(idx_hbm, idx_smem, idx_sems, tile, slot):
    return pltpu.make_async_copy(idx_hbm.at[pl.ds(tile * IDX_CHUNK, IDX_CHUNK)],
                                 idx_smem.at[pl.ds(slot * IDX_CHUNK, IDX_CHUNK)], idx_sems.at[slot])


def _dispatch_kernel(pend_ref, padded_ref, nu_ref, idx_hbm, h_ref, xb_hbm, idx_smem, zero_blk, idx_sems, row_sem,
                     zero_sem):
    def zero_block(start):
        return pltpu.make_async_copy(zero_blk, xb_hbm.at[pl.ds(pl.multiple_of(start, MOE_BLOCK), MOE_BLOCK)],
                                     zero_sem)

    @pl.when(pl.program_id(0) == 0)
    def _():
        zero_blk[...] = jnp.zeros_like(zero_blk)
        for e in range(N_EXPERTS):
            @pl.when(padded_ref[e] > 0)
            def _(e=e):
                zero_block(pend_ref[e] - MOE_BLOCK).start()
        for e in range(N_EXPERTS):
            @pl.when(padded_ref[e] > 0)
            def _(e=e):
                zero_block(pend_ref[e] - MOE_BLOCK).wait()

        def zero_unused(b, carry):
            copy = zero_block(b * MOE_BLOCK)
            copy.start()
            copy.wait()
            return carry

        lax.fori_loop(nu_ref[0], xb_hbm.shape[0] // MOE_BLOCK, zero_unused, 0)

    i = pl.program_id(0)

    @pl.when(i == 0)
    def _():
        _tile_indices(idx_hbm, idx_smem, idx_sems, i, 0).start()

    def row_copy(r, dst_row):
        return pltpu.make_async_copy(h_ref.at[pl.ds(r, 1)], xb_hbm.at[pl.ds(dst_row, 1)], row_sem)

    def step_for_slot(slot):
        _tile_indices(idx_hbm, idx_smem, idx_sems, i, slot).wait()

        @pl.when(i + 1 < pl.num_programs(0))
        def _():
            _tile_indices(idx_hbm, idx_smem, idx_sems, i + 1, 1 - slot).start()

        def issue(r, carry):
            for k in range(TOP_K):
                row_copy(r, idx_smem[slot * IDX_CHUNK + k * ROW_TILE + r]).start(priority=k)
            return carry

        lax.fori_loop(0, ROW_TILE, issue, 0, unroll=8)

    for parity in range(2):
        pl.when(i % 2 == parity)(functools.partial(step_for_slot, parity))

    def drain(r, carry):
        for k in range(TOP_K):
            row_copy(0, 0).wait()
        return carry

    lax.fori_loop(0, ROW_TILE, drain, 0, unroll=8)


def _dispatch(p_ends, padded, n_used, idx_tiles, h2, dst_rows):
    t, d = h2.shape
    any_spec = pl.BlockSpec(memory_space=pl.ANY)
    grid_spec = pltpu.PrefetchScalarGridSpec(
        num_scalar_prefetch=3,
        grid=(t // ROW_TILE,),
        in_specs=[any_spec, pl.BlockSpec((ROW_TILE, d), lambda i, pe, pd, nu: (i, 0))],
        out_specs=any_spec,
        scratch_shapes=[pltpu.SMEM((2 * IDX_CHUNK,), I32), pltpu.VMEM((MOE_BLOCK, d), h2.dtype),
                        pltpu.SemaphoreType.DMA((2,)), pltpu.SemaphoreType.DMA, pltpu.SemaphoreType.DMA])
    return pl.pallas_call(
        _dispatch_kernel,
        grid_spec=grid_spec,
        out_shape=jax.ShapeDtypeStruct((dst_rows, d), h2.dtype),
        compiler_params=_cparams("arbitrary"),
        name="moe_dispatch",
    )(p_ends, padded, n_used, idx_tiles, h2)


def _expert_kernel(be_ref, nu_ref, x_ref, wg_ref, wu_ref, wd_ref, y_ref, wg_b, wu_b, wd_b):
    b = pl.program_id(0)

    @pl.when(b < nu_ref[0])
    def _():
        @pl.when(jnp.logical_or(b == 0, be_ref[b] != be_ref[jnp.maximum(b - 1, 0)]))
        def _():
            wg_b[...] = wg_ref[0].astype(BF16)
            wu_b[...] = wu_ref[0].astype(BF16)
            wd_b[...] = wd_ref[0].astype(BF16)

        x = jnp.concatenate(_unpack_bf16_pairs(x_ref[...]), axis=1).astype(BF16)
        a = _silu(_dot(x, wg_b[...])) * _dot(x, wu_b[...])
        y_ref[...] = _pack_bf16_pairs(_dot(a.astype(BF16), wd_b[...]))

    @pl.when(b >= nu_ref[0])
    def _():
        y_ref[...] = jnp.zeros_like(y_ref)


def _expert_mlp(block_e, n_used, xb, w_gate, w_up, w_down):
    p, words = xb.shape
    d, de = w_gate.shape[1:]
    blk = lambda b, be, nu: (jnp.minimum(b, nu[0] - 1), 0)
    grid_spec = pltpu.PrefetchScalarGridSpec(
        num_scalar_prefetch=2,
        grid=(p // MOE_BLOCK,),
        in_specs=[pl.BlockSpec((MOE_BLOCK, words), blk),
                  pl.BlockSpec((1, d, de), lambda b, be, nu: (be[b], 0, 0)),
                  pl.BlockSpec((1, d, de), lambda b, be, nu: (be[b], 0, 0)),
                  pl.BlockSpec((1, de, d), lambda b, be, nu: (be[b], 0, 0))],
        out_specs=pl.BlockSpec((MOE_BLOCK, words), lambda b, be, nu: (b, 0)),
        scratch_shapes=[pltpu.VMEM((d, de), BF16), pltpu.VMEM((d, de), BF16), pltpu.VMEM((de, d), BF16)])
    return pl.pallas_call(
        _expert_kernel,
        grid_spec=grid_spec,
        out_shape=jax.ShapeDtypeStruct((p, words), jnp.uint32),
        compiler_params=_cparams("arbitrary"),
        name="expert_mlp",
    )(block_e, n_used, xb, w_gate, w_up, w_down)


def _combine_kernel(idx_hbm, yb_hbm, x_ref, gcol_ref, mod_ref, wnf_ref, o_ref, ybuf, idx_smem, idx_sems, row_sems,
                    *, final):
    i = pl.program_id(0)
    n = pl.num_programs(0)

    indices = functools.partial(_tile_indices, idx_hbm, idx_smem, idx_sems)

    def row_copy(src_row, slot, k, r):
        return pltpu.make_async_copy(yb_hbm.at[pl.ds(src_row, 1)], ybuf.at[slot, k, pl.ds(r, 1)],
                                     row_sems.at[slot])

    def issue_tile(slot):
        def issue(r, carry):
            for k in range(TOP_K):
                row_copy(idx_smem[slot * IDX_CHUNK + k * ROW_TILE + r], slot, k, r).start(priority=k)
            return carry

        lax.fori_loop(0, ROW_TILE, issue, 0, unroll=8)

    @pl.when(i == 0)
    def _():
        first = indices(i, 0)
        first.start()
        first.wait()
        issue_tile(0)

        @pl.when(n > 1)
        def _():
            indices(i + 1, 1).start()

    def step_for_slot(slot):
        other = 1 - slot

        @pl.when(i + 1 < n)
        def _():
            indices(i + 1, other).wait()

        @pl.when(i + 2 < n)
        def _():
            indices(i + 2, slot).start()

        @pl.when(i + 1 < n)
        def _():
            issue_tile(other)

        def drain(r, carry):
            for k in range(TOP_K):
                row_copy(0, slot, k, 0).wait()
            return carry

        lax.fori_loop(0, ROW_TILE, drain, 0, unroll=8)

        gc = gcol_ref[...]
        y0 = _unpack_bf16_pairs(ybuf[slot, 0])
        y1 = _unpack_bf16_pairs(ybuf[slot, 1])
        moe = jnp.concatenate([gc[:, 0:1] * y0[0] + gc[:, 1:2] * y1[0], gc[:, 0:1] * y0[1] + gc[:, 1:2] * y1[1]],
                              axis=1)
        x2 = x_ref[...] + mod_ref[0][5:6, :] * moe
        o_ref[...] = _rms_norm(x2, wnf_ref[...]) if final else x2

    for parity in range(2):
        pl.when(i % 2 == parity)(functools.partial(step_for_slot, parity))


def _combine(idx_tiles, yb, x2, gcol, mod, w_norm_final, seq, final):
    t, d = x2.shape
    tm = ROW_TILE
    spb = seq // tm
    any_spec = pl.BlockSpec(memory_space=pl.ANY)
    return pl.pallas_call(
        functools.partial(_combine_kernel, final=final),
        grid=(t // tm,),
        in_specs=[any_spec, any_spec,
                  pl.BlockSpec((tm, d), lambda i: (i, 0)),
                  pl.BlockSpec((tm, LANES), lambda i: (i, 0)),
                  pl.BlockSpec((1, 6, d), lambda i: (i // spb, 0, 0)),
                  pl.BlockSpec((1, d), lambda i: (0, 0))],
        out_specs=pl.BlockSpec((tm, d), lambda i: (i, 0)),
        out_shape=jax.ShapeDtypeStruct((t, d), F32),
        scratch_shapes=[pltpu.VMEM((2, TOP_K, tm, d // 2), jnp.uint32), pltpu.SMEM((2 * IDX_CHUNK,), I32),
                        pltpu.SemaphoreType.DMA((2,)), pltpu.SemaphoreType.DMA((2,))],
        compiler_params=_cparams("arbitrary"),
        name="moe_combine",
    )(idx_tiles, yb, x2, gcol, mod, w_norm_final)


def _lane_row(pieces, width=LANES):
    row = jnp.zeros((width,), F32)
    for off, vec in pieces.items():
        row = row.at[off:off + vec.shape[0]].set(vec.astype(F32))
    return row.reshape(1, width)


def _split_w_in(w_in):
    gw, nh = GROUP_WIDTH, N_HEADS
    widths = [gw, gw, gw, nh, gw, gw, gw, 3 * gw, nh, gw, gw, gw, gw, nh, nh]
    cuts, acc = [], 0
    for w in widths[:-1]:
        acc += w
        cuts.append(acc)
    (aq, ak, av, af, su, sv, mz, mxbc, mdt, lq, lk, lv, lo, li, lf) = jnp.split(w_in, cuts, axis=1)
    scale = HEAD_DIM ** -0.5
    wmain = jnp.concatenate([ak, su, sv, mz, mxbc, lq, lk * scale, lv, lo], axis=1).astype(BF16)
    wqt = (aq * scale).T.astype(BF16)
    wvt = av.T.astype(BF16)
    wg = jnp.zeros((w_in.shape[0], LANES), F32)
    for off, w in ((GATE_AF, af), (GATE_DT, mdt), (GATE_LI, li), (GATE_LF, lf)):
        wg = wg.at[:, off:off + nh].set(w)
    return wmain, wqt, wvt, wg.astype(BF16)


def _moe_layer(x1, h2, eids, gcol, mod_l, w_gate, w_up, w_down, layer, w_norm_final, seq, final):
    t, d = x1.shape
    ranks, counts = _expert_ranks(eids)
    counts = counts[:, 0].astype(I32)
    padded = ((counts + MOE_BLOCK - 1) // MOE_BLOCK) * MOE_BLOCK
    p_ends = jnp.cumsum(padded)
    p_starts = (p_ends - padded).astype(I32)
    n_blocks = (t * TOP_K) // MOE_BLOCK + N_EXPERTS
    blocks = jnp.arange(n_blocks, dtype=I32)
    block_e = jnp.sum((p_ends[None, :] <= (blocks * MOE_BLOCK)[:, None]).astype(I32), axis=1)
    block_e = jnp.minimum(block_e, N_EXPERTS - 1)
    n_used = (p_ends[-1:] // MOE_BLOCK).astype(I32)
    block_e = jnp.where(blocks < n_used, block_e, block_e[n_used[0] - 1])
    dest = _dest_rows(p_starts, eids, ranks)
    idx_tiles = dest.reshape(TOP_K, t // ROW_TILE, ROW_TILE).transpose(1, 0, 2).reshape(-1)
    xb = _dispatch(p_ends.astype(I32), padded.astype(I32), n_used, idx_tiles, h2, n_blocks * MOE_BLOCK)
    yb = _expert_mlp(block_e + layer * N_EXPERTS, n_used, xb, w_gate, w_up, w_down)
    return _combine(idx_tiles, yb, x1, gcol, mod_l, w_norm_final, seq, final)


def kernel(x, c, w_in, w_out, w_mix_norm, attn_f_bias, sg_w, sg_b, ssm_conv_w, ssm_conv_b, ssm_dt_bias,
           ssm_a_log, ssm_d, mlstm_i_bias, mlstm_f_bias, w_ada, b_ada, w_norm1, w_norm2, w_router_group,
           b_router_group, w_router_expert, b_router_expert, w_expert_gate, w_expert_up, w_expert_down,
           w_norm_final):
    bsz, seq, d = x.shape
    depth = w_in.shape[0]
    gw = GROUP_WIDTH
    mod = _ada_modulation(c, w_ada, b_ada).reshape(depth, bsz, 6, d)
    x2 = x.reshape(bsz * seq, d)
    wnf = w_norm_final.reshape(1, d)
    w_eg = w_expert_gate.reshape((depth * N_EXPERTS,) + w_expert_gate.shape[2:])
    w_eu = w_expert_up.reshape((depth * N_EXPERTS,) + w_expert_up.shape[2:])
    w_ed = w_expert_down.reshape((depth * N_EXPERTS,) + w_expert_down.shape[2:])
    for l in range(depth):
        wmain, wqt, wvt, wg = _split_w_in(w_in[l])
        gains = w_mix_norm[l].reshape(N_HEADS, 1, gw)
        proj, qt, vt, gates, f2 = _inproj(x2, mod[l], w_norm1[l].reshape(1, d), wmain, wqt, wvt, wg,
                                          _lane_row({GATE_AF: attn_f_bias[l]}), bsz, seq)
        y_attn = _attention(proj, qt, vt, f2, gains[0], bsz, seq)
        sg_bias_full = jnp.repeat(sg_b[l].T, HEAD_DIM, axis=1)
        y_sg = _spatial_gating(proj, gains[1], sg_w[l], sg_bias_full)
        y_ssm = _ssd_mixer(proj, gates, ssm_conv_w[l], ssm_conv_b[l].reshape(1, -1),
                           _lane_row({GATE_DT: ssm_dt_bias[l]}), _lane_row({GATE_DT: ssm_a_log[l]}),
                           jnp.repeat(ssm_d[l], HEAD_DIM).reshape(1, gw), gains[2], bsz, seq)
        y_ml = _mlstm_mixer(proj, gates, _lane_row({GATE_LI: mlstm_i_bias[l], GATE_LF: mlstm_f_bias[l]}),
                            gains[3], bsz, seq)
        w_route = jnp.zeros((d, LANES), F32)
        w_route = w_route.at[:, ROUTE_G:ROUTE_G + N_EXPERT_GROUPS].set(w_router_group[l])
        w_route = w_route.at[:, ROUTE_E:ROUTE_E + N_EXPERTS].set(w_router_expert[l]).astype(BF16)
        b_route = _lane_row({ROUTE_G: b_router_group[l], ROUTE_E: b_router_expert[l]})
        x1, h2, eids, gcol = _outproj_router(x2, (y_attn, y_sg, y_ssm, y_ml), w_out[l].astype(BF16), mod[l],
                                             w_norm2[l].reshape(1, d), w_route, b_route, seq)
        x2 = _moe_layer(x1, h2, eids, gcol, mod[l], w_eg, w_eu, w_ed, l, wnf, seq, final=(l == depth - 1))
    return x2.reshape(bsz, seq, d)
```

```python
import functools

import jax
import jax.numpy as jnp
from jax import lax
from jax.experimental import pallas as pl
from jax.experimental.pallas import tpu as pltpu

F32 = jnp.float32
BF16 = jnp.bfloat16
I32 = jnp.int32

LANES = 128
SUBLANES = 8
HEAD_DIM = 64
N_HEADS = 4
GROUP_WIDTH = 256
CHUNK = 128
MIX_TILE = 8 * CHUNK
SSM_GROUPS = 2
CONV_WIDTH = 4
N_EXPERT_GROUPS = 4
EXPERTS_PER_GROUP = 8
N_EXPERTS = N_EXPERT_GROUPS * EXPERTS_PER_GROUP
TOP_K = 2
MOE_BLOCK = 512
NORM_EPS = 1e-6
MLSTM_M_INIT = -1e30
NEG_INF = float("-inf")
VMEM_LIMIT_BYTES = 48 * 1024 * 1024
IDX_CHUNK = 1024

(PB_K, PB_SU, PB_SV, PB_MZ, PB_X, PB_B, PB_C, PB_LQ, PB_LK, PB_LV, PB_LO) = range(11)
PROJ_COLS = 11 * GROUP_WIDTH
LOG2_E = 1.4426950408889634
GATE_AF, GATE_DT, GATE_LI, GATE_LF = 0, 4, 8, 12
ROUTE_G, ROUTE_E = 0, 8

NT_DIMS = (((1,), (1,)), ((), ()))


def _cparams(*sem):
    return pltpu.CompilerParams(dimension_semantics=sem, vmem_limit_bytes=VMEM_LIMIT_BYTES)


def _dot(a, b):
    return jnp.dot(a, b, preferred_element_type=F32)


def _dot_nt(a, b):
    return lax.dot_general(a, b, NT_DIMS, preferred_element_type=F32)


def _dot_exact(a, b):
    return jnp.dot(a, b, preferred_element_type=F32, precision=lax.Precision.HIGHEST)


def _head_of_lane(shape, axis=1):
    return lax.broadcasted_iota(I32, shape, axis) // HEAD_DIM


def _sigmoid(x):
    return 1.0 / (1.0 + jnp.exp(-x))


def _silu(x):
    return x * _sigmoid(x)


def _log_sigmoid(x):
    return jnp.minimum(x, 0.0) - jnp.log1p(jnp.exp(-jnp.abs(x)))


def _softplus(x):
    return jnp.maximum(x, 0.0) + jnp.log1p(jnp.exp(-jnp.abs(x)))


def _gelu_tanh(x):
    return 0.5 * x * (1.0 + jnp.tanh(0.7978845608028654 * (x + 0.044715 * (x * x * x))))


def _expand_heads(cols, width=GROUP_WIDTH):
    rows = cols[0].shape[0]
    head = _head_of_lane((rows, width))
    out = jnp.broadcast_to(cols[N_HEADS - 1], (rows, width))
    for h in range(N_HEADS - 2, -1, -1):
        out = jnp.where(head == h, jnp.broadcast_to(cols[h], (rows, width)), out)
    return out


def _mask_head(x, h):
    return jnp.where(_head_of_lane(x.shape) == h, x, jnp.zeros_like(x))


def _head_rms_norm(y, gain):
    head = _head_of_lane(y.shape)
    sq = y * y
    cols = [jnp.sum(jnp.where(head == h, sq, 0.0), axis=1, keepdims=True) * (1.0 / HEAD_DIM)
            for h in range(N_HEADS)]
    return y * lax.rsqrt(_expand_heads(cols) + NORM_EPS) * gain


def _rms_norm(x, gain):
    ms = jnp.mean(x * x, axis=1, keepdims=True)
    return x * lax.rsqrt(ms + NORM_EPS) * gain


def _pack_bf16_pairs(x):
    half = x.shape[1] // 2
    bits = lax.bitcast_convert_type(x.astype(BF16).astype(F32), jnp.uint32)
    return bits[:, :half] | lax.shift_right_logical(bits[:, half:], jnp.uint32(16))


def _unpack_bf16_pairs(words):
    first = lax.bitcast_convert_type(words & jnp.uint32(0xFFFF0000), F32)
    second = lax.bitcast_convert_type(lax.shift_left(words, jnp.uint32(16)), F32)
    return first, second


def _cumsum_rows(x):
    tri = jnp.where(_tril(x.shape[0]), 1.0, 0.0).astype(BF16)
    hi = x.astype(BF16)
    rest = x - hi.astype(F32)
    mid = rest.astype(BF16)
    lo = (rest - mid.astype(F32)).astype(BF16)
    return _dot(tri, hi) + _dot(tri, mid) + _dot(tri, lo)


def _tril(n, strict=False):
    r = lax.broadcasted_iota(I32, (n, n), 0)
    c = lax.broadcasted_iota(I32, (n, n), 1)
    return (r > c) if strict else (r >= c)


def _ada_kernel(c_ref, w_ref, b_ref, o_ref):
    o_ref[0] = _dot_exact(_silu(c_ref[...]), w_ref[0]) + b_ref[0]


def _ada_modulation(c, w_ada, b_ada):
    depth, d, d6 = w_ada.shape
    bsz = c.shape[0]
    return pl.pallas_call(
        _ada_kernel,
        grid=(depth, d6 // d),
        in_specs=[pl.BlockSpec((bsz, d), lambda l, j: (0, 0)),
                  pl.BlockSpec((1, d, d), lambda l, j: (l, 0, j)),
                  pl.BlockSpec((1, 1, d), lambda l, j: (l, 0, j))],
        out_specs=pl.BlockSpec((1, bsz, d), lambda l, j: (l, 0, j)),
        out_shape=jax.ShapeDtypeStruct((depth, bsz, d6), F32),
        compiler_params=_cparams("arbitrary", "arbitrary"),
        name="ada_modulation",
    )(c, w_ada, b_ada.reshape(depth, 1, d6))


def _forget_bias_slab(gates, carry):
    tb = gates.shape[0]
    cum = _cumsum_rows(_log_sigmoid(gates)) + carry[...]
    carry[...] = cum[tb - 1:tb, :]
    f2 = cum * LOG2_E
    hi = f2.astype(BF16)
    rest = f2 - hi.astype(F32)
    mid = rest.astype(BF16)
    lo = (rest - mid.astype(F32)).astype(BF16)
    r = lax.broadcasted_iota(I32, (LANES, GROUP_WIDTH), 0) - GATE_AF
    c = lax.broadcasted_iota(I32, (LANES, GROUP_WIDTH), 1)
    slab = jnp.zeros((tb, GROUP_WIDTH), F32)
    for j, piece in enumerate((hi, mid, lo)):
        place = (r >= 0) & (r < N_HEADS) & (c == ((r + 1) % N_HEADS) * HEAD_DIM + j)
        slab = slab + _dot(piece, jnp.where(place, -1.0, 0.0).astype(BF16))
    return slab.astype(BF16)


def _inproj_kernel(x_ref, mod_ref, wn_ref, wmain_ref, wqt_ref, wvt_ref, wg_ref, fbias_ref,
                   proj_ref, qt_ref, vt_ref, gates_ref, slab_ref, h_scr, f_carry, *, tiles_per_seq):
    @pl.when(pl.program_id(0) % tiles_per_seq == 0)
    def _():
        f_carry[...] = jnp.zeros_like(f_carry)

    mod = mod_ref[0]
    h = _rms_norm(x_ref[...], wn_ref[...]) * (1.0 + mod[1:2, :]) + mod[0:1, :]
    h_scr[...] = h.astype(BF16)
    for c0 in range(0, PROJ_COLS, GROUP_WIDTH):
        proj_ref[:, c0:c0 + GROUP_WIDTH] = _dot(h_scr[...], wmain_ref[:, c0:c0 + GROUP_WIDTH]).astype(BF16)
    qt_ref[0] = (_dot_nt(wqt_ref[...], h_scr[...]) * LOG2_E).astype(BF16)
    vt_ref[0] = _dot_nt(wvt_ref[...], h_scr[...]).astype(BF16)
    gates = _dot(h_scr[...], wg_ref[...])
    gates_ref[...] = gates
    slab_ref[...] = _forget_bias_slab(gates + fbias_ref[...], f_carry)


def _inproj(x2, mod, w_norm, wmain, wqt, wvt, wg, f_bias_row, bsz, seq, tm=512):
    t, d = x2.shape
    spb = seq // tm
    tspec = pl.BlockSpec((1, GROUP_WIDTH, tm), lambda i: (i // spb, 0, i % spb))
    return pl.pallas_call(
        functools.partial(_inproj_kernel, tiles_per_seq=spb),
        grid=(t // tm,),
        in_specs=[pl.BlockSpec((tm, d), lambda i: (i, 0)),
                  pl.BlockSpec((1, 6, d), lambda i: (i // spb, 0, 0)),
                  pl.BlockSpec((1, d), lambda i: (0, 0)),
                  pl.BlockSpec((d, PROJ_COLS), lambda i: (0, 0)),
                  pl.BlockSpec((GROUP_WIDTH, d), lambda i: (0, 0)),
                  pl.BlockSpec((GROUP_WIDTH, d), lambda i: (0, 0)),
                  pl.BlockSpec((d, LANES), lambda i: (0, 0)),
                  pl.BlockSpec((1, LANES), lambda i: (0, 0))],
        out_specs=[pl.BlockSpec((tm, PROJ_COLS), lambda i: (i, 0)), tspec, tspec,
                   pl.BlockSpec((tm, LANES), lambda i: (i, 0)),
                   pl.BlockSpec((tm, GROUP_WIDTH), lambda i: (i, 0))],
        out_shape=[jax.ShapeDtypeStruct((t, PROJ_COLS), BF16),
                   jax.ShapeDtypeStruct((bsz, GROUP_WIDTH, seq), BF16),
                   jax.ShapeDtypeStruct((bsz, GROUP_WIDTH, seq), BF16),
                   jax.ShapeDtypeStruct((t, LANES), F32),
                   jax.ShapeDtypeStruct((t, GROUP_WIDTH), BF16)],
        scratch_shapes=[pltpu.VMEM((tm, d), BF16), pltpu.VMEM((1, LANES), F32)],
        compiler_params=_cparams("arbitrary"),
        name="norm_inproj",
    )(x2, mod, w_norm, wmain, wqt, wvt, wg, f_bias_row)


ONES_ROWS = 16
QUERY_SPLITS = 2


def _attn_kernel(qi_ref, kj_ref, k_ref, qt_ref, vt_ref, f_ref, gain_ref, o_ref, qtm, m_s, l_s, acc):
    p = pl.program_id(1)
    qi, kj = qi_ref[p], kj_ref[p]
    tk = k_ref.shape[0]
    tq = qt_ref.shape[2]

    @pl.when(kj == 0)
    def _():
        qt = qt_ref[0]
        row = lax.broadcasted_iota(I32, qt.shape, 0)
        for h in range(N_HEADS):
            slot = ((h + 1) % N_HEADS) * HEAD_DIM
            ones_rows = jnp.where((row >= slot) & (row < slot + 3), 1.0, 0.0).astype(BF16)
            qtm[h] = jnp.where(row // HEAD_DIM == h, qt, ones_rows)
        m_s[...] = jnp.full_like(m_s, NEG_INF)
        l_s[...] = jnp.zeros_like(l_s)
        acc[...] = jnp.zeros_like(acc)

    def step(diagonal):
        k = k_ref[...]
        bias = f_ref[...]
        head = _head_of_lane(k.shape)
        ones = jnp.ones((ONES_ROWS, tk), BF16)
        if diagonal:
            visible = lax.broadcasted_iota(I32, (tk, tq), 0) <= lax.broadcasted_iota(I32, (tk, tq), 1)
        scores, m_news, alphas, probs = {}, {}, {}, {}
        k_aug = [jnp.where(head == h, k, bias) for h in range(N_HEADS)]
        splits = 1 if diagonal else QUERY_SPLITS
        half = tq // splits
        units = [(h, c) for h in range(N_HEADS) for c in range(splits)]

        def score(u):
            h, c = units[u]
            cols = slice(c * half, (c + 1) * half)
            s = _dot(k_aug[h], qtm[h, :, cols])
            if diagonal:
                s = jnp.where(visible[:, cols], s, NEG_INF)
            scores[u] = s
            m_prev = m_s[h, :, cols]
            m_news[u] = jnp.maximum(m_prev, jnp.max(s, axis=0, keepdims=True))
            alphas[u] = jnp.exp2(m_prev - m_news[u])
            m_s[h, :, cols] = m_news[u]

        def prob(u):
            probs[u] = jnp.exp2(scores[u] - m_news[u]).astype(BF16)

        def weighted_sum(u):
            h, c = units[u]
            cols = slice(c * half, (c + 1) * half)
            vt_ext = jnp.concatenate([vt_ref[0, h * HEAD_DIM:(h + 1) * HEAD_DIM, :], ones], axis=0)
            pv = _dot(vt_ext, probs[u])
            acc[h, :, cols] = alphas[u] * acc[h, :, cols] + pv[0:HEAD_DIM, :]
            l_s[h, :, cols] = alphas[u] * l_s[h, :, cols] + pv[HEAD_DIM:HEAD_DIM + 1, :]

        for stage in range(len(units) + 2):
            if stage < len(units):
                score(stage)
            if 0 <= stage - 1 < len(units):
                prob(stage - 1)
            if 0 <= stage - 2 < len(units):
                weighted_sum(stage - 2)

    @pl.when(kj < qi)
    def _():
        step(False)

    @pl.when(kj == qi)
    def _():
        step(True)
        yt = jnp.concatenate([acc[h] / l_s[h] for h in range(N_HEADS)], axis=0)
        o_ref[...] = _head_rms_norm(yt.T, gain_ref[...]).astype(BF16)


def _attention(proj, qt, vt, f2, gain, bsz, seq, tq=1024):
    nq = seq // tq
    pairs = [(i, j) for i in range(nq) for j in range(i + 1)]
    qi = jnp.asarray([a for a, _ in pairs], I32)
    kj = jnp.asarray([b for _, b in pairs], I32)
    t = bsz * seq
    grid_spec = pltpu.PrefetchScalarGridSpec(
        num_scalar_prefetch=2,
        grid=(bsz, len(pairs)),
        in_specs=[pl.BlockSpec((tq, GROUP_WIDTH), lambda b, p, qi, kj: (b * nq + kj[p], PB_K)),
                  pl.BlockSpec((1, GROUP_WIDTH, tq), lambda b, p, qi, kj: (b, 0, qi[p])),
                  pl.BlockSpec((1, GROUP_WIDTH, tq), lambda b, p, qi, kj: (b, 0, kj[p])),
                  pl.BlockSpec((tq, GROUP_WIDTH), lambda b, p, qi, kj: (b * nq + kj[p], 0)),
                  pl.BlockSpec((1, GROUP_WIDTH), lambda b, p, qi, kj: (0, 0))],
        out_specs=pl.BlockSpec((tq, GROUP_WIDTH), lambda b, p, qi, kj: (b * nq + qi[p], 0)),
        scratch_shapes=[pltpu.VMEM((N_HEADS, GROUP_WIDTH, tq), BF16),
                        pltpu.VMEM((N_HEADS, 1, tq), F32),
                        pltpu.VMEM((N_HEADS, 1, tq), F32),
                        pltpu.VMEM((N_HEADS, HEAD_DIM, tq), F32)])
    return pl.pallas_call(
        _attn_kernel,
        grid_spec=grid_spec,
        out_shape=jax.ShapeDtypeStruct((t, GROUP_WIDTH), BF16),
        compiler_params=_cparams("arbitrary", "arbitrary"),
        name="fox_attention",
    )(qi, kj, proj, qt, vt, f2, gain)


def _sg_kernel(u_ref, v_ref, gain_ref, w_ref, b_ref, o_ref):
    tm = u_ref.shape[0]
    u = _gelu_tanh(u_ref[...].astype(F32))
    v = _head_rms_norm(_gelu_tanh(v_ref[...].astype(F32)), gain_ref[...])
    causal = _tril(CHUNK)
    ws = [jnp.where(causal, w_ref[h], 0.0).astype(BF16) for h in range(N_HEADS)]
    for c0 in range(0, tm, CHUNK):
        vc = v[c0:c0 + CHUNK, :].astype(BF16)
        mixed = b_ref[...]
        for h in range(N_HEADS):
            mixed = mixed + _dot(ws[h], _mask_head(vc, h))
        o_ref[c0:c0 + CHUNK, :] = (u[c0:c0 + CHUNK, :] * mixed).astype(BF16)


def _spatial_gating(proj, gain, sg_w, sg_bias_full, tm=512):
    t = proj.shape[0]
    return pl.pallas_call(
        _sg_kernel,
        grid=(t // tm,),
        in_specs=[pl.BlockSpec((tm, GROUP_WIDTH), lambda i: (i, PB_SU)),
                  pl.BlockSpec((tm, GROUP_WIDTH), lambda i: (i, PB_SV)),
                  pl.BlockSpec((1, GROUP_WIDTH), lambda i: (0, 0)),
                  pl.BlockSpec((N_HEADS, CHUNK, CHUNK), lambda i: (0, 0, 0)),
                  pl.BlockSpec((CHUNK, GROUP_WIDTH), lambda i: (0, 0))],
        out_specs=pl.BlockSpec((tm, GROUP_WIDTH), lambda i: (i, 0)),
        out_shape=jax.ShapeDtypeStruct((t, GROUP_WIDTH), BF16),
        compiler_params=_cparams("arbitrary"),
        name="spatial_gating",
    )(proj, proj, gain, sg_w, sg_bias_full)


def _ssd_kernel(z_ref, x_ref, b_ref, c_ref, g_ref, cw_ref, cb_ref, dtb_ref, alog_ref, dskip_ref, gain_ref,
                o_ref, conv_scr, xbc_scr, state):
    L = CHUNK
    W = GROUP_WIDTH
    tile = x_ref.shape[0]

    @pl.when(pl.program_id(1) == 0)
    def _():
        conv_scr[0:SUBLANES, :] = jnp.zeros((SUBLANES, 3 * W), F32)
        state[...] = jnp.zeros_like(state)

    conv_scr[SUBLANES:, 0:W] = x_ref[...].astype(F32)
    conv_scr[SUBLANES:, W:2 * W] = b_ref[...].astype(F32)
    conv_scr[SUBLANES:, 2 * W:] = c_ref[...].astype(F32)
    cw = cw_ref[...]
    conv = cb_ref[...] + cw[CONV_WIDTH - 1:CONV_WIDTH, :] * conv_scr[SUBLANES:, :]
    for s in range(1, CONV_WIDTH):
        conv = conv + cw[CONV_WIDTH - 1 - s:CONV_WIDTH - s, :] * conv_scr[SUBLANES - s:SUBLANES - s + tile, :]
    conv_scr[0:SUBLANES, :] = conv_scr[tile:tile + SUBLANES, :]
    xbc_scr[...] = _silu(conv)

    neg_a = -jnp.exp(alog_ref[...])
    for c0 in range(0, tile, L):
        rows = slice(c0, c0 + L)
        dt = _softplus(g_ref[rows, :] + dtb_ref[...])
        y = _ssd_chunk(xbc_scr[rows, 0:W], xbc_scr[rows, W:2 * W], xbc_scr[rows, 2 * W:], dt, dt * neg_a,
                       dskip_ref[...], state)
        y = y * _silu(z_ref[rows, :].astype(F32))
        o_ref[rows, :] = _head_rms_norm(y, gain_ref[...]).astype(BF16)


def _ssd_chunk(xs, bm, cm, dt, da, dskip, state):
    L = CHUNK
    a_cum = _cumsum_rows(da)
    a_row = a_cum.T
    dt_cols = [dt[:, GATE_DT + h:GATE_DT + h + 1] for h in range(N_HEADS)]
    a_cols = [a_cum[:, GATE_DT + h:GATE_DT + h + 1] for h in range(N_HEADS)]
    a_end = [a_cum[L - 1:L, GATE_DT + h:GATE_DT + h + 1] for h in range(N_HEADS)]
    xdt = xs * _expand_heads(dt_cols)
    xdt_b = xdt.astype(BF16)
    xw = (xdt * _expand_heads([jnp.exp(a_end[h] - a_cols[h]) for h in range(N_HEADS)])).astype(BF16)
    exp_a = _expand_heads([jnp.exp(a_cols[h]) for h in range(N_HEADS)])
    causal = _tril(L)
    half = lax.broadcasted_iota(I32, (1, LANES), 1) < HEAD_DIM

    heads_per_group = N_HEADS // SSM_GROUPS
    bgs = [bm[:, g * LANES:(g + 1) * LANES] for g in range(SSM_GROUPS)]
    cgs = [cm[:, g * LANES:(g + 1) * LANES].astype(BF16) for g in range(SSM_GROUPS)]
    scores = [_dot_nt(cgs[g], bgs[g].astype(BF16)) for g in range(SSM_GROUPS)]
    y_off = [_dot(cgs[g], state[g].astype(BF16)) for g in range(SSM_GROUPS)]
    for g in range(SSM_GROUPS):
        chunk_decay = jnp.where(half, jnp.exp(a_end[2 * g]), jnp.exp(a_end[2 * g + 1]))
        state[g] = chunk_decay * state[g] + _dot(bgs[g].T.astype(BF16), xw[:, g * LANES:(g + 1) * LANES])
    decays = [jnp.exp(jnp.where(causal, a_cols[h] - a_row[GATE_DT + h:GATE_DT + h + 1, :], NEG_INF))
              for h in range(N_HEADS)]
    y = dskip * xs
    for h in range(N_HEADS):
        y = y + _dot((scores[h // heads_per_group] * decays[h]).astype(BF16), _mask_head(xdt_b, h))
    return y + jnp.concatenate(y_off, axis=1) * exp_a


def _ssd_mixer(proj, gates, conv_w, conv_b, dtb_row, alog_row, dskip_row, gain, bsz, seq, tile=MIX_TILE):
    t = proj.shape[0]
    nc = seq // tile
    row = lambda blk: pl.BlockSpec((tile, GROUP_WIDTH), lambda b, j, blk=blk: (b * nc + j, blk))
    const = lambda shape: pl.BlockSpec(shape, lambda b, j: (0,) * len(shape))
    return pl.pallas_call(
        _ssd_kernel,
        grid=(bsz, nc),
        in_specs=[row(PB_MZ), row(PB_X), row(PB_B), row(PB_C),
                  pl.BlockSpec((tile, LANES), lambda b, j: (b * nc + j, 0)),
                  const((CONV_WIDTH, 3 * GROUP_WIDTH)), const((1, 3 * GROUP_WIDTH)),
                  const((1, LANES)), const((1, LANES)), const((1, GROUP_WIDTH)), const((1, GROUP_WIDTH))],
        out_specs=pl.BlockSpec((tile, GROUP_WIDTH), lambda b, j: (b * nc + j, 0)),
        out_shape=jax.ShapeDtypeStruct((t, GROUP_WIDTH), BF16),
        scratch_shapes=[pltpu.VMEM((tile + SUBLANES, 3 * GROUP_WIDTH), F32),
                        pltpu.VMEM((tile, 3 * GROUP_WIDTH), F32),
                        pltpu.VMEM((SSM_GROUPS, LANES, LANES), F32)],
        compiler_params=_cparams("arbitrary", "arbitrary"),
        name="ssd_mixer",
    )(proj, proj, proj, proj, gates, conv_w, conv_b, dtb_row, alog_row, dskip_row, gain)


def _mlstm_kernel(q_ref, k_ref, v_ref, o_gate_ref, g_ref, bias_ref, gain_ref, o_ref, ct, nb, m_row):
    L = CHUNK
    W = GROUP_WIDTH

    @pl.when(pl.program_id(1) == 0)
    def _():
        ct[...] = jnp.zeros_like(ct)
        nb[...] = jnp.zeros_like(nb)
        m_row[...] = jnp.full_like(m_row, MLSTM_M_INIT)

    def chunk(c, carry):
        rows = pl.ds(pl.multiple_of(c * L, L), L)
        y = _mlstm_chunk(q_ref[rows, :], k_ref[rows, :], v_ref[rows, :], g_ref[rows, :] + bias_ref[...],
                         gain_ref[...], ct, nb, m_row)
        o_ref[rows, :] = (_sigmoid(o_gate_ref[rows, :].astype(F32)) * y).astype(BF16)
        return carry

    lax.fori_loop(0, q_ref.shape[0] // L, chunk, 0)


def _mlstm_chunk(q, k, v, gate, gain, ct, nb, m_row):
    L = CHUNK
    W = GROUP_WIDTH
    a_full = _cumsum_rows(_log_sigmoid(gate))
    a_rows = a_full.T
    g_rows = gate.T
    causal = _tril(L)
    lane = lax.broadcasted_iota(I32, (1, LANES), 1)

    inter_q = _dot(q, ct[...].astype(BF16))
    n_q = _dot(q, nb[...].astype(BF16))
    m_old = m_row[...]
    num = jnp.zeros((L, W), F32)
    inter_cols, den_cols, ws_cols, scale_cols = [], [], [], []
    m_next = m_old
    qk = [_dot_nt(_mask_head(q, h), k) for h in range(N_HEADS)]
    decays, m_ts = [], []
    for h in range(N_HEADS):
        a_col = a_full[:, GATE_LF + h:GATE_LF + h + 1]
        i_col = gate[:, GATE_LI + h:GATE_LI + h + 1]
        a_r = a_rows[GATE_LF + h:GATE_LF + h + 1, :]
        i_r = g_rows[GATE_LI + h:GATE_LI + h + 1, :]
        a_end = a_full[L - 1:L, GATE_LF + h:GATE_LF + h + 1]
        m_in = m_old[:, h:h + 1]
        src = jnp.where(causal, jnp.broadcast_to(i_r - a_r, (L, L)), NEG_INF)
        log_inter = a_col + m_in
        m_t = jnp.maximum(a_col + jnp.max(src, axis=1, keepdims=True), log_inter)
        decays.append(jnp.exp(src + (a_col - m_t)))
        m_ts.append(m_t)
        inter_cols.append(jnp.exp(log_inter - m_t))
        g_col = a_end - a_col + i_col
        m_new = jnp.maximum(a_end + m_in, jnp.max(g_col, axis=0, keepdims=True))
        ws_cols.append(jnp.exp(g_col - m_new))
        scale_cols.append(jnp.exp(a_end + m_in - m_new))
        m_next = jnp.where(lane == h, m_new, m_next)
    for h in range(N_HEADS):
        w = qk[h] * decays[h]
        num = num + _dot(w.astype(BF16), _mask_head(v, h))
        den = jnp.sum(w, axis=1, keepdims=True) + inter_cols[h] * n_q[:, h:h + 1]
        den_cols.append(jnp.maximum(jnp.abs(den), jnp.exp(-m_ts[h])))
    n = num + _expand_heads(inter_cols) * inter_q
    head = _head_of_lane(n.shape)
    sq = n * n
    norm_cols = [lax.rsqrt(jnp.sum(jnp.where(head == h, sq, 0.0), axis=1, keepdims=True) * (1.0 / HEAD_DIM)
                           + NORM_EPS * den_cols[h] * den_cols[h]) for h in range(N_HEADS)]
    hout = n * _expand_heads(norm_cols) * gain

    kw_t = (k.astype(F32) * _expand_heads(ws_cols)).T.astype(BF16)
    scale_row = _expand_heads(scale_cols)
    same_head = _head_of_lane((W, W), 0) == _head_of_lane((W, W), 1)
    ct[...] = scale_row * ct[...] + jnp.where(same_head, _dot(kw_t, v), 0.0)
    col_is_head = _head_of_lane((W, LANES), 0) == lax.broadcasted_iota(I32, (W, LANES), 1)
    scale_n = scale_cols[N_HEADS - 1]
    for h in range(N_HEADS - 2, -1, -1):
        scale_n = jnp.where(lane == h, scale_cols[h], scale_n)
    nb[...] = scale_n * nb[...] + jnp.where(col_is_head, _dot(kw_t, jnp.ones((L, LANES), BF16)), 0.0)
    m_row[...] = m_next
    return hout


def _mlstm_mixer(proj, gates, bias_row, gain, bsz, seq, tile=MIX_TILE):
    t = proj.shape[0]
    nc = seq // tile
    row = lambda blk: pl.BlockSpec((tile, GROUP_WIDTH), lambda b, j, blk=blk: (b * nc + j, blk))
    const = lambda shape: pl.BlockSpec(shape, lambda b, j: (0,) * len(shape))
    return pl.pallas_call(
        _mlstm_kernel,
        grid=(bsz, nc),
        in_specs=[row(PB_LQ), row(PB_LK), row(PB_LV), row(PB_LO),
                  pl.BlockSpec((tile, LANES), lambda b, j: (b * nc + j, 0)),
                  const((1, LANES)), const((1, GROUP_WIDTH))],
        out_specs=pl.BlockSpec((tile, GROUP_WIDTH), lambda b, j: (b * nc + j, 0)),
        out_shape=jax.ShapeDtypeStruct((t, GROUP_WIDTH), BF16),
        scratch_shapes=[pltpu.VMEM((GROUP_WIDTH, GROUP_WIDTH), F32),
                        pltpu.VMEM((GROUP_WIDTH, LANES), F32),
                        pltpu.VMEM((1, LANES), F32)],
        compiler_params=_cparams("arbitrary", "arbitrary"),
        name="mlstm_mixer",
    )(proj, proj, proj, proj, gates, bias_row, gain)


def _outproj_router_kernel(x_ref, ya_ref, ys_ref, ym_ref, yl_ref, wo_ref, mod_ref, wn_ref, wr_ref, br_ref,
                           xo_ref, h2_ref, eid_ref, gcol_ref):
    W = GROUP_WIDTH
    tm = x_ref.shape[0]
    mod = mod_ref[0]
    out = _dot(ya_ref[...], wo_ref[0:W, :])
    out = out + _dot(ys_ref[...], wo_ref[W:2 * W, :])
    out = out + _dot(ym_ref[...], wo_ref[2 * W:3 * W, :])
    out = out + _dot(yl_ref[...], wo_ref[3 * W:4 * W, :])
    x1 = x_ref[...] + mod[2:3, :] * out
    xo_ref[...] = x1
    h2 = _rms_norm(x1, wn_ref[...]) * (1.0 + mod[4:5, :]) + mod[3:4, :]
    h2_ref[...] = _pack_bf16_pairs(h2)

    logits_t = (_dot(h2.astype(BF16), wr_ref[...]) + br_ref[...]).T
    row8 = lax.broadcasted_iota(I32, (SUBLANES, tm), 0).astype(F32)
    gl = jnp.where(row8 < N_EXPERT_GROUPS, logits_t[ROUTE_G:ROUTE_G + SUBLANES, :], NEG_INF)
    g_max = jnp.max(gl, axis=0, keepdims=True)
    g_sel = jnp.min(jnp.where(gl == g_max, row8, SUBLANES), axis=0, keepdims=True)
    g_prob = 1.0 / jnp.sum(jnp.exp(gl - g_max), axis=0, keepdims=True)
    el = logits_t[ROUTE_E:ROUTE_E + EXPERTS_PER_GROUP, :]
    for g in range(1, N_EXPERT_GROUPS):
        lo = ROUTE_E + g * EXPERTS_PER_GROUP
        el = jnp.where(g_sel == g, logits_t[lo:lo + EXPERTS_PER_GROUP, :], el)
    m1 = jnp.max(el, axis=0, keepdims=True)
    i1 = jnp.min(jnp.where(el == m1, row8, SUBLANES), axis=0, keepdims=True)
    el2 = jnp.where(row8 == i1, NEG_INF, el)
    m2 = jnp.max(el2, axis=0, keepdims=True)
    i2 = jnp.min(jnp.where(el2 == m2, row8, SUBLANES), axis=0, keepdims=True)
    ratio = jnp.exp(m2 - m1)
    p1 = 1.0 / (1.0 + ratio)
    eid_ref[0:1, :] = (g_sel * EXPERTS_PER_GROUP + i1).astype(I32)
    eid_ref[1:2, :] = (g_sel * EXPERTS_PER_GROUP + i2).astype(I32)
    rows = lax.broadcasted_iota(I32, (LANES, tm), 0)
    gate_rows = jnp.where(rows == 0, g_prob * p1, jnp.where(rows == 1, g_prob * p1 * ratio, 0.0))
    gcol_ref[...] = gate_rows.T


def _outproj_router(x2, ys, w_out, mod, w_norm2, w_route, b_route, seq, tm=512):
    t, d = x2.shape
    spb = seq // tm
    ytile = pl.BlockSpec((tm, GROUP_WIDTH), lambda i: (i, 0))
    return pl.pallas_call(
        _outproj_router_kernel,
        grid=(t // tm,),
        in_specs=[pl.BlockSpec((tm, d), lambda i: (i, 0)), ytile, ytile, ytile, ytile,
                  pl.BlockSpec((d, d), lambda i: (0, 0)),
                  pl.BlockSpec((1, 6, d), lambda i: (i // spb, 0, 0)),
                  pl.BlockSpec((1, d), lambda i: (0, 0)),
                  pl.BlockSpec((d, LANES), lambda i: (0, 0)),
                  pl.BlockSpec((1, LANES), lambda i: (0, 0))],
        out_specs=[pl.BlockSpec((tm, d), lambda i: (i, 0)),
                   pl.BlockSpec((tm, d // 2), lambda i: (i, 0)),
                   pl.BlockSpec((TOP_K, tm), lambda i: (0, i)),
                   pl.BlockSpec((tm, LANES), lambda i: (i, 0))],
        out_shape=[jax.ShapeDtypeStruct((t, d), F32),
                   jax.ShapeDtypeStruct((t, d // 2), jnp.uint32),
                   jax.ShapeDtypeStruct((TOP_K, t), I32),
                   jax.ShapeDtypeStruct((t, LANES), F32)],
        compiler_params=_cparams("arbitrary"),
        name="outproj_router",
    )(x2, *ys, w_out, mod, w_norm2, w_route, b_route)


def _rank_kernel(eid_ref, rank_ref, count_ref, carry):
    @pl.when(pl.program_id(0) == 0)
    def _():
        carry[...] = jnp.zeros_like(carry)

    tr = eid_ref.shape[1]
    expert = lax.broadcasted_iota(I32, (N_EXPERTS, tr), 0)
    before = (lax.broadcasted_iota(I32, (tr, tr), 0) < lax.broadcasted_iota(I32, (tr, tr), 1)).astype(BF16)
    base = carry[...]
    for k in range(TOP_K):
        onehot = (expert == eid_ref[k:k + 1, :]).astype(F32)
        prefix = _dot(onehot.astype(BF16), before)
        rank_ref[k:k + 1, :] = jnp.sum(onehot * (base + prefix), axis=0, keepdims=True).astype(I32)
        base = base + jnp.sum(onehot, axis=1, keepdims=True)
    carry[...] = base
    count_ref[...] = jnp.broadcast_to(base, count_ref.shape)


def _expert_ranks(eids, tr=512):
    t = eids.shape[1]
    return pl.pallas_call(
        _rank_kernel,
        grid=(t // tr,),
        in_specs=[pl.BlockSpec((TOP_K, tr), lambda i: (0, i))],
        out_specs=[pl.BlockSpec((TOP_K, tr), lambda i: (0, i)),
                   pl.BlockSpec((N_EXPERTS, LANES), lambda i: (0, 0))],
        out_shape=[jax.ShapeDtypeStruct((TOP_K, t), I32),
                   jax.ShapeDtypeStruct((N_EXPERTS, LANES), F32)],
        scratch_shapes=[pltpu.VMEM((N_EXPERTS, 1), F32)],
        compiler_params=_cparams("arbitrary"),
        name="expert_ranks",
    )(eids)


def _dest_kernel(pstart_ref, eid_ref, rank_ref, dest_ref):
    e = eid_ref[...]
    dest = rank_ref[...]
    for j in range(N_EXPERTS):
        dest = dest + jnp.where(e == j, pstart_ref[j], 0)
    dest_ref[...] = dest


def _dest_rows(p_starts, eids, ranks, tm=2048):
    t = eids.shape[1]
    grid_spec = pltpu.PrefetchScalarGridSpec(
        num_scalar_prefetch=1,
        grid=(t // tm,),
        in_specs=[pl.BlockSpec((TOP_K, tm), lambda i, ps: (0, i)),
                  pl.BlockSpec((TOP_K, tm), lambda i, ps: (0, i))],
        out_specs=pl.BlockSpec((TOP_K, tm), lambda i, ps: (0, i)))
    return pl.pallas_call(
        _dest_kernel,
        grid_spec=grid_spec,
        out_shape=jax.ShapeDtypeStruct((TOP_K, t), I32),
        compiler_params=_cparams("arbitrary"),
        name="dest_rows",
    )(p_starts, eids, ranks)


ROW_TILE = IDX_CHUNK // TOP_K


def _tile_indices(idx_hbm, idx_smem, idx_sems, tile, slot):
    return pltpu.make_async_copy(idx_hbm.at[pl.ds(tile * IDX_CHUNK, IDX_CHUNK)],
                                 idx_smem.at[pl.ds(slot * IDX_CHUNK, IDX_CHUNK)], idx_sems.at[slot])


def _dispatch_kernel(pend_ref, padded_ref, nu_ref, idx_hbm, h_ref, xb_hbm, idx_smem, zero_blk, idx_sems, row_sem,
                     zero_sem):
    def zero_block(start):
        return pltpu.make_async_copy(zero_blk, xb_hbm.at[pl.ds(pl.multiple_of(start, MOE_BLOCK), MOE_BLOCK)],
                                     zero_sem)

    @pl.when(pl.program_id(0) == 0)
    def _():
        zero_blk[...] = jnp.zeros_like(zero_blk)
        for e in range(N_EXPERTS):
            @pl.when(padded_ref[e] > 0)
            def _(e=e):
                zero_block(pend_ref[e] - MOE_BLOCK).start()
        for e in range(N_EXPERTS):
            @pl.when(padded_ref[e] > 0)
            def _(e=e):
                zero_block(pend_ref[e] - MOE_BLOCK).wait()

        def zero_unused(b, carry):
            copy = zero_block(b * MOE_BLOCK)
            copy.start()
            copy.wait()
            return carry

        lax.fori_loop(nu_ref[0], xb_hbm.shape[0] // MOE_BLOCK, zero_unused, 0)

    i = pl.program_id(0)

    @pl.when(i == 0)
    def _():
        _tile_indices(idx_hbm, idx_smem, idx_sems, i, 0).start()

    def row_copy(r, dst_row):
        return pltpu.make_async_copy(h_ref.at[pl.ds(r, 1)], xb_hbm.at[pl.ds(dst_row, 1)], row_sem)

    def step_for_slot(slot):
        _tile_indices(idx_hbm, idx_smem, idx_sems, i, slot).wait()

        @pl.when(i + 1 < pl.num_programs(0))
        def _():
            _tile_indices(idx_hbm, idx_smem, idx_sems, i + 1, 1 - slot).start()

        def issue(r, carry):
            for k in range(TOP_K):
                row_copy(r, idx_smem[slot * IDX_CHUNK + k * ROW_TILE + r]).start(priority=k)
            return carry

        lax.fori_loop(0, ROW_TILE, issue, 0, unroll=8)

    for parity in range(2):
        pl.when(i % 2 == parity)(functools.partial(step_for_slot, parity))

    def drain(r, carry):
        for k in range(TOP_K):
            row_copy(0, 0).wait()
        return carry

    lax.fori_loop(0, ROW_TILE, drain, 0, unroll=8)


def _dispatch(p_ends, padded, n_used, idx_tiles, h2, dst_rows):
    t, d = h2.shape
    any_spec = pl.BlockSpec(memory_space=pl.ANY)
    grid_spec = pltpu.PrefetchScalarGridSpec(
        num_scalar_prefetch=3,
        grid=(t // ROW_TILE,),
        in_specs=[any_spec, pl.BlockSpec((ROW_TILE, d), lambda i, pe, pd, nu: (i, 0))],
        out_specs=any_spec,
        scratch_shapes=[pltpu.SMEM((2 * IDX_CHUNK,), I32), pltpu.VMEM((MOE_BLOCK, d), h2.dtype),
                        pltpu.SemaphoreType.DMA((2,)), pltpu.SemaphoreType.DMA, pltpu.SemaphoreType.DMA])
    return pl.pallas_call(
        _dispatch_kernel,
        grid_spec=grid_spec,
        out_shape=jax.ShapeDtypeStruct((dst_rows, d), h2.dtype),
        compiler_params=_cparams("arbitrary"),
        name="moe_dispatch",
    )(p_ends, padded, n_used, idx_tiles, h2)


def _expert_kernel(be_ref, nu_ref, x_ref, wg_ref, wu_ref, wd_ref, y_ref, wg_b, wu_b, wd_b):
    b = pl.program_id(0)

    @pl.when(b < nu_ref[0])
    def _():
        @pl.when(jnp.logical_or(b == 0, be_ref[b] != be_ref[jnp.maximum(b - 1, 0)]))
        def _():
            wg_b[...] = wg_ref[0].astype(BF16)
            wu_b[...] = wu_ref[0].astype(BF16)
            wd_b[...] = wd_ref[0].astype(BF16)

        x = jnp.concatenate(_unpack_bf16_pairs(x_ref[...]), axis=1).astype(BF16)
        a = _silu(_dot(x, wg_b[...])) * _dot(x, wu_b[...])
        y_ref[...] = _pack_bf16_pairs(_dot(a.astype(BF16), wd_b[...]))

    @pl.when(b >= nu_ref[0])
    def _():
        y_ref[...] = jnp.zeros_like(y_ref)


def _expert_mlp(block_e, n_used, xb, w_gate, w_up, w_down):
    p, words = xb.shape
    d, de = w_gate.shape[1:]
    blk = lambda b, be, nu: (jnp.minimum(b, nu[0] - 1), 0)
    grid_spec = pltpu.PrefetchScalarGridSpec(
        num_scalar_prefetch=2,
        grid=(p // MOE_BLOCK,),
        in_specs=[pl.BlockSpec((MOE_BLOCK, words), blk),
                  pl.BlockSpec((1, d, de), lambda b, be, nu: (be[b], 0, 0)),
                  pl.BlockSpec((1, d, de), lambda b, be, nu: (be[b], 0, 0)),
                  pl.BlockSpec((1, de, d), lambda b, be, nu: (be[b], 0, 0))],
        out_specs=pl.BlockSpec((MOE_BLOCK, words), lambda b, be, nu: (b, 0)),
        scratch_shapes=[pltpu.VMEM((d, de), BF16), pltpu.VMEM((d, de), BF16), pltpu.VMEM((de, d), BF16)])
    return pl.pallas_call(
        _expert_kernel,
        grid_spec=grid_spec,
        out_shape=jax.ShapeDtypeStruct((p, words), jnp.uint32),
        compiler_params=_cparams("arbitrary"),
        name="expert_mlp",
    )(block_e, n_used, xb, w_gate, w_up, w_down)


def _combine_kernel(idx_hbm, yb_hbm, x_ref, gcol_ref, mod_ref, wnf_ref, o_ref, ybuf, idx_smem, idx_sems, row_sems,
                    *, final):
    i = pl.program_id(0)
    n = pl.num_programs(0)

    indices = functools.partial(_tile_indices, idx_hbm, idx_smem, idx_sems)

    def row_copy(src_row, slot, k, r):
        return pltpu.make_async_copy(yb_hbm.at[pl.ds(src_row, 1)], ybuf.at[slot, k, pl.ds(r, 1)],
                                     row_sems.at[slot])

    def issue_tile(slot):
        def issue(r, carry):
            for k in range(TOP_K):
                row_copy(idx_smem[slot * IDX_CHUNK + k * ROW_TILE + r], slot, k, r).start(priority=k)
            return carry

        lax.fori_loop(0, ROW_TILE, issue, 0, unroll=8)

    @pl.when(i == 0)
    def _():
        first = indices(i, 0)
        first.start()
        first.wait()
        issue_tile(0)

        @pl.when(n > 1)
        def _():
            indices(i + 1, 1).start()

    def step_for_slot(slot):
        other = 1 - slot

        @pl.when(i + 1 < n)
        def _():
            indices(i + 1, other).wait()

        @pl.when(i + 2 < n)
        def _():
            indices(i + 2, slot).start()

        @pl.when(i + 1 < n)
        def _():
            issue_tile(other)

        def drain(r, carry):
            for k in range(TOP_K):
                row_copy(0, slot, k, 0).wait()
            return carry

        lax.fori_loop(0, ROW_TILE, drain, 0, unroll=8)

        gc = gcol_ref[...]
        y0 = _unpack_bf16_pairs(ybuf[slot, 0])
        y1 = _unpack_bf16_pairs(ybuf[slot, 1])
        moe = jnp.concatenate([gc[:, 0:1] * y0[0] + gc[:, 1:2] * y1[0], gc[:, 0:1] * y0[1] + gc[:, 1:2] * y1[1]],
                              axis=1)
        x2 = x_ref[...] + mod_ref[0][5:6, :] * moe
        o_ref[...] = _rms_norm(x2, wnf_ref[...]) if final else x2

    for parity in range(2):
        pl.when(i % 2 == parity)(functools.partial(step_for_slot, parity))


def _combine(idx_tiles, yb, x2, gcol, mod, w_norm_final, seq, final):
    t, d = x2.shape
    tm = ROW_TILE
    spb = seq // tm
    any_spec = pl.BlockSpec(memory_space=pl.ANY)
    return pl.pallas_call(
        functools.partial(_combine_kernel, final=final),
        grid=(t // tm,),
        in_specs=[any_spec, any_spec,
                  pl.BlockSpec((tm, d), lambda i: (i, 0)),
                  pl.BlockSpec((tm, LANES), lambda i: (i, 0)),
                  pl.BlockSpec((1, 6, d), lambda i: (i // spb, 0, 0)),
                  pl.BlockSpec((1, d), lambda i: (0, 0))],
        out_specs=pl.BlockSpec((tm, d), lambda i: (i, 0)),
        out_shape=jax.ShapeDtypeStruct((t, d), F32),
        scratch_shapes=[pltpu.VMEM((2, TOP_K, tm, d // 2), jnp.uint32), pltpu.SMEM((2 * IDX_CHUNK,), I32),
                        pltpu.SemaphoreType.DMA((2,)), pltpu.SemaphoreType.DMA((2,))],
        compiler_params=_cparams("arbitrary"),
        name="moe_combine",
    )(idx_tiles, yb, x2, gcol, mod, w_norm_final)


def _lane_row(pieces, width=LANES):
    row = jnp.zeros((width,), F32)
    for off, vec in pieces.items():
        row = row.at[off:off + vec.shape[0]].set(vec.astype(F32))
    return row.reshape(1, width)


def _split_w_in(w_in):
    gw, nh = GROUP_WIDTH, N_HEADS
    widths = [gw, gw, gw, nh, gw, gw, gw, 3 * gw, nh, gw, gw, gw, gw, nh, nh]
    cuts, acc = [], 0
    for w in widths[:-1]:
        acc += w
        cuts.append(acc)
    (aq, ak, av, af, su, sv, mz, mxbc, mdt, lq, lk, lv, lo, li, lf) = jnp.split(w_in, cuts, axis=1)
    scale = HEAD_DIM ** -0.5
    wmain = jnp.concatenate([ak, su, sv, mz, mxbc, lq, lk * scale, lv, lo], axis=1).astype(BF16)
    wqt = (aq * scale).T.astype(BF16)
    wvt = av.T.astype(BF16)
    wg = jnp.zeros((w_in.shape[0], LANES), F32)
    for off, w in ((GATE_AF, af), (GATE_DT, mdt), (GATE_LI, li), (GATE_LF, lf)):
        wg = wg.at[:, off:off + nh].set(w)
    return wmain, wqt, wvt, wg.astype(BF16)


def _moe_layer(x1, h2, eids, gcol, mod_l, w_gate, w_up, w_down, layer, w_norm_final, seq, final):
    t, d = x1.shape
    ranks, counts = _expert_ranks(eids)
    counts = counts[:, 0].astype(I32)
    padded = ((counts + MOE_BLOCK - 1) // MOE_BLOCK) * MOE_BLOCK
    p_ends = jnp.cumsum(padded)
    p_starts = (p_ends - padded).astype(I32)
    n_blocks = (t * TOP_K) // MOE_BLOCK + N_EXPERTS
    blocks = jnp.arange(n_blocks, dtype=I32)
    block_e = jnp.sum((p_ends[None, :] <= (blocks * MOE_BLOCK)[:, None]).astype(I32), axis=1)
    block_e = jnp.minimum(block_e, N_EXPERTS - 1)
    n_used = (p_ends[-1:] // MOE_BLOCK).astype(I32)
    block_e = jnp.where(blocks < n_used, block_e, block_e[n_used[0] - 1])
    dest = _dest_rows(p_starts, eids, ranks)
    idx_tiles = dest.reshape(TOP_K, t // ROW_TILE, ROW_TILE).transpose(1, 0, 2).reshape(-1)
    xb = _dispatch(p_ends.astype(I32), padded.astype(I32), n_used, idx_tiles, h2, n_blocks * MOE_BLOCK)
    yb = _expert_mlp(block_e + layer * N_EXPERTS, n_used, xb, w_gate, w_up, w_down)
    return _combine(idx_tiles, yb, x1, gcol, mod_l, w_norm_final, seq, final)


def kernel(x, c, w_in, w_out, w_mix_norm, attn_f_bias, sg_w, sg_b, ssm_conv_w, ssm_conv_b, ssm_dt_bias,
           ssm_a_log, ssm_d, mlstm_i_bias, mlstm_f_bias, w_ada, b_ada, w_norm1, w_norm2, w_router_group,
           b_router_group, w_router_expert, b_router_expert, w_expert_gate, w_expert_up, w_expert_down,
           w_norm_final):
    bsz, seq, d = x.shape
    depth = w_in.shape[0]
    gw = GROUP_WIDTH
    mod = _ada_modulation(c, w_ada, b_ada).reshape(depth, bsz, 6, d)
    x2 = x.reshape(bsz * seq, d)
    wnf = w_norm_final.reshape(1, d)
    w_eg = w_expert_gate.reshape((depth * N_EXPERTS,) + w_expert_gate.shape[2:])
    w_eu = w_expert_up.reshape((depth * N_EXPERTS,) + w_expert_up.shape[2:])
    w_ed = w_expert_down.reshape((depth * N_EXPERTS,) + w_expert_down.shape[2:])
    for l in range(depth):
        wmain, wqt, wvt, wg = _split_w_in(w_in[l])
        gains = w_mix_norm[l].reshape(N_HEADS, 1, gw)
        proj, qt, vt, gates, f2 = _inproj(x2, mod[l], w_norm1[l].reshape(1, d), wmain, wqt, wvt, wg,
                                          _lane_row({GATE_AF: attn_f_bias[l]}), bsz, seq)
        y_attn = _attention(proj, qt, vt, f2, gains[0], bsz, seq)
        sg_bias_full = jnp.repeat(sg_b[l].T, HEAD_DIM, axis=1)
        y_sg = _spatial_gating(proj, gains[1], sg_w[l], sg_bias_full)
        y_ssm = _ssd_mixer(proj, gates, ssm_conv_w[l], ssm_conv_b[l].reshape(1, -1),
                           _lane_row({GATE_DT: ssm_dt_bias[l]}), _lane_row({GATE_DT: ssm_a_log[l]}),
                           jnp.repeat(ssm_d[l], HEAD_DIM).reshape(1, gw), gains[2], bsz, seq)
        y_ml = _mlstm_mixer(proj, gates, _lane_row({GATE_LI: mlstm_i_bias[l], GATE_LF: mlstm_f_bias[l]}),
                            gains[3], bsz, seq)
        w_route = jnp.zeros((d, LANES), F32)
        w_route = w_route.at[:, ROUTE_G:ROUTE_G + N_EXPERT_GROUPS].set(w_router_group[l])
        w_route = w_route.at[:, ROUTE_E:ROUTE_E + N_EXPERTS].set(w_router_expert[l]).astype(BF16)
        b_route = _lane_row({ROUTE_G: b_router_group[l], ROUTE_E: b_router_expert[l]})
        x1, h2, eids, gcol = _outproj_router(x2, (y_attn, y_sg, y_ssm, y_ml), w_out[l].astype(BF16), mod[l],
                                             w_norm2[l].reshape(1, d), w_route, b_route, seq)
        x2 = _moe_layer(x1, h2, eids, gcol, mod[l], w_eg, w_eu, w_ed, l, wnf, seq, final=(l == depth - 1))
    return x2.reshape(bsz, seq, d)
```
